```python
import math
import jax
import jax.numpy as jnp
from jax import lax
import numpy as np

D_MODEL = 1024
BATCH = 2
SEQ = 8192
DEPTH = 2

HEAD_DIM = 64
N_META = 16
M_HEADS = 4
M_WIDTH = M_HEADS * HEAD_DIM
M_CHUNK = 64
M_NORM_EPS = 1e-6
R_HEADS = 4
R_WIDTH = R_HEADS * HEAD_DIM
R_DECAY_RANK = 32
R_A_RANK = 32
R_GATE_RANK = 64
R_GN_EPS = 64e-5
A_HEADS = 8
A_KV_HEADS = 2
A_GROUP = A_HEADS // A_KV_HEADS
A_WIDTH = A_HEADS * HEAD_DIM
A_KV_WIDTH = A_KV_HEADS * HEAD_DIM
WINDOW = 128
A_BLOCK = 128
N_BUCKETS = 32
MAX_DISTANCE = 128
MIX_WIDTH = M_WIDTH + R_WIDTH + A_WIDTH
D_FF = int(math.ceil(8 * D_MODEL / 3 / 256)) * 256
DEEPNORM_ALPHA = (2 * DEPTH) ** 0.25
DEEPNORM_BETA = (8 * DEPTH) ** -0.25
LN_EPS = 1e-5
NEG = -1e30
IN_SIZES = (A_WIDTH, A_KV_WIDTH, A_KV_WIDTH,
            M_WIDTH, M_WIDTH, M_WIDTH, M_WIDTH,
            M_HEADS, M_HEADS,
            R_WIDTH, R_WIDTH, R_WIDTH,
            R_DECAY_RANK, R_A_RANK, R_GATE_RANK)
IN_DIM = sum(IN_SIZES)
IN_SPLITS = tuple(int(s) for s in np.cumsum(IN_SIZES)[:-1])

kernel_name = 'hybrid_mlstm_rwkv7_swa_deepnorm'


def layer_norm(x, g, b, eps=LN_EPS):
    xf = x.astype(jnp.float32)
    mu = jnp.mean(xf, axis=-1, keepdims=True)
    var = jnp.mean(jnp.square(xf - mu), axis=-1, keepdims=True)
    return ((xf - mu) * lax.rsqrt(var + eps) * g + b).astype(x.dtype)


def head_norm(y, eps):
    mu = jnp.mean(y, axis=-1, keepdims=True)
    var = jnp.mean(jnp.square(y - mu), axis=-1, keepdims=True)
    return (y - mu) * lax.rsqrt(var + eps)


def token_shift_lerp(t, mu):
    prev = jnp.pad(t, ((0, 0), (1, 0), (0, 0)))[:, :-1]
    return t + (prev - t) * mu


def t5_bucket(dist):
    max_exact = N_BUCKETS // 2
    d = jnp.maximum(dist, 1).astype(jnp.float32)
    large = max_exact + (jnp.log(d / max_exact) / math.log(MAX_DISTANCE / max_exact)
                         * (N_BUCKETS - max_exact)).astype(jnp.int32)
    large = jnp.minimum(large, N_BUCKETS - 1)
    return jnp.where(dist < max_exact, dist, large)


def mlstm(q, k, v, o_pre, i_pre, f_pre, norm_w):
    B, L, _ = q.shape
    pad = M_CHUNK - N_META
    Lp = L + pad
    NC = Lp // M_CHUNK

    def heads(t):
        t = t.astype(jnp.float32).reshape(B, L, M_HEADS, HEAD_DIM)
        t = jnp.pad(t, ((0, 0), (pad, 0), (0, 0), (0, 0)))
        return t.reshape(B, NC, M_CHUNK, M_HEADS, HEAD_DIM).transpose(1, 0, 3, 2, 4)

    def gates(t, fill):
        t = jnp.pad(t.astype(jnp.float32), ((0, 0), (pad, 0), (0, 0)), constant_values=fill)
        return t.reshape(B, NC, M_CHUNK, M_HEADS).transpose(1, 0, 3, 2)

    qc = heads(q)
    kc = heads(k) * (HEAD_DIM ** -0.5)
    vc = heads(v)
    log_f = gates(jax.nn.log_sigmoid(f_pre.astype(jnp.float32)), 0.0)
    log_i = gates(i_pre, NEG)
    causal = jnp.tril(jnp.ones((M_CHUNK, M_CHUNK), dtype=bool))

    def step(carry, inp):
        C, n, m = carry
        qt, kt, vt, lf, li = inp
        b = jnp.cumsum(lf, axis=-1)
        D = jnp.where(causal, b[..., :, None] - b[..., None, :] + li[..., None, :], NEG)
        inter = b + m[..., None]
        m_t = jnp.maximum(jnp.max(D, axis=-1), inter)
        Dw = jnp.where(causal, jnp.exp(D - m_t[..., None]), 0.0)
        S = jnp.einsum('bhtd,bhsd->bhts', qt, kt) * Dw
        s_inter = jnp.exp(inter - m_t)
        num = (jnp.einsum('bhts,bhsd->bhtd', S, vt)
               + s_inter[..., None] * jnp.einsum('bhvk,bhtk->bhtv', C, qt))
        den = jnp.sum(S, axis=-1) + s_inter * jnp.einsum('bhk,bhtk->bht', n, qt)
        h = num / jnp.maximum(jnp.abs(den), jnp.exp(-m_t))[..., None]
        g = b[..., -1]
        a = g[..., None] - b + li
        m_new = jnp.maximum(g + m, jnp.max(a, axis=-1))
        wa = jnp.exp(a - m_new[..., None])
        decay = jnp.exp(g + m - m_new)
        C_new = decay[..., None, None] * C + jnp.einsum('bhs,bhsv,bhsk->bhvk', wa, vt, kt)
        n_new = decay[..., None] * n + jnp.einsum('bhs,bhsk->bhk', wa, kt)
        return (C_new, n_new, m_new), h

    carry0 = (jnp.zeros((B, M_HEADS, HEAD_DIM, HEAD_DIM), jnp.float32),
              jnp.zeros((B, M_HEADS, HEAD_DIM), jnp.float32),
              jnp.full((B, M_HEADS), NEG, jnp.float32))
    _, h = lax.scan(step, carry0, (qc, kc, vc, log_f, log_i))
    h = h.transpose(1, 0, 3, 2, 4).reshape(B, Lp, M_HEADS, HEAD_DIM)[:, pad:]
    h = head_norm(h, M_NORM_EPS).reshape(B, L, M_WIDTH) * norm_w
    return h * jax.nn.sigmoid(o_pre.astype(jnp.float32))


def rwkv7(r, k, v, w_lat, a_lat, g_lat, mu_rkv, mu_w, mu_a, mu_g, w0, w2, a0, a2, g2,
          k_k, k_a, r_k, ln_x):
    B, L, _ = r.shape
    f32 = lambda t: t.astype(jnp.float32)
    r = token_shift_lerp(f32(r), mu_rkv[0])
    k = token_shift_lerp(f32(k), mu_rkv[1])
    v = token_shift_lerp(f32(v), mu_rkv[2])
    w_lat = token_shift_lerp(f32(w_lat), mu_w)
    a_lat = token_shift_lerp(f32(a_lat), mu_a)
    g_lat = token_shift_lerp(f32(g_lat), mu_g)
    w_log = -jax.nn.softplus(-(w0 + jnp.tanh(w_lat) @ w2)) - 0.5
    decay = jnp.exp(-jnp.exp(w_log))
    a = jax.nn.sigmoid(a0 + a_lat @ a2)
    g = jax.nn.sigmoid(g_lat) @ g2
    hd = lambda t: t.reshape(B, L, R_HEADS, HEAD_DIM)
    kk = hd(k * k_k)
    kk = kk / jnp.maximum(jnp.sqrt(jnp.sum(kk * kk, axis=-1, keepdims=True)), 1e-12)
    k = k * (1.0 + (a - 1.0) * k_a)
    r_h, k_h, v_h, w_h, a_h = hd(r), hd(k), hd(v), hd(decay), hd(a)

    def step(S, inp):
        rt, wt, kt, vt, at, bt = inp
        sa = jnp.einsum('bhvk,bhk->bhv', S, at)
        S = S * wt[:, :, None, :] + sa[..., None] * bt[:, :, None, :] + vt[..., None] * kt[:, :, None, :]
        return S, jnp.einsum('bhvk,bhk->bhv', S, rt)

    tm = lambda t: jnp.moveaxis(t, 1, 0)
    S0 = jnp.zeros((B, R_HEADS, HEAD_DIM, HEAD_DIM), jnp.float32)
    _, y = lax.scan(step, S0, (tm(r_h), tm(w_h), tm(k_h), tm(v_h), tm(-kk), tm(kk * a_h)))
    y = jnp.moveaxis(y, 0, 1)
    y = head_norm(y, R_GN_EPS).reshape(B, L, R_WIDTH) * ln_x[0] + ln_x[1]
    bonus = jnp.sum(r_h * k_h * r_k.reshape(R_HEADS, HEAD_DIM), axis=-1, keepdims=True) * v_h
    return (y + bonus.reshape(B, L, R_WIDTH)) * g


def sliding_window_attention(q, k, v, sinks, rel_bias):
    B, L, _ = q.shape
    pad = A_BLOCK - N_META
    Lp = L + pad
    NB = Lp // A_BLOCK
    q = q.reshape(B, L, A_KV_HEADS, A_GROUP, HEAD_DIM)
    k = k.reshape(B, L, A_KV_HEADS, HEAD_DIM)
    v = v.reshape(B, L, A_KV_HEADS, HEAD_DIM)
    k_meta, v_meta = k[:, :N_META], v[:, :N_META]
    qb = jnp.pad(q, ((0, 0), (pad, 0), (0, 0), (0, 0), (0, 0))).reshape(
        B, NB, A_BLOCK, A_KV_HEADS, A_GROUP, HEAD_DIM)
    kb = jnp.pad(k, ((0, 0), (pad + A_BLOCK, 0), (0, 0), (0, 0))).reshape(
        B, NB + 1, A_BLOCK, A_KV_HEADS, HEAD_DIM)
    vb = jnp.pad(v, ((0, 0), (pad + A_BLOCK, 0), (0, 0), (0, 0))).reshape(
        B, NB + 1, A_BLOCK, A_KV_HEADS, HEAD_DIM)
    k_band = jnp.concatenate([kb[:, :-1], kb[:, 1:]], axis=2)
    v_band = jnp.concatenate([vb[:, :-1], vb[:, 1:]], axis=2)
    scale = HEAD_DIM ** -0.5
    s_band = jnp.einsum('bnikgd,bnjkd->bkgnij', qb, k_band).astype(jnp.float32) * scale
    s_meta = jnp.einsum('bnikgd,bmkd->bkgnim', qb, k_meta).astype(jnp.float32) * scale

    q_pos = jnp.arange(NB)[:, None] * A_BLOCK + jnp.arange(A_BLOCK)[None, :] - pad
    kb_pos = jnp.arange(NB)[:, None] * A_BLOCK - A_BLOCK + jnp.arange(2 * A_BLOCK)[None, :] - pad
    dist_band = A_BLOCK + jnp.arange(A_BLOCK)[:, None] - jnp.arange(2 * A_BLOCK)[None, :]
    band_ok = ((dist_band >= 0) & (dist_band < WINDOW))[None] & (kb_pos[:, None, :] >= N_META)
    dist_meta = q_pos[:, :, None] - jnp.arange(N_META)[None, None, :]
    meta_ok = dist_meta >= 0
    rb = rel_bias.astype(jnp.float32)
    bias_band = rb[:, t5_bucket(jnp.maximum(dist_band, 0))].reshape(
        A_KV_HEADS, A_GROUP, 1, A_BLOCK, 2 * A_BLOCK)
    bias_meta = rb[:, t5_bucket(jnp.maximum(dist_meta, 0))].reshape(
        A_KV_HEADS, A_GROUP, NB, A_BLOCK, N_META)
    s_band = jnp.where(band_ok, s_band + bias_band, NEG)
    s_meta = jnp.where(meta_ok, s_meta + bias_meta, NEG)
    sink = jnp.broadcast_to(sinks.astype(jnp.float32).reshape(A_KV_HEADS, A_GROUP, 1, 1, 1),
                            (B, A_KV_HEADS, A_GROUP, NB, A_BLOCK, 1))
    p = jax.nn.softmax(jnp.concatenate([s_meta, s_band, sink], axis=-1), axis=-1)
    p_meta = p[..., :N_META].astype(v.dtype)
    p_band = p[..., N_META:N_META + 2 * A_BLOCK].astype(v.dtype)
    out = (jnp.einsum('bkgnim,bmkd->bnikgd', p_meta, v_meta)
           + jnp.einsum('bkgnij,bnjkd->bnikgd', p_band, v_band))
    return out.reshape(B, Lp, A_WIDTH)[:, pad:]


def setup_inputs(seed: int = 0) -> dict:
    key = jax.random.key(seed)
    keys = jax.random.split(key, 32)
    cnt = [0]

    def nk():
        cnt[0] += 1
        return keys[cnt[0] - 1]

    nrm = lambda shape, s: jax.random.normal(nk(), shape, jnp.float32) * s
    uni = lambda shape, lo, hi: jax.random.uniform(nk(), shape, jnp.float32, lo, hi)
    ln_pair = lambda: jnp.stack([1.0 + nrm((DEPTH, D_MODEL), 0.02), nrm((DEPTH, D_MODEL), 0.02)], axis=1)
    return {
        'x': nrm((BATCH, SEQ, D_MODEL), 1.0),
        'meta_tokens': nrm((N_META, D_MODEL), 1.0),
        'rel_bias': nrm((A_HEADS, N_BUCKETS), 0.5),
        'w_in': nrm((DEPTH, D_MODEL, IN_DIM), D_MODEL ** -0.5),
        'm_gate_bias': jnp.stack([nrm((DEPTH, M_HEADS), 0.5), uni((DEPTH, M_HEADS), 3.0, 6.0)], axis=1),
        'm_norm_w': 1.0 + nrm((DEPTH, M_WIDTH), 0.02),
        'r_mu_rkv': uni((DEPTH, 3, R_WIDTH), 0.0, 1.0),
        'r_mu_w': uni((DEPTH, R_DECAY_RANK), 0.0, 1.0),
        'r_mu_a': uni((DEPTH, R_A_RANK), 0.0, 1.0),
        'r_mu_g': uni((DEPTH, R_GATE_RANK), 0.0, 1.0),
        'r_w0': uni((DEPTH, R_WIDTH), -6.5, 0.0),
        'r_w2': nrm((DEPTH, R_DECAY_RANK, R_WIDTH), 0.1 * R_DECAY_RANK ** -0.5),
        'r_a0': nrm((DEPTH, R_WIDTH), 0.1),
        'r_a2': nrm((DEPTH, R_A_RANK, R_WIDTH), 0.1 * R_A_RANK ** -0.5),
        'r_g2': nrm((DEPTH, R_GATE_RANK, R_WIDTH), R_GATE_RANK ** -0.5),
        'r_k_k': 0.85 + nrm((DEPTH, R_WIDTH), 0.02),
        'r_k_a': 1.0 + nrm((DEPTH, R_WIDTH), 0.02),
        'r_r_k': nrm((DEPTH, R_WIDTH), 0.1),
        'r_ln_x': jnp.stack([1.0 + nrm((DEPTH, R_WIDTH), 0.02), nrm((DEPTH, R_WIDTH), 0.02)], axis=1),
        'a_sinks': nrm((DEPTH, A_HEADS), 0.5),
        'w_out': nrm((DEPTH, MIX_WIDTH, D_MODEL), DEEPNORM_BETA * MIX_WIDTH ** -0.5),
        'ln_mix': ln_pair(),
        'w_ff_in': nrm((DEPTH, D_MODEL, 2 * D_FF), D_MODEL ** -0.5),
        'w_ff_out': nrm((DEPTH, D_FF, D_MODEL), DEEPNORM_BETA * D_FF ** -0.5),
        'ln_ffn': ln_pair(),
    }


def reference(x, meta_tokens, rel_bias, w_in, m_gate_bias, m_norm_w, r_mu_rkv, r_mu_w, r_mu_a,
              r_mu_g, r_w0, r_w2, r_a0, r_a2, r_g2, r_k_k, r_k_a, r_r_k, r_ln_x, a_sinks, w_out,
              ln_mix, w_ff_in, w_ff_out, ln_ffn):
    B = x.shape[0]
    meta = jnp.broadcast_to(meta_tokens[None].astype(x.dtype), (B, N_META, D_MODEL))
    h = jnp.concatenate([meta, x], axis=1)
    for l in range(DEPTH):
        proj = h @ w_in[l]
        (aq, ak, av, mq, mk, mv, mo, mi, mf,
         rr, rk, rv, rw, ra, rg) = jnp.split(proj, IN_SPLITS, axis=-1)
        y_m = mlstm(mq, mk, mv, mo, mi + m_gate_bias[l, 0], mf + m_gate_bias[l, 1], m_norm_w[l])
        y_r = rwkv7(rr, rk, rv, rw, ra, rg, r_mu_rkv[l], r_mu_w[l], r_mu_a[l], r_mu_g[l],
                    r_w0[l], r_w2[l], r_a0[l], r_a2[l], r_g2[l], r_k_k[l], r_k_a[l], r_r_k[l],
                    r_ln_x[l])
        y_a = sliding_window_attention(aq, ak, av, a_sinks[l], rel_bias)
        mix = jnp.concatenate([y_m.astype(h.dtype), y_r.astype(h.dtype), y_a.astype(h.dtype)],
                              axis=-1) @ w_out[l]
        h = layer_norm(DEEPNORM_ALPHA * h + mix, ln_mix[l, 0], ln_mix[l, 1])
        gate, up = jnp.split(h @ w_ff_in[l], 2, axis=-1)
        ffn = (jax.nn.silu(gate) * up) @ w_ff_out[l]
        h = layer_norm(DEEPNORM_ALPHA * h + ffn, ln_ffn[l, 0], ln_ffn[l, 1])
    return h[:, N_META:]
```

```python
import functools
import math

import jax
import jax.numpy as jnp
from jax import lax
from jax.experimental import pallas as pl
from jax.experimental.pallas import tpu as pltpu

F32 = jnp.float32
BF16 = jnp.bfloat16

HEAD_DIM = 64
N_META = 16
M_HEADS = 4
M_WIDTH = M_HEADS * HEAD_DIM
M_CHUNK = 64
M_NORM_EPS = 1e-6
R_HEADS = 4
R_WIDTH = R_HEADS * HEAD_DIM
R_CHUNK = 64
R_DECAY_RANK = 32
R_A_RANK = 32
R_GATE_RANK = 64
R_GN_EPS = 64e-5
A_HEADS = 8
A_KV_HEADS = 2
A_GROUP = A_HEADS // A_KV_HEADS
A_WIDTH = A_HEADS * HEAD_DIM
A_KV_WIDTH = A_KV_HEADS * HEAD_DIM
WINDOW = 128
A_BLOCK = 128
N_BUCKETS = 32
MAX_DISTANCE = 128
LN_EPS = 1e-5
NEG = -1e30

LANES = 128
FRONT_PAD = A_BLOCK - N_META
TOKEN_START = FRONT_PAD + N_META

PA_W = A_WIDTH + 2 * A_KV_WIDTH
PM_W = 4 * M_WIDTH + LANES
PR_W = 3 * R_WIDTH + R_DECAY_RANK + R_A_RANK + R_GATE_RANK
M_RAW = 4 * M_WIDTH + 2 * M_HEADS

VMEM_LIMIT = 56 * 1024 * 1024


def _row_tile(n):
    for t in (640, 512, 256, 128, 64):
        if n % t == 0:
            return t
    raise ValueError(f"row count {n} has no supported tile")


def _cparams(sem):
    return pltpu.CompilerParams(dimension_semantics=sem, vmem_limit_bytes=VMEM_LIMIT)


_HI = lax.Precision.HIGHEST


def _mm(a, b):
    return lax.dot_general(a, b, (((1,), (0,)), ((), ())), precision=_HI,
                           preferred_element_type=F32)


def _mm_nt(a, b):
    return lax.dot_general(a, b, (((1,), (1,)), ((), ())), precision=_HI,
                           preferred_element_type=F32)


def _mm_tn(a, b):
    return lax.dot_general(a, b, (((0,), (0,)), ((), ())), precision=_HI,
                           preferred_element_type=F32)


def _softplus(x):
    return jnp.maximum(x, 0.0) + jnp.log1p(jnp.exp(-jnp.abs(x)))


def _iota2(shape, dim):
    return lax.broadcasted_iota(jnp.int32, shape, dim)


def _proj_kernel(x_ref, w_ref, oa_ref, om_ref, or_ref):
    acc = jnp.dot(x_ref[...].astype(BF16), w_ref[...], preferred_element_type=F32)
    oa_ref[...] = acc[:, :PA_W]
    om_ref[...] = acc[:, PA_W:PA_W + PM_W]
    or_ref[...] = acc[:, PA_W + PM_W:]


def _proj(h, w):
    b, lp, d = h.shape
    tm = _row_tile(lp)
    n = w.shape[1]
    spec = lambda width: pl.BlockSpec((None, tm, width), lambda i, j: (i, j, 0))
    return pl.pallas_call(
        _proj_kernel,
        grid=(b, lp // tm),
        in_specs=[spec(d),
                  pl.BlockSpec((d, n), lambda i, j: (0, 0), pipeline_mode=pl.Buffered(1))],
        out_specs=[spec(PA_W), spec(PM_W), spec(PR_W)],
        out_shape=[jax.ShapeDtypeStruct((b, lp, PA_W), F32),
                   jax.ShapeDtypeStruct((b, lp, PM_W), F32),
                   jax.ShapeDtypeStruct((b, lp, PR_W), F32)],
        compiler_params=_cparams(("arbitrary", "arbitrary")),
        name="in_proj",
    )(h, w)


def _mlstm_kernel(pm_ref, bias_ref, nw_ref, out_ref, cst_ref, m_ref):
    c = pl.program_id(1)
    ch = M_CHUNK

    @pl.when(c == 0)
    def _():
        cst_ref[...] = jnp.zeros_like(cst_ref)
        m_ref[...] = jnp.full_like(m_ref, NEG)

    x = pm_ref[...]
    row = c * ch + _iota2((ch, 1), 0)
    is_pad = row < FRONT_PAD
    gb = x[:, 4 * M_WIDTH:] + bias_ref[...]
    lane = _iota2((ch, LANES), 1)
    li = jnp.where(is_pad, NEG, gb)
    lf = jnp.where(is_pad, 0.0, -_softplus(-gb))
    gcol = jnp.where(lane < M_HEADS, li, jnp.where(lane < 2 * M_HEADS, lf, 0.0))
    grow = gcol.T
    r_i = _iota2((ch, ch), 0)
    c_i = _iota2((ch, ch), 1)
    causal = r_i >= c_i
    tri = causal.astype(F32)
    bcol = _mm(tri, gcol)
    brow = _mm_nt(grow[0:8, :], tri)
    ones_col = (_iota2((ch, HEAD_DIM), 1) == 0).astype(F32)

    outs = []
    for h in range(M_HEADS):
        sl = slice(h * HEAD_DIM, (h + 1) * HEAD_DIM)
        q = x[:, sl]
        k = x[:, M_WIDTH + h * HEAD_DIM:M_WIDTH + (h + 1) * HEAD_DIM] * (HEAD_DIM ** -0.5)
        v = x[:, 2 * M_WIDTH + h * HEAD_DIM:2 * M_WIDTH + (h + 1) * HEAD_DIM]
        o_pre = x[:, 3 * M_WIDTH + h * HEAD_DIM:3 * M_WIDTH + (h + 1) * HEAD_DIM]
        b_c = bcol[:, M_HEADS + h:M_HEADS + h + 1]
        b_r = brow[M_HEADS + h:M_HEADS + h + 1, :]
        li_r = grow[h:h + 1, :]
        li_c = gcol[:, h:h + 1]
        m_prev = m_ref[h][0:1, 0:1]

        d_mat = jnp.where(causal, b_c - b_r + li_r, NEG)
        inter = b_c + m_prev
        m_t = jnp.maximum(jnp.max(d_mat, axis=-1, keepdims=True), inter)
        dw = jnp.where(causal, jnp.exp(d_mat - m_t), 0.0)
        s = _mm_nt(q, k) * dw
        s_inter = jnp.exp(inter - m_t)
        vext = jnp.concatenate([v, ones_col], axis=1)
        cext = cst_ref[h]
        tot = _mm(s, vext) + s_inter * _mm_nt(q, cext)
        num = tot[:, :HEAD_DIM]
        den = tot[:, HEAD_DIM:HEAD_DIM + 1]
        hh = num / jnp.maximum(jnp.abs(den), jnp.exp(-m_t))

        g = b_c[ch - 1:ch, :]
        a_c = g - b_c + li_c
        m_new = jnp.maximum(g + m_prev, jnp.max(a_c, axis=0, keepdims=True))
        wa = jnp.exp(a_c - m_new)
        decay = jnp.exp(g + m_prev - m_new)
        cst_ref[h] = decay * cext + _mm_tn(vext, wa * k)
        m_ref[h] = jnp.broadcast_to(m_new, m_ref.shape[1:])

        mu = jnp.mean(hh, axis=-1, keepdims=True)
        var = jnp.mean(jnp.square(hh - mu), axis=-1, keepdims=True)
        hn = (hh - mu) * lax.rsqrt(var + M_NORM_EPS)
        outs.append(hn * nw_ref[:, sl] * jax.nn.sigmoid(o_pre))
    out_ref[...] = jnp.concatenate(outs, axis=1)


def _mlstm(pm, gate_bias, norm_w):
    b, lp, _ = pm.shape
    ch = M_CHUNK
    bias = jnp.zeros((1, LANES), F32).at[0, :2 * M_HEADS].set(gate_bias.reshape(-1))
    return pl.pallas_call(
        _mlstm_kernel,
        grid=(b, lp // ch),
        in_specs=[pl.BlockSpec((None, ch, PM_W), lambda i, j: (i, j, 0)),
                  pl.BlockSpec((1, LANES), lambda i, j: (0, 0)),
                  pl.BlockSpec((1, M_WIDTH), lambda i, j: (0, 0))],
        out_specs=pl.BlockSpec((None, ch, M_WIDTH), lambda i, j: (i, j, 0)),
        out_shape=jax.ShapeDtypeStruct((b, lp, M_WIDTH), F32),
        scratch_shapes=[pltpu.VMEM((M_HEADS, 2 * HEAD_DIM, HEAD_DIM), F32),
                        pltpu.VMEM((M_HEADS, 8, LANES), F32)],
        compiler_params=_cparams(("arbitrary", "arbitrary")),
        name="mlstm",
    )(pm, bias, norm_w.reshape(1, M_WIDTH))


def _rwkv_kernel(pr_ref, prev_ref, mu_ref, w0_ref, w2_ref, a0_ref, a2_ref, g2_ref,
                 kk_ref, ka_ref, rk_ref, lnx_ref, out_ref, st_ref):
    c = pl.program_id(1)
    ch = R_CHUNK
    rw = R_WIDTH

    @pl.when(c == 0)
    def _():
        st_ref[...] = jnp.zeros_like(st_ref)

    x = pr_ref[...]
    last = jnp.where(c == 0, 0.0, prev_ref[7:8, :])
    prev = jnp.where(_iota2((ch, 1), 0) == 0, last, pltpu.roll(x, 1, axis=0))
    t = x + (prev - x) * mu_ref[...]
    r = t[:, 0:rw]
    k = t[:, rw:2 * rw]
    v = t[:, 2 * rw:3 * rw]
    o = 3 * rw
    w_lat = t[:, o:o + R_DECAY_RANK]
    a_lat = t[:, o + R_DECAY_RANK:o + R_DECAY_RANK + R_A_RANK]
    g_lat = t[:, o + R_DECAY_RANK + R_A_RANK:]

    w_log = -_softplus(-(w0_ref[...] + _mm(jnp.tanh(w_lat), w2_ref[...]))) - 0.5
    lw = -jnp.exp(w_log)
    a = jax.nn.sigmoid(a0_ref[...] + _mm(a_lat, a2_ref[...]))
    g = _mm(jax.nn.sigmoid(g_lat), g2_ref[...])

    head_shift = HEAD_DIM.bit_length() - 1
    head_ones = (jnp.right_shift(_iota2((rw, rw), 0), head_shift)
                 == jnp.right_shift(_iota2((rw, rw), 1), head_shift)).astype(F32)
    kk = k * kk_ref[...]
    kk = kk / jnp.maximum(jnp.sqrt(_mm(kk * kk, head_ones)), 1e-12)
    k2 = k * (1.0 + (a - 1.0) * ka_ref[...])
    avec = -kk
    bvec = kk * a

    r_i = _iota2((ch, ch), 0)
    c_i = _iota2((ch, ch), 1)
    incl = r_i >= c_i
    strict = r_i > c_i
    eye = r_i == c_i
    cum = _mm(incl.astype(F32), lw)
    w_incl = jnp.exp(cum)
    inv = jnp.exp(-cum)
    cl = cum[ch - 1:ch, :]
    rel = jnp.exp(cl - cum)
    ah_all = avec * jnp.exp(cum - lw)
    rh_all = r * w_incl
    kt_all = k2 * inv
    bt_all = bvec * inv
    kp_all = k2 * rel
    bp_all = bvec * rel
    wc_all = jnp.exp(cl)

    ys = []
    for h in range(R_HEADS):
        sl = slice(h * HEAD_DIM, (h + 1) * HEAD_DIM)
        ah, rh, kt, bt, kp, bp, vh = (ah_all[:, sl], rh_all[:, sl], kt_all[:, sl], bt_all[:, sl],
                                      kp_all[:, sl], bp_all[:, sl], v[:, sl])
        m_ab = jnp.where(strict, _mm_nt(ah, bt), 0.0)
        m_ak = jnp.where(strict, _mm_nt(ah, kt), 0.0)
        n_rk = jnp.where(incl, _mm_nt(rh, kt), 0.0)
        n_rb = jnp.where(incl, _mm_nt(rh, bt), 0.0)
        xinv = jnp.where(eye, 1.0, m_ab)
        pw = m_ab
        for _ in range(int(math.log2(ch)) - 1):
            pw = _mm(pw, pw)
            xinv = xinv + _mm(pw, xinv)
        a_eff = _mm(xinv, ah)
        p = _mm(xinv, _mm(m_ak, vh))
        r_eff = rh + _mm(n_rb, a_eff)
        y_in = _mm(n_rk, vh) + _mm(n_rb, p)
        t_mat = jnp.where(eye, wc_all[:, sl], 0.0) + _mm_tn(bp, a_eff)
        g_mat = _mm_tn(kp, vh) + _mm_tn(bp, p)
        st = st_ref[h]
        ys.append(_mm(r_eff, st) + y_in)
        st_ref[h] = _mm(t_mat, st) + g_mat
    y = jnp.concatenate(ys, axis=1)

    inv_d = 1.0 / HEAD_DIM
    mu = _mm(y, head_ones) * inv_d
    yc = y - mu
    var = _mm(yc * yc, head_ones) * inv_d
    yn = yc * lax.rsqrt(var + R_GN_EPS) * lnx_ref[0:1, :] + lnx_ref[1:2, :]
    bonus = _mm(r * k2 * rk_ref[...], head_ones) * v
    out_ref[...] = (yn + bonus) * g


def _rwkv(pr, mu, w0, w2, a0, a2, g2, k_k, k_a, r_k, ln_x):
    b, lp, _ = pr.shape
    ch = R_CHUNK
    per8 = ch // 8
    row = lambda a: a.reshape(1, -1)
    full = lambda a: pl.BlockSpec(a.shape, lambda i, j: (0,) * a.ndim)
    params = [row(mu), row(w0), w2, row(a0), a2, g2, row(k_k), row(k_a), row(r_k), ln_x]
    return pl.pallas_call(
        _rwkv_kernel,
        grid=(b, lp // ch),
        in_specs=[pl.BlockSpec((None, ch, PR_W), lambda i, j: (i, j, 0)),
                  pl.BlockSpec((None, 8, PR_W), lambda i, j: (i, jnp.maximum(j * per8 - 1, 0), 0))]
                 + [full(a) for a in params],
        out_specs=pl.BlockSpec((None, ch, R_WIDTH), lambda i, j: (i, j, 0)),
        out_shape=jax.ShapeDtypeStruct((b, lp, R_WIDTH), F32),
        scratch_shapes=[pltpu.VMEM((R_HEADS, HEAD_DIM, HEAD_DIM), F32)],
        compiler_params=_cparams(("arbitrary", "arbitrary")),
        name="rwkv7",
    )(pr, pr, *params)


def _t5_bucket(dist):
    max_exact = N_BUCKETS // 2
    d = jnp.maximum(dist, 1).astype(F32)
    large = max_exact + (jnp.log(d / max_exact) / math.log(MAX_DISTANCE / max_exact)
                         * (N_BUCKETS - max_exact)).astype(jnp.int32)
    large = jnp.minimum(large, N_BUCKETS - 1)
    return jnp.where(dist < max_exact, dist, large)


def _attn_tables(rel_bias):
    blk = A_BLOCK
    i = jnp.arange(blk)[:, None]
    j = jnp.arange(blk)[None, :]
    rb = rel_bias.astype(F32)
    bias = lambda dist: rb[:, _t5_bucket(jnp.maximum(dist, 0))]
    neg = jnp.full((A_HEADS, blk, blk), NEG, F32)
    is_meta = j >= FRONT_PAD
    cur_d = i - j
    prev_d = blk + i - j
    cur = jnp.where(cur_d >= 0, bias(cur_d), NEG)
    prev = jnp.where(prev_d < WINDOW, bias(prev_d), NEG)
    meta0 = jnp.where(is_meta & (cur_d >= 0), bias(cur_d), NEG)
    meta1 = jnp.where(is_meta, bias(prev_d), NEG)
    meta2 = jnp.where(is_meta, bias(2 * blk + i - j), NEG)
    case0 = jnp.concatenate([meta0, neg, neg], axis=-1)
    case1 = jnp.concatenate([meta1, neg, cur], axis=-1)
    case2 = jnp.concatenate([meta2, prev, cur], axis=-1)
    return jnp.stack([case0, case1, case2], axis=0)


def _swa_kernel(q_ref, kc_ref, kp_ref, km_ref, vc_ref, vp_ref, vm_ref, tab_ref, sink_ref, out_ref):
    scale = HEAD_DIM ** -0.5
    outs = []
    for kv in range(A_KV_HEADS):
        ks = slice(kv * HEAD_DIM, (kv + 1) * HEAD_DIM)
        kcat = jnp.concatenate([km_ref[:, ks], kp_ref[:, ks], kc_ref[:, ks]], axis=0).astype(BF16)
        vcat = jnp.concatenate([vm_ref[:, ks], vp_ref[:, ks], vc_ref[:, ks]], axis=0).astype(BF16)
        for gi in range(A_GROUP):
            h = kv * A_GROUP + gi
            q = q_ref[:, h * HEAD_DIM:(h + 1) * HEAD_DIM].astype(BF16)
            s = lax.dot_general(q, kcat, (((1,), (1,)), ((), ())), preferred_element_type=F32) * scale
            tab = tab_ref[h]
            s = jnp.where(tab > 0.5 * NEG, s + tab, NEG)
            sink = sink_ref[h:h + 1, 0:1]
            m = jnp.maximum(jnp.max(s, axis=-1, keepdims=True), sink)
            p = jnp.exp(s - m)
            denom = jnp.sum(p, axis=-1, keepdims=True) + jnp.exp(sink - m)
            p = p / denom
            outs.append(jnp.dot(p.astype(BF16), vcat, preferred_element_type=F32))
    out_ref[...] = jnp.concatenate(outs, axis=1)


def _swa(pa, sinks, tables):
    b, lp, _ = pa.shape
    blk = A_BLOCK
    kcol = A_WIDTH // A_KV_WIDTH
    sink = jnp.broadcast_to(sinks.astype(F32)[:, None], (A_HEADS, LANES))
    kv_spec = lambda col, rowmap: pl.BlockSpec((None, blk, A_KV_WIDTH),
                                               lambda i, j: (i, rowmap(j), col))
    cur = lambda j: j
    prv = lambda j: jnp.maximum(j - 1, 0)
    first = lambda j: 0
    return pl.pallas_call(
        _swa_kernel,
        grid=(b, lp // blk),
        in_specs=[pl.BlockSpec((None, blk, A_WIDTH), lambda i, j: (i, j, 0)),
                  kv_spec(kcol, cur), kv_spec(kcol, prv), kv_spec(kcol, first),
                  kv_spec(kcol + 1, cur), kv_spec(kcol + 1, prv), kv_spec(kcol + 1, first),
                  pl.BlockSpec((None, A_HEADS, blk, 3 * blk), lambda i, j: (jnp.minimum(j, 2), 0, 0, 0)),
                  pl.BlockSpec((A_HEADS, LANES), lambda i, j: (0, 0))],
        out_specs=pl.BlockSpec((None, blk, A_WIDTH), lambda i, j: (i, j, 0)),
        out_shape=jax.ShapeDtypeStruct((b, lp, A_WIDTH), F32),
        compiler_params=_cparams(("arbitrary", "arbitrary")),
        name="swa",
    )(pa, pa, pa, pa, pa, pa, pa, tables, sink)


def _layer_norm_rows(z, ln_ref):
    mu = jnp.mean(z, axis=-1, keepdims=True)
    zc = z - mu
    var = jnp.mean(zc * zc, axis=-1, keepdims=True)
    return zc * lax.rsqrt(var + LN_EPS) * ln_ref[0:1, :] + ln_ref[1:2, :]


def _mix_kernel(ym_ref, yr_ref, ya_ref, h_ref, w_ref, ln_ref, out_ref, *, alpha):
    tm = h_ref.shape[0]
    mix = jnp.dot(ym_ref[...].astype(BF16), w_ref[0:M_WIDTH, :], preferred_element_type=F32)
    mix += jnp.dot(yr_ref[...].astype(BF16), w_ref[M_WIDTH:M_WIDTH + R_WIDTH, :],
                   preferred_element_type=F32)
    mix += jnp.dot(ya_ref[...].astype(BF16), w_ref[M_WIDTH + R_WIDTH:, :],
                   preferred_element_type=F32)
    y = _layer_norm_rows(alpha * h_ref[...] + mix, ln_ref)
    row = pl.program_id(1) * tm + _iota2((tm, 1), 0)
    out_ref[...] = jnp.where(row < FRONT_PAD, 0.0, y)


def _mix(ym, yr, ya, h, w, ln, alpha):
    b, lp, d = h.shape
    tm = _row_tile(lp)
    spec = lambda width: pl.BlockSpec((None, tm, width), lambda i, j: (i, j, 0))
    return pl.pallas_call(
        functools.partial(_mix_kernel, alpha=alpha),
        grid=(b, lp // tm),
        in_specs=[spec(M_WIDTH), spec(R_WIDTH), spec(A_WIDTH), spec(d),
                  pl.BlockSpec(w.shape, lambda i, j: (0, 0), pipeline_mode=pl.Buffered(1)),
                  pl.BlockSpec((2, d), lambda i, j: (0, 0))],
        out_specs=spec(d),
        out_shape=jax.ShapeDtypeStruct((b, lp, d), F32),
        compiler_params=_cparams(("arbitrary", "arbitrary")),
        name="out_proj_ln",
    )(ym, yr, ya, h, w, ln)


FF_CHUNK = 256


def _ffn_kernel(h_ref, wi_ref, wo_ref, ln_ref, out_ref, *, alpha, d_ff):
    tm = h_ref.shape[0]
    x = h_ref[...]
    xb = x.astype(BF16)
    acc = alpha * x
    for j in range(d_ff // FF_CHUNK):
        cs = slice(j * FF_CHUNK, (j + 1) * FF_CHUNK)
        us = slice(d_ff + j * FF_CHUNK, d_ff + (j + 1) * FF_CHUNK)
        gate = jnp.dot(xb, wi_ref[:, cs], preferred_element_type=F32)
        up = jnp.dot(xb, wi_ref[:, us], preferred_element_type=F32)
        act = (gate * jax.nn.sigmoid(gate) * up).astype(BF16)
        acc += jnp.dot(act, wo_ref[cs, :], preferred_element_type=F32)
    y = _layer_norm_rows(acc, ln_ref)
    row = pl.program_id(1) * tm + _iota2((tm, 1), 0)
    out_ref[...] = jnp.where(row < FRONT_PAD, 0.0, y)


def _ffn(h, wi, wo, ln, alpha):
    b, lp, d = h.shape
    d_ff = wo.shape[0]
    assert d_ff % FF_CHUNK == 0
    tm = _row_tile(lp)
    spec = pl.BlockSpec((None, tm, d), lambda i, j: (i, j, 0))
    const = lambda a: pl.BlockSpec(a.shape, lambda i, j: (0, 0), pipeline_mode=pl.Buffered(1))
    return pl.pallas_call(
        functools.partial(_ffn_kernel, alpha=alpha, d_ff=d_ff),
        grid=(b, lp // tm),
        in_specs=[spec, const(wi), const(wo), pl.BlockSpec((2, d), lambda i, j: (0, 0))],
        out_specs=spec,
        out_shape=jax.ShapeDtypeStruct((b, lp, d), F32),
        compiler_params=_cparams(("arbitrary", "arbitrary")),
        name="ffn_ln",
    )(h, wi, wo, ln)


def kernel(x, meta_tokens, rel_bias, w_in, m_gate_bias, m_norm_w, r_mu_rkv, r_mu_w, r_mu_a, r_mu_g, r_w0, r_w2, r_a0, r_a2, r_g2, r_k_k, r_k_a, r_r_k, r_ln_x, a_sinks, w_out, ln_mix, w_ff_in, w_ff_out, ln_ffn):
    b, seq, d = x.shape
    depth = w_in.shape[0]
    alpha = (2 * depth) ** 0.25
    assert (TOKEN_START + seq) % A_BLOCK == 0

    meta = jnp.broadcast_to(meta_tokens[None].astype(x.dtype), (b, N_META, d))
    h = jnp.concatenate([jnp.zeros((b, FRONT_PAD, d), x.dtype), meta, x], axis=1)
    tables = _attn_tables(rel_bias)
    a_end = PA_W
    m_end = a_end + M_RAW

    for l in range(depth):
        wl = w_in[l]
        w_pad = jnp.concatenate(
            [wl[:, :m_end], jnp.zeros((d, PM_W - M_RAW), wl.dtype), wl[:, m_end:]], axis=1).astype(BF16)
        pa, pm, pr = _proj(h, w_pad)
        y_m = _mlstm(pm, m_gate_bias[l], m_norm_w[l])
        mu = jnp.concatenate([r_mu_rkv[l].reshape(-1), r_mu_w[l], r_mu_a[l], r_mu_g[l]])
        y_r = _rwkv(pr, mu, r_w0[l], r_w2[l], r_a0[l], r_a2[l], r_g2[l], r_k_k[l], r_k_a[l],
                    r_r_k[l], r_ln_x[l])
        y_a = _swa(pa, a_sinks[l], tables)
        h = _mix(y_m, y_r, y_a, h, w_out[l].astype(BF16), ln_mix[l], alpha)
        h = _ffn(h, w_ff_in[l].astype(BF16), w_ff_out[l].astype(BF16), ln_ffn[l], alpha)
    return h[:, TOKEN_START:]
```

```python
import functools
import math

import jax
import jax.numpy as jnp
from jax import lax
from jax.experimental import pallas as pl
from jax.experimental.pallas import tpu as pltpu

F32 = jnp.float32
BF16 = jnp.bfloat16

HEAD_DIM = 64
N_META = 16
M_HEADS = 4
M_WIDTH = M_HEADS * HEAD_DIM
M_CHUNK = 64
M_NORM_EPS = 1e-6
R_HEADS = 4
R_WIDTH = R_HEADS * HEAD_DIM
R_CHUNK = 64
R_DECAY_RANK = 32
R_A_RANK = 32
R_GATE_RANK = 64
R_GN_EPS = 64e-5
A_HEADS = 8
A_KV_HEADS = 2
A_GROUP = A_HEADS // A_KV_HEADS
A_WIDTH = A_HEADS * HEAD_DIM
A_KV_WIDTH = A_KV_HEADS * HEAD_DIM
WINDOW = 128
A_BLOCK = 128
N_BUCKETS = 32
MAX_DISTANCE = 128
LN_EPS = 1e-5
NEG = -1e30

LANES = 128
FRONT_PAD = A_BLOCK - N_META
TOKEN_START = FRONT_PAD + N_META

PA_W = A_WIDTH + 2 * A_KV_WIDTH
PM_W = 4 * M_WIDTH + LANES
PR_W = 3 * R_WIDTH + R_DECAY_RANK + R_A_RANK + R_GATE_RANK
M_RAW = 4 * M_WIDTH + 2 * M_HEADS

VMEM_LIMIT = 56 * 1024 * 1024
MIX_ROWS = 256


def _row_tile(n):
    for t in (640, 512, 256, 128, 64):
        if n % t == 0:
            return t
    raise ValueError(f"row count {n} has no supported tile")


def _mix_rows(lp):
    return MIX_ROWS if lp % MIX_ROWS == 0 else A_BLOCK


def _cparams(sem):
    return pltpu.CompilerParams(dimension_semantics=sem, vmem_limit_bytes=VMEM_LIMIT)


def _bf(x):
    return x.astype(BF16)


def _mm(a, b):
    return lax.dot_general(_bf(a), _bf(b), (((1,), (0,)), ((), ())), preferred_element_type=F32)


def _mm_nt(a, b):
    return lax.dot_general(_bf(a), _bf(b), (((1,), (1,)), ((), ())), preferred_element_type=F32)


def _mm_tn(a, b):
    return lax.dot_general(_bf(a), _bf(b), (((0,), (0,)), ((), ())), preferred_element_type=F32)


def _iota2(shape, dim):
    return lax.broadcasted_iota(jnp.int32, shape, dim)


def _split3(x):
    hi = _bf(x).astype(F32)
    r1 = x - hi
    mid = _bf(r1).astype(F32)
    return hi, mid, r1 - mid


def _chunk_mask(n, chunk, lower):
    r_i = _iota2((n, n), 0)
    c_i = _iota2((n, n), 1)
    sh = chunk.bit_length() - 1
    same = jnp.right_shift(r_i, sh) == jnp.right_shift(c_i, sh)
    return same & (r_i >= c_i) if lower else same


def _cumsum_rows(tri, x):
    w = x.shape[1]
    res = _mm(tri, jnp.concatenate(_split3(x), axis=1))
    return res[:, :w] + res[:, w:2 * w] + res[:, 2 * w:]


def _cumsum_lanes(tri, x):
    r = x.shape[0]
    res = _mm_nt(jnp.concatenate(_split3(x), axis=0), tri)
    return res[:r] + res[r:2 * r] + res[2 * r:]


def _softplus(x):
    return jnp.maximum(x, 0.0) + jnp.log1p(jnp.exp(-jnp.abs(x)))


def _proj_kernel(x_ref, w_ref, oa_ref, om_ref, or_ref):
    acc = jnp.dot(x_ref[...].astype(BF16), w_ref[...], preferred_element_type=F32)
    oa_ref[...] = acc[:, :PA_W]
    om_ref[...] = acc[:, PA_W:PA_W + PM_W]
    or_ref[...] = acc[:, PA_W + PM_W:]


def _proj(h, w):
    b, lp, d = h.shape
    tm = _row_tile(lp)
    n = w.shape[1]
    spec = lambda width: pl.BlockSpec((None, tm, width), lambda i, j: (i, j, 0))
    return pl.pallas_call(
        _proj_kernel,
        grid=(b, lp // tm),
        in_specs=[spec(d),
                  pl.BlockSpec((d, n), lambda i, j: (0, 0), pipeline_mode=pl.Buffered(1))],
        out_specs=[spec(PA_W), spec(PM_W), spec(PR_W)],
        out_shape=[jax.ShapeDtypeStruct((b, lp, PA_W), F32),
                   jax.ShapeDtypeStruct((b, lp, PM_W), F32),
                   jax.ShapeDtypeStruct((b, lp, PR_W), F32)],
        compiler_params=_cparams(("arbitrary", "arbitrary")),
        name="in_proj",
    )(h, w)


def _mlstm_kernel(pm_ref, bias_ref, nw_ref, out_ref, cst_ref, m_ref):
    blk = pl.program_id(1)
    tb = pm_ref.shape[0]
    ch = M_CHUNK

    @pl.when(blk == 0)
    def _():
        cst_ref[...] = jnp.zeros_like(cst_ref)
        m_ref[...] = jnp.full_like(m_ref, NEG)

    row = blk * tb + _iota2((tb, 1), 0)
    is_pad = row < FRONT_PAD
    gb = pm_ref[:, 4 * M_WIDTH:] + bias_ref[...]
    lane = _iota2((tb, LANES), 1)
    li = jnp.where(is_pad, NEG, gb)
    lf = jnp.where(is_pad, 0.0, -_softplus(-gb))
    gcol = jnp.where(lane < M_HEADS, li, jnp.where(lane < 2 * M_HEADS, lf, 0.0))
    grow = gcol.T[0:8, :]
    tri = _chunk_mask(tb, ch, True).astype(F32)
    bcol = _cumsum_rows(tri, gcol)
    brow = _cumsum_lanes(tri, grow)
    causal = _iota2((ch, ch), 0) >= _iota2((ch, ch), 1)
    ones_col = (_iota2((ch, HEAD_DIM), 1) == 0).astype(F32)

    cexts = [cst_ref[h] for h in range(M_HEADS)]
    ms = [m_ref[h][0:1, 0:1] for h in range(M_HEADS)]
    for c in range(tb // ch):
        cs = slice(c * ch, (c + 1) * ch)
        outs = []
        for h in range(M_HEADS):
            sl = slice(h * HEAD_DIM, (h + 1) * HEAD_DIM)
            q = pm_ref[cs, sl]
            k = pm_ref[cs, M_WIDTH + h * HEAD_DIM:M_WIDTH + (h + 1) * HEAD_DIM] * (HEAD_DIM ** -0.5)
            v = pm_ref[cs, 2 * M_WIDTH + h * HEAD_DIM:2 * M_WIDTH + (h + 1) * HEAD_DIM]
            o_pre = pm_ref[cs, 3 * M_WIDTH + h * HEAD_DIM:3 * M_WIDTH + (h + 1) * HEAD_DIM]
            b_c = bcol[cs, M_HEADS + h:M_HEADS + h + 1]
            b_r = brow[M_HEADS + h:M_HEADS + h + 1, cs]
            li_r = grow[h:h + 1, cs]
            li_c = gcol[cs, h:h + 1]
            m_prev = ms[h]
            cext = cexts[h]

            d_mat = jnp.where(causal, b_c - b_r + li_r, NEG)
            inter = b_c + m_prev
            m_t = jnp.maximum(jnp.max(d_mat, axis=-1, keepdims=True), inter)
            dw = jnp.where(causal, jnp.exp(d_mat - m_t), 0.0)
            s = _mm_nt(q, k) * dw
            s_inter = jnp.exp(inter - m_t)
            vext = jnp.concatenate([v, ones_col], axis=1)
            tot = _mm(s, vext) + s_inter * _mm_nt(q, cext)
            num = tot[:, :HEAD_DIM]
            den = tot[:, HEAD_DIM:HEAD_DIM + 1]
            hh = num / jnp.maximum(jnp.abs(den), jnp.exp(-m_t))

            g = b_c[ch - 1:ch, :]
            a_c = g - b_c + li_c
            m_new = jnp.maximum(g + m_prev, jnp.max(a_c, axis=0, keepdims=True))
            wa = jnp.exp(a_c - m_new)
            decay = jnp.exp(g + m_prev - m_new)
            cexts[h] = decay * cext + _mm_tn(vext, wa * k)
            ms[h] = m_new

            mu = jnp.mean(hh, axis=-1, keepdims=True)
            var = jnp.mean(jnp.square(hh - mu), axis=-1, keepdims=True)
            hn = (hh - mu) * lax.rsqrt(var + M_NORM_EPS)
            outs.append(hn * nw_ref[:, sl] * jax.nn.sigmoid(o_pre))
        out_ref[cs, :] = jnp.concatenate(outs, axis=1)
    for h in range(M_HEADS):
        cst_ref[h] = cexts[h]
        m_ref[h] = jnp.broadcast_to(ms[h], m_ref.shape[1:])


def _mlstm(pm, gate_bias, norm_w):
    b, lp, _ = pm.shape
    tb = _mix_rows(lp)
    bias = jnp.zeros((1, LANES), F32).at[0, :2 * M_HEADS].set(gate_bias.reshape(-1))
    return pl.pallas_call(
        _mlstm_kernel,
        grid=(b, lp // tb),
        in_specs=[pl.BlockSpec((None, tb, PM_W), lambda i, j: (i, j, 0)),
                  pl.BlockSpec((1, LANES), lambda i, j: (0, 0)),
                  pl.BlockSpec((1, M_WIDTH), lambda i, j: (0, 0))],
        out_specs=pl.BlockSpec((None, tb, M_WIDTH), lambda i, j: (i, j, 0)),
        out_shape=jax.ShapeDtypeStruct((b, lp, M_WIDTH), F32),
        scratch_shapes=[pltpu.VMEM((M_HEADS, 2 * HEAD_DIM, HEAD_DIM), F32),
                        pltpu.VMEM((M_HEADS, 8, LANES), F32)],
        compiler_params=_cparams(("arbitrary", "arbitrary")),
        name="mlstm",
    )(pm, bias, norm_w.reshape(1, M_WIDTH))


def _rwkv_kernel(pr_ref, prev_ref, mu_ref, w0_ref, w2_ref, a0_ref, a2_ref, g2_ref,
                 kk_ref, ka_ref, rk_ref, lnx_ref, out_ref, st_ref):
    blk = pl.program_id(1)
    tb = pr_ref.shape[0]
    ch = R_CHUNK
    rw = R_WIDTH
    hd = HEAD_DIM

    @pl.when(blk == 0)
    def _():
        st_ref[...] = jnp.zeros_like(st_ref)

    x = pr_ref[...]
    last = jnp.where(blk == 0, 0.0, prev_ref[7:8, :])
    prev = jnp.where(_iota2((tb, 1), 0) == 0, last, pltpu.roll(x, 1, axis=0))
    t = x + (prev - x) * mu_ref[...]
    r = t[:, 0:rw]
    k = t[:, rw:2 * rw]
    v = t[:, 2 * rw:3 * rw]
    o = 3 * rw
    w_lat = t[:, o:o + R_DECAY_RANK]
    a_lat = t[:, o + R_DECAY_RANK:o + R_DECAY_RANK + R_A_RANK]
    g_lat = t[:, o + R_DECAY_RANK + R_A_RANK:]

    w_log = -_softplus(-(w0_ref[...] + _mm(jnp.tanh(w_lat), w2_ref[...]))) - 0.5
    lw = -jnp.exp(w_log)
    a = jax.nn.sigmoid(a0_ref[...] + _mm(a_lat, a2_ref[...]))
    g = _mm(jax.nn.sigmoid(g_lat), g2_ref[...])

    head_ones = _chunk_mask(rw, hd, False).astype(BF16)
    kk = k * kk_ref[...]
    kk = kk / jnp.maximum(jnp.sqrt(_mm(kk * kk, head_ones)), 1e-12)
    k2 = k * (1.0 + (a - 1.0) * ka_ref[...])
    avec = -kk
    bvec = kk * a
    cum = _cumsum_rows(_chunk_mask(tb, ch, True).astype(F32), lw)

    r_i = _iota2((ch, ch), 0)
    c_i = _iota2((ch, ch), 1)
    incl = r_i >= c_i
    strict = r_i > c_i
    eye = r_i == c_i
    zeros_h = jnp.zeros((ch, hd), F32)
    n_levels = ch.bit_length() - 1

    sts = [st_ref[h] for h in range(R_HEADS)]
    y_chunks = []
    for c in range(tb // ch):
        cs = slice(c * ch, (c + 1) * ch)
        cum_c = cum[cs]
        cl = cum_c[ch - 1:ch, :]
        inv = jnp.exp(-cum_c)
        rel = jnp.exp(cl - cum_c)
        ah_all = avec[cs] * jnp.exp(cum_c - lw[cs])
        rh_all = r[cs] * jnp.exp(cum_c)
        kt_all = k2[cs] * inv
        bt_all = bvec[cs] * inv
        kp_all = k2[cs] * rel
        bp_all = bvec[cs] * rel
        wc_all = jnp.exp(cl)
        v_c = v[cs]
        ys = []
        for h in range(R_HEADS):
            sl = slice(h * hd, (h + 1) * hd)
            ah, rh, vh = ah_all[:, sl], rh_all[:, sl], v_c[:, sl]
            gram = _mm_nt(jnp.concatenate([ah, rh], axis=0),
                          jnp.concatenate([kt_all[:, sl], bt_all[:, sl]], axis=0))
            m_ak = jnp.where(strict, gram[:ch, :ch], 0.0)
            m_ab = jnp.where(strict, gram[:ch, ch:], 0.0)
            n_rk = jnp.where(incl, gram[ch:, :ch], 0.0)
            n_rb = jnp.where(incl, gram[ch:, ch:], 0.0)
            z = jnp.concatenate([ah, _mm(m_ak, vh)], axis=1)
            pw = m_ab
            for lvl in range(n_levels):
                if lvl + 1 < n_levels:
                    res = _mm(pw, jnp.concatenate([z, pw], axis=1))
                    z = z + res[:, :2 * hd]
                    pw = res[:, 2 * hd:]
                else:
                    z = z + _mm(pw, z)
            rhs = jnp.concatenate([jnp.concatenate([zeros_h, vh], axis=1), z], axis=0)
            ry = _mm(jnp.concatenate([n_rk, n_rb], axis=1), rhs)
            tg = _mm_tn(jnp.concatenate([kp_all[:, sl], bp_all[:, sl]], axis=0), rhs)
            r_eff = rh + ry[:, :hd]
            t_mat = jnp.where(eye, wc_all[:, sl], 0.0) + tg[:, :hd]
            nxt = _mm(jnp.concatenate([r_eff, t_mat], axis=0), sts[h])
            ys.append(nxt[:ch] + ry[:, hd:])
            sts[h] = nxt[ch:] + tg[:, hd:]
        y_chunks.append(jnp.concatenate(ys, axis=1))
    for h in range(R_HEADS):
        st_ref[h] = sts[h]
    y = jnp.concatenate(y_chunks, axis=0)

    inv_d = 1.0 / hd
    mu = _mm(y, head_ones) * inv_d
    yc = y - mu
    var = _mm(yc * yc, head_ones) * inv_d
    yn = yc * lax.rsqrt(var + R_GN_EPS) * lnx_ref[0:1, :] + lnx_ref[1:2, :]
    bonus = _mm(r * k2 * rk_ref[...], head_ones) * v
    out_ref[...] = (yn + bonus) * g


def _rwkv(pr, mu, w0, w2, a0, a2, g2, k_k, k_a, r_k, ln_x):
    b, lp, _ = pr.shape
    tb = _mix_rows(lp)
    per8 = tb // 8
    row = lambda a: a.reshape(1, -1)
    full = lambda a: pl.BlockSpec(a.shape, lambda i, j: (0,) * a.ndim)
    params = [row(mu), row(w0), w2, row(a0), a2, g2, row(k_k), row(k_a), row(r_k), ln_x]
    return pl.pallas_call(
        _rwkv_kernel,
        grid=(b, lp // tb),
        in_specs=[pl.BlockSpec((None, tb, PR_W), lambda i, j: (i, j, 0)),
                  pl.BlockSpec((None, 8, PR_W), lambda i, j: (i, jnp.maximum(j * per8 - 1, 0), 0))]
                 + [full(a) for a in params],
        out_specs=pl.BlockSpec((None, tb, R_WIDTH), lambda i, j: (i, j, 0)),
        out_shape=jax.ShapeDtypeStruct((b, lp, R_WIDTH), F32),
        scratch_shapes=[pltpu.VMEM((R_HEADS, HEAD_DIM, HEAD_DIM), F32)],
        compiler_params=_cparams(("arbitrary", "arbitrary")),
        name="rwkv7",
    )(pr, pr, *params)


def _t5_bucket(dist):
    max_exact = N_BUCKETS // 2
    d = jnp.maximum(dist, 1).astype(F32)
    large = max_exact + (jnp.log(d / max_exact) / math.log(MAX_DISTANCE / max_exact)
                         * (N_BUCKETS - max_exact)).astype(jnp.int32)
    large = jnp.minimum(large, N_BUCKETS - 1)
    return jnp.where(dist < max_exact, dist, large)


def _attn_tables(rel_bias):
    blk = A_BLOCK
    i = jnp.arange(blk)[:, None]
    j = jnp.arange(blk)[None, :]
    rb = rel_bias.astype(F32)

    def bias(dist):
        onehot = jax.nn.one_hot(_t5_bucket(jnp.maximum(dist, 0)), N_BUCKETS, dtype=F32)
        return jnp.einsum("hb,ijb->hij", rb, onehot, precision=lax.Precision.HIGHEST)

    neg = jnp.full((A_HEADS, blk, blk), NEG, F32)
    is_meta = j >= FRONT_PAD
    cur_d = i - j
    prev_d = blk + i - j
    cur = jnp.where(cur_d >= 0, bias(cur_d), NEG)
    prev = jnp.where(prev_d < WINDOW, bias(prev_d), NEG)
    meta0 = jnp.where(is_meta & (cur_d >= 0), bias(cur_d), NEG)
    meta1 = jnp.where(is_meta, bias(prev_d), NEG)
    meta2 = jnp.where(is_meta, bias(2 * blk + i - j), NEG)
    case0 = jnp.concatenate([meta0, neg, neg], axis=-1)
    case1 = jnp.concatenate([meta1, neg, cur], axis=-1)
    case2 = jnp.concatenate([meta2, prev, cur], axis=-1)
    return jnp.stack([case0, case1, case2], axis=0)


def _swa_kernel(q_ref, kc_ref, kp_ref, km_ref, vc_ref, vp_ref, vm_ref, tab_ref, sink_ref, out_ref):
    scale = HEAD_DIM ** -0.5
    blk = q_ref.shape[0]
    outs = []
    for kv in range(A_KV_HEADS):
        ks = slice(kv * HEAD_DIM, (kv + 1) * HEAD_DIM)
        heads = range(kv * A_GROUP, (kv + 1) * A_GROUP)
        kcat = jnp.concatenate([km_ref[:, ks], kp_ref[:, ks], kc_ref[:, ks]], axis=0)
        vcat = jnp.concatenate([vm_ref[:, ks], vp_ref[:, ks], vc_ref[:, ks]], axis=0)
        q = jnp.concatenate([q_ref[:, h * HEAD_DIM:(h + 1) * HEAD_DIM] for h in heads], axis=0)
        tab = jnp.concatenate([tab_ref[h] for h in heads], axis=0)
        sink = jnp.concatenate([jnp.broadcast_to(sink_ref[h:h + 1, 0:1], (blk, 1)) for h in heads],
                               axis=0)
        s = _mm_nt(q, kcat) * scale
        s = jnp.where(tab > 0.5 * NEG, s + tab, NEG)
        m = jnp.maximum(jnp.max(s, axis=-1, keepdims=True), sink)
        p = jnp.exp(s - m)
        denom = jnp.sum(p, axis=-1, keepdims=True) + jnp.exp(sink - m)
        o = _mm(p, vcat) / denom
        outs.extend(o[g * blk:(g + 1) * blk] for g in range(A_GROUP))
    out_ref[...] = jnp.concatenate(outs, axis=1)


def _swa(pa, sinks, tables):
    b, lp, _ = pa.shape
    blk = A_BLOCK
    kcol = A_WIDTH // A_KV_WIDTH
    sink = jnp.broadcast_to(sinks.astype(F32)[:, None], (A_HEADS, LANES))
    kv_spec = lambda col, rowmap: pl.BlockSpec((None, blk, A_KV_WIDTH),
                                               lambda i, j: (i, rowmap(j), col))
    cur = lambda j: j
    prv = lambda j: jnp.maximum(j - 1, 0)
    first = lambda j: 0
    return pl.pallas_call(
        _swa_kernel,
        grid=(b, lp // blk),
        in_specs=[pl.BlockSpec((None, blk, A_WIDTH), lambda i, j: (i, j, 0)),
                  kv_spec(kcol, cur), kv_spec(kcol, prv), kv_spec(kcol, first),
                  kv_spec(kcol + 1, cur), kv_spec(kcol + 1, prv), kv_spec(kcol + 1, first),
                  pl.BlockSpec((None, A_HEADS, blk, 3 * blk), lambda i, j: (jnp.minimum(j, 2), 0, 0, 0)),
                  pl.BlockSpec((A_HEADS, LANES), lambda i, j: (0, 0))],
        out_specs=pl.BlockSpec((None, blk, A_WIDTH), lambda i, j: (i, j, 0)),
        out_shape=jax.ShapeDtypeStruct((b, lp, A_WIDTH), F32),
        compiler_params=_cparams(("arbitrary", "arbitrary")),
        name="swa",
    )(pa, pa, pa, pa, pa, pa, pa, tables, sink)


def _layer_norm_rows(z, ln_ref):
    mu = jnp.mean(z, axis=-1, keepdims=True)
    zc = z - mu
    var = jnp.mean(zc * zc, axis=-1, keepdims=True)
    return zc * lax.rsqrt(var + LN_EPS) * ln_ref[0:1, :] + ln_ref[1:2, :]


def _mix_kernel(ym_ref, yr_ref, ya_ref, h_ref, w_ref, ln_ref, out_ref, *, alpha):
    tm = h_ref.shape[0]
    mix = jnp.dot(ym_ref[...].astype(BF16), w_ref[0:M_WIDTH, :], preferred_element_type=F32)
    mix += jnp.dot(yr_ref[...].astype(BF16), w_ref[M_WIDTH:M_WIDTH + R_WIDTH, :],
                   preferred_element_type=F32)
    mix += jnp.dot(ya_ref[...].astype(BF16), w_ref[M_WIDTH + R_WIDTH:, :],
                   preferred_element_type=F32)
    y = _layer_norm_rows(alpha * h_ref[...] + mix, ln_ref)
    row = pl.program_id(1) * tm + _iota2((tm, 1), 0)
    out_ref[...] = jnp.where(row < FRONT_PAD, 0.0, y)


def _mix(ym, yr, ya, h, w, ln, alpha):
    b, lp, d = h.shape
    tm = _row_tile(lp)
    spec = lambda width: pl.BlockSpec((None, tm, width), lambda i, j: (i, j, 0))
    return pl.pallas_call(
        functools.partial(_mix_kernel, alpha=alpha),
        grid=(b, lp // tm),
        in_specs=[spec(M_WIDTH), spec(R_WIDTH), spec(A_WIDTH), spec(d),
                  pl.BlockSpec(w.shape, lambda i, j: (0, 0), pipeline_mode=pl.Buffered(1)),
                  pl.BlockSpec((2, d), lambda i, j: (0, 0))],
        out_specs=spec(d),
        out_shape=jax.ShapeDtypeStruct((b, lp, d), F32),
        compiler_params=_cparams(("arbitrary", "arbitrary")),
        name="out_proj_ln",
    )(ym, yr, ya, h, w, ln)


FF_CHUNK = 256


def _ffn_kernel(h_ref, wi_ref, wo_ref, ln_ref, out_ref, *, alpha, d_ff):
    tm = h_ref.shape[0]
    x = h_ref[...]
    xb = x.astype(BF16)
    acc = alpha * x
    for j in range(d_ff // FF_CHUNK):
        cs = slice(j * FF_CHUNK, (j + 1) * FF_CHUNK)
        us = slice(d_ff + j * FF_CHUNK, d_ff + (j + 1) * FF_CHUNK)
        gate = jnp.dot(xb, wi_ref[:, cs], preferred_element_type=F32)
        up = jnp.dot(xb, wi_ref[:, us], preferred_element_type=F32)
        act = (gate * jax.nn.sigmoid(gate) * up).astype(BF16)
        acc += jnp.dot(act, wo_ref[cs, :], preferred_element_type=F32)
    y = _layer_norm_rows(acc, ln_ref)
    row = pl.program_id(1) * tm + _iota2((tm, 1), 0)
    out_ref[...] = jnp.where(row < FRONT_PAD, 0.0, y)


def _ffn(h, wi, wo, ln, alpha):
    b, lp, d = h.shape
    d_ff = wo.shape[0]
    assert d_ff % FF_CHUNK == 0
    tm = _row_tile(lp)
    spec = pl.BlockSpec((None, tm, d), lambda i, j: (i, j, 0))
    const = lambda a: pl.BlockSpec(a.shape, lambda i, j: (0, 0), pipeline_mode=pl.Buffered(1))
    return pl.pallas_call(
        functools.partial(_ffn_kernel, alpha=alpha, d_ff=d_ff),
        grid=(b, lp // tm),
        in_specs=[spec, const(wi), const(wo), pl.BlockSpec((2, d), lambda i, j: (0, 0))],
        out_specs=spec,
        out_shape=jax.ShapeDtypeStruct((b, lp, d), F32),
        compiler_params=_cparams(("arbitrary", "arbitrary")),
        name="ffn_ln",
    )(h, wi, wo, ln)


def kernel(x, meta_tokens, rel_bias, w_in, m_gate_bias, m_norm_w, r_mu_rkv, r_mu_w, r_mu_a, r_mu_g, r_w0, r_w2, r_a0, r_a2, r_g2, r_k_k, r_k_a, r_r_k, r_ln_x, a_sinks, w_out, ln_mix, w_ff_in, w_ff_out, ln_ffn):
    b, seq, d = x.shape
    depth = w_in.shape[0]
    alpha = (2 * depth) ** 0.25
    assert (TOKEN_START + seq) % A_BLOCK == 0

    meta = jnp.broadcast_to(meta_tokens[None].astype(x.dtype), (b, N_META, d))
    h = jnp.concatenate([jnp.zeros((b, FRONT_PAD, d), x.dtype), meta, x], axis=1)
    tables = _attn_tables(rel_bias)
    a_end = PA_W
    m_end = a_end + M_RAW

    for l in range(depth):
        wl = w_in[l]
        w_pad = jnp.concatenate(
            [wl[:, :m_end], jnp.zeros((d, PM_W - M_RAW), wl.dtype), wl[:, m_end:]], axis=1).astype(BF16)
        pa, pm, pr = _proj(h, w_pad)
        y_m = _mlstm(pm, m_gate_bias[l], m_norm_w[l])
        mu = jnp.concatenate([r_mu_rkv[l].reshape(-1), r_mu_w[l], r_mu_a[l], r_mu_g[l]])
        y_r = _rwkv(pr, mu, r_w0[l], r_w2[l], r_a0[l], r_a2[l], r_g2[l], r_k_k[l], r_k_a[l],
                    r_r_k[l], r_ln_x[l])
        y_a = _swa(pa, a_sinks[l], tables)
        h = _mix(y_m, y_r, y_a, h, w_out[l].astype(BF16), ln_mix[l], alpha)
        h = _ffn(h, w_ff_in[l].astype(BF16), w_ff_out[l].astype(BF16), ln_ffn[l], alpha)
    return h[:, TOKEN_START:]
```

```python
import functools
import math

import jax
import jax.numpy as jnp
from jax import lax
from jax.experimental import pallas as pl
from jax.experimental.pallas import tpu as pltpu

F32 = jnp.float32
BF16 = jnp.bfloat16

HEAD_DIM = 64
N_META = 16
M_HEADS = 4
M_WIDTH = M_HEADS * HEAD_DIM
M_CHUNK = 64
M_NORM_EPS = 1e-6
R_HEADS = 4
R_WIDTH = R_HEADS * HEAD_DIM
R_CHUNK = 64
R_DECAY_RANK = 32
R_A_RANK = 32
R_GATE_RANK = 64
R_GN_EPS = 64e-5
A_HEADS = 8
A_KV_HEADS = 2
A_GROUP = A_HEADS // A_KV_HEADS
A_WIDTH = A_HEADS * HEAD_DIM
A_KV_WIDTH = A_KV_HEADS * HEAD_DIM
WINDOW = 128
A_BLOCK = 128
N_BUCKETS = 32
MAX_DISTANCE = 128
LN_EPS = 1e-5
NEG = -1e30

LANES = 128
FRONT_PAD = A_BLOCK - N_META
TOKEN_START = FRONT_PAD + N_META

PA_W = A_WIDTH + 2 * A_KV_WIDTH
PM_W = 4 * M_WIDTH + LANES
PR_W = 3 * R_WIDTH + R_DECAY_RANK + R_A_RANK + R_GATE_RANK
M_RAW = 4 * M_WIDTH + 2 * M_HEADS

VMEM_LIMIT = 56 * 1024 * 1024
MIX_ROWS = 256


def _row_tile(n):
    for t in (640, 512, 256, 128, 64):
        if n % t == 0:
            return t
    raise ValueError(f"row count {n} has no supported tile")


def _mix_rows(lp):
    return MIX_ROWS if lp % MIX_ROWS == 0 else A_BLOCK


def _cparams(sem):
    return pltpu.CompilerParams(dimension_semantics=sem, vmem_limit_bytes=VMEM_LIMIT)


def _bf(x):
    return x.astype(BF16)


def _mm(a, b):
    return lax.dot_general(_bf(a), _bf(b), (((1,), (0,)), ((), ())), preferred_element_type=F32)


def _mm_nt(a, b):
    return lax.dot_general(_bf(a), _bf(b), (((1,), (1,)), ((), ())), preferred_element_type=F32)


def _mm_tn(a, b):
    return lax.dot_general(_bf(a), _bf(b), (((0,), (0,)), ((), ())), preferred_element_type=F32)


def _iota2(shape, dim):
    return lax.broadcasted_iota(jnp.int32, shape, dim)


def _split3(x):
    hi = _bf(x).astype(F32)
    r1 = x - hi
    mid = _bf(r1).astype(F32)
    return hi, mid, r1 - mid


def _chunk_mask(n, chunk, lower):
    r_i = _iota2((n, n), 0)
    c_i = _iota2((n, n), 1)
    sh = chunk.bit_length() - 1
    same = jnp.right_shift(r_i, sh) == jnp.right_shift(c_i, sh)
    return same & (r_i >= c_i) if lower else same


def _cumsum_rows(tri, x):
    w = x.shape[1]
    res = _mm(tri, jnp.concatenate(_split3(x), axis=1))
    return res[:, :w] + res[:, w:2 * w] + res[:, 2 * w:]


def _cumsum_lanes(tri, x):
    r = x.shape[0]
    res = _mm_nt(jnp.concatenate(_split3(x), axis=0), tri)
    return res[:r] + res[r:2 * r] + res[2 * r:]


def _softplus(x):
    return jnp.maximum(x, 0.0) + jnp.log1p(jnp.exp(-jnp.abs(x)))


def _proj_kernel(x_ref, w_ref, oa_ref, om_ref, or_ref):
    acc = jnp.dot(x_ref[...].astype(BF16), w_ref[...], preferred_element_type=F32)
    oa_ref[...] = acc[:, :PA_W]
    om_ref[...] = acc[:, PA_W:PA_W + PM_W]
    or_ref[...] = acc[:, PA_W + PM_W:]


def _proj(h, w):
    b, lp, d = h.shape
    tm = _row_tile(lp)
    n = w.shape[1]
    spec = lambda width: pl.BlockSpec((None, tm, width), lambda i, j: (i, j, 0))
    return pl.pallas_call(
        _proj_kernel,
        grid=(b, lp // tm),
        in_specs=[spec(d),
                  pl.BlockSpec((d, n), lambda i, j: (0, 0), pipeline_mode=pl.Buffered(1))],
        out_specs=[spec(PA_W), spec(PM_W), spec(PR_W)],
        out_shape=[jax.ShapeDtypeStruct((b, lp, PA_W), F32),
                   jax.ShapeDtypeStruct((b, lp, PM_W), F32),
                   jax.ShapeDtypeStruct((b, lp, PR_W), F32)],
        compiler_params=_cparams(("arbitrary", "arbitrary")),
        name="in_proj",
    )(h, w)


def _mlstm_kernel(pm_ref, bias_ref, nw_ref, out_ref, cst_ref, m_ref):
    blk = pl.program_id(1)
    tb = pm_ref.shape[0]
    ch = M_CHUNK

    @pl.when(blk == 0)
    def _():
        cst_ref[...] = jnp.zeros_like(cst_ref)
        m_ref[...] = jnp.full_like(m_ref, NEG)

    row = blk * tb + _iota2((tb, 1), 0)
    is_pad = row < FRONT_PAD
    gb = pm_ref[:, 4 * M_WIDTH:] + bias_ref[...]
    lane = _iota2((tb, LANES), 1)
    li = jnp.where(is_pad, NEG, gb)
    lf = jnp.where(is_pad, 0.0, -_softplus(-gb))
    gcol = jnp.where(lane < M_HEADS, li, jnp.where(lane < 2 * M_HEADS, lf, 0.0))
    grow = gcol.T[0:8, :]
    tri = _chunk_mask(tb, ch, True).astype(F32)
    bcol = _cumsum_rows(tri, gcol)
    brow = _cumsum_lanes(tri, grow)
    causal = _iota2((ch, ch), 0) >= _iota2((ch, ch), 1)
    ones_col = (_iota2((ch, HEAD_DIM), 1) == 0).astype(F32)

    nch = tb // ch
    probs = [(c, h) for c in range(nch) for h in range(M_HEADS)]
    q_p, k_p, vext_p = {}, {}, {}
    for c, h in probs:
        cs = slice(c * ch, (c + 1) * ch)
        q_p[c, h] = pm_ref[cs, h * HEAD_DIM:(h + 1) * HEAD_DIM]
        k_p[c, h] = (pm_ref[cs, M_WIDTH + h * HEAD_DIM:M_WIDTH + (h + 1) * HEAD_DIM]
                     * (HEAD_DIM ** -0.5))
        v = pm_ref[cs, 2 * M_WIDTH + h * HEAD_DIM:2 * M_WIDTH + (h + 1) * HEAD_DIM]
        vext_p[c, h] = jnp.concatenate([v, ones_col], axis=1)
    qk = {p: _mm_nt(q_p[p], k_p[p]) for p in probs}

    dw_p, sint_p, mt_p, wa_p, decay_p = {}, {}, {}, {}, {}
    for h in range(M_HEADS):
        m_prev = m_ref[h][0:1, 0:1]
        for c in range(nch):
            cs = slice(c * ch, (c + 1) * ch)
            b_c = bcol[cs, M_HEADS + h:M_HEADS + h + 1]
            b_r = brow[M_HEADS + h:M_HEADS + h + 1, cs]
            d_mat = jnp.where(causal, b_c - b_r + grow[h:h + 1, cs], NEG)
            inter = b_c + m_prev
            m_t = jnp.maximum(jnp.max(d_mat, axis=-1, keepdims=True), inter)
            dw_p[c, h] = jnp.where(causal, jnp.exp(d_mat - m_t), 0.0)
            sint_p[c, h] = jnp.exp(inter - m_t)
            mt_p[c, h] = m_t
            g = b_c[ch - 1:ch, :]
            a_c = g - b_c + gcol[cs, h:h + 1]
            m_new = jnp.maximum(g + m_prev, jnp.max(a_c, axis=0, keepdims=True))
            wa_p[c, h] = jnp.exp(a_c - m_new)
            decay_p[c, h] = jnp.exp(g + m_prev - m_new)
            m_prev = m_new
        m_ref[h] = jnp.broadcast_to(m_prev, m_ref.shape[1:])

    upd = {p: _mm_tn(vext_p[p], wa_p[p] * k_p[p]) for p in probs}
    cext_p = {}
    for h in range(M_HEADS):
        cext = cst_ref[h]
        for c in range(nch):
            cext_p[c, h] = cext
            cext = decay_p[c, h] * cext + upd[c, h]
        cst_ref[h] = cext
    inter_p = {p: _mm_nt(q_p[p], cext_p[p]) for p in probs}
    intra_p = {p: _mm(qk[p] * dw_p[p], vext_p[p]) for p in probs}

    for c in range(nch):
        cs = slice(c * ch, (c + 1) * ch)
        outs = []
        for h in range(M_HEADS):
            sl = slice(h * HEAD_DIM, (h + 1) * HEAD_DIM)
            tot = intra_p[c, h] + sint_p[c, h] * inter_p[c, h]
            num = tot[:, :HEAD_DIM]
            den = tot[:, HEAD_DIM:HEAD_DIM + 1]
            hh = num / jnp.maximum(jnp.abs(den), jnp.exp(-mt_p[c, h]))
            mu = jnp.mean(hh, axis=-1, keepdims=True)
            var = jnp.mean(jnp.square(hh - mu), axis=-1, keepdims=True)
            hn = (hh - mu) * lax.rsqrt(var + M_NORM_EPS)
            o_pre = pm_ref[cs, 3 * M_WIDTH + h * HEAD_DIM:3 * M_WIDTH + (h + 1) * HEAD_DIM]
            outs.append(hn * nw_ref[:, sl] * jax.nn.sigmoid(o_pre))
        out_ref[cs, :] = jnp.concatenate(outs, axis=1)


def _mlstm(pm, gate_bias, norm_w):
    b, lp, _ = pm.shape
    tb = _mix_rows(lp)
    bias = jnp.zeros((1, LANES), F32).at[0, :2 * M_HEADS].set(gate_bias.reshape(-1))
    return pl.pallas_call(
        _mlstm_kernel,
        grid=(b, lp // tb),
        in_specs=[pl.BlockSpec((None, tb, PM_W), lambda i, j: (i, j, 0)),
                  pl.BlockSpec((1, LANES), lambda i, j: (0, 0)),
                  pl.BlockSpec((1, M_WIDTH), lambda i, j: (0, 0))],
        out_specs=pl.BlockSpec((None, tb, M_WIDTH), lambda i, j: (i, j, 0)),
        out_shape=jax.ShapeDtypeStruct((b, lp, M_WIDTH), F32),
        scratch_shapes=[pltpu.VMEM((M_HEADS, 2 * HEAD_DIM, HEAD_DIM), F32),
                        pltpu.VMEM((M_HEADS, 8, LANES), F32)],
        compiler_params=_cparams(("arbitrary", "arbitrary")),
        name="mlstm",
    )(pm, bias, norm_w.reshape(1, M_WIDTH))


def _rwkv_kernel(pr_ref, prev_ref, mu_ref, w0_ref, w2_ref, a0_ref, a2_ref, g2_ref,
                 kk_ref, ka_ref, rk_ref, lnx_ref, out_ref, st_ref):
    blk = pl.program_id(1)
    tb = pr_ref.shape[0]
    ch = R_CHUNK
    rw = R_WIDTH
    hd = HEAD_DIM

    @pl.when(blk == 0)
    def _():
        st_ref[...] = jnp.zeros_like(st_ref)

    x = pr_ref[...]
    last = jnp.where(blk == 0, 0.0, prev_ref[7:8, :])
    prev = jnp.where(_iota2((tb, 1), 0) == 0, last, pltpu.roll(x, 1, axis=0))
    t = x + (prev - x) * mu_ref[...]
    r = t[:, 0:rw]
    k = t[:, rw:2 * rw]
    v = t[:, 2 * rw:3 * rw]
    o = 3 * rw
    w_lat = t[:, o:o + R_DECAY_RANK]
    a_lat = t[:, o + R_DECAY_RANK:o + R_DECAY_RANK + R_A_RANK]
    g_lat = t[:, o + R_DECAY_RANK + R_A_RANK:]

    w_log = -_softplus(-(w0_ref[...] + _mm(jnp.tanh(w_lat), w2_ref[...]))) - 0.5
    lw = -jnp.exp(w_log)
    a = jax.nn.sigmoid(a0_ref[...] + _mm(a_lat, a2_ref[...]))
    g = _mm(jax.nn.sigmoid(g_lat), g2_ref[...])

    head_ones = _chunk_mask(rw, hd, False).astype(BF16)
    kk = k * kk_ref[...]
    kk = kk / jnp.maximum(jnp.sqrt(_mm(kk * kk, head_ones)), 1e-12)
    k2 = k * (1.0 + (a - 1.0) * ka_ref[...])
    avec = -kk
    bvec = kk * a
    cum = _cumsum_rows(_chunk_mask(tb, ch, True).astype(F32), lw)

    r_i = _iota2((ch, ch), 0)
    c_i = _iota2((ch, ch), 1)
    incl = r_i >= c_i
    strict = r_i > c_i
    eye = r_i == c_i
    zeros_h = jnp.zeros((ch, hd), F32)
    n_levels = ch.bit_length() - 1

    nch = tb // ch
    probs = [(c, h) for c in range(nch) for h in range(R_HEADS)]
    ah_p, rh_p, vh_p, kt_p, bt_p, kp_p, bp_p, wc_p = ({} for _ in range(8))
    for c in range(nch):
        cs = slice(c * ch, (c + 1) * ch)
        cum_c = cum[cs]
        cl = cum_c[ch - 1:ch, :]
        inv = jnp.exp(-cum_c)
        rel = jnp.exp(cl - cum_c)
        ah_all = avec[cs] * jnp.exp(cum_c - lw[cs])
        rh_all = r[cs] * jnp.exp(cum_c)
        kt_all = k2[cs] * inv
        bt_all = bvec[cs] * inv
        kp_all = k2[cs] * rel
        bp_all = bvec[cs] * rel
        wc_all = jnp.exp(cl)
        v_c = v[cs]
        for h in range(R_HEADS):
            sl = slice(h * hd, (h + 1) * hd)
            key = (c, h)
            ah_p[key], rh_p[key], vh_p[key] = ah_all[:, sl], rh_all[:, sl], v_c[:, sl]
            kt_p[key], bt_p[key] = kt_all[:, sl], bt_all[:, sl]
            kp_p[key], bp_p[key], wc_p[key] = kp_all[:, sl], bp_all[:, sl], wc_all[:, sl]

    gram = {p: _mm_nt(jnp.concatenate([ah_p[p], rh_p[p]], axis=0),
                      jnp.concatenate([kt_p[p], bt_p[p]], axis=0)) for p in probs}
    m_ak = {p: jnp.where(strict, gram[p][:ch, :ch], 0.0) for p in probs}
    pw = {p: jnp.where(strict, gram[p][:ch, ch:], 0.0) for p in probs}
    n_cat = {p: jnp.concatenate([jnp.where(incl, gram[p][ch:, :ch], 0.0),
                                 jnp.where(incl, gram[p][ch:, ch:], 0.0)], axis=1) for p in probs}
    z = {p: jnp.concatenate([ah_p[p], _mm(m_ak[p], vh_p[p])], axis=1) for p in probs}
    for lvl in range(n_levels):
        if lvl + 1 < n_levels:
            res = {p: _mm(pw[p], jnp.concatenate([z[p], pw[p]], axis=1)) for p in probs}
            z = {p: z[p] + res[p][:, :2 * hd] for p in probs}
            pw = {p: res[p][:, 2 * hd:] for p in probs}
        else:
            z = {p: z[p] + _mm(pw[p], z[p]) for p in probs}
    rhs = {p: jnp.concatenate([jnp.concatenate([zeros_h, vh_p[p]], axis=1), z[p]], axis=0)
           for p in probs}
    ry = {p: _mm(n_cat[p], rhs[p]) for p in probs}
    tg = {p: _mm_tn(jnp.concatenate([kp_p[p], bp_p[p]], axis=0), rhs[p]) for p in probs}
    lhs = {p: jnp.concatenate([rh_p[p] + ry[p][:, :hd],
                               jnp.where(eye, wc_p[p], 0.0) + tg[p][:, :hd]], axis=0)
           for p in probs}

    sts = [st_ref[h] for h in range(R_HEADS)]
    y_chunks = []
    for c in range(nch):
        nxt = [_mm(lhs[(c, h)], sts[h]) for h in range(R_HEADS)]
        y_chunks.append(jnp.concatenate(
            [nxt[h][:ch] + ry[(c, h)][:, hd:] for h in range(R_HEADS)], axis=1))
        sts = [nxt[h][ch:] + tg[(c, h)][:, hd:] for h in range(R_HEADS)]
    for h in range(R_HEADS):
        st_ref[h] = sts[h]
    y = jnp.concatenate(y_chunks, axis=0)

    inv_d = 1.0 / hd
    mu = _mm(y, head_ones) * inv_d
    yc = y - mu
    var = _mm(yc * yc, head_ones) * inv_d
    yn = yc * lax.rsqrt(var + R_GN_EPS) * lnx_ref[0:1, :] + lnx_ref[1:2, :]
    bonus = _mm(r * k2 * rk_ref[...], head_ones) * v
    out_ref[...] = (yn + bonus) * g


def _rwkv(pr, mu, w0, w2, a0, a2, g2, k_k, k_a, r_k, ln_x):
    b, lp, _ = pr.shape
    tb = _mix_rows(lp)
    per8 = tb // 8
    row = lambda a: a.reshape(1, -1)
    full = lambda a: pl.BlockSpec(a.shape, lambda i, j: (0,) * a.ndim)
    params = [row(mu), row(w0), w2, row(a0), a2, g2, row(k_k), row(k_a), row(r_k), ln_x]
    return pl.pallas_call(
        _rwkv_kernel,
        grid=(b, lp // tb),
        in_specs=[pl.BlockSpec((None, tb, PR_W), lambda i, j: (i, j, 0)),
                  pl.BlockSpec((None, 8, PR_W), lambda i, j: (i, jnp.maximum(j * per8 - 1, 0), 0))]
                 + [full(a) for a in params],
        out_specs=pl.BlockSpec((None, tb, R_WIDTH), lambda i, j: (i, j, 0)),
        out_shape=jax.ShapeDtypeStruct((b, lp, R_WIDTH), F32),
        scratch_shapes=[pltpu.VMEM((R_HEADS, HEAD_DIM, HEAD_DIM), F32)],
        compiler_params=_cparams(("arbitrary", "arbitrary")),
        name="rwkv7",
    )(pr, pr, *params)


def _t5_bucket(dist):
    max_exact = N_BUCKETS // 2
    d = jnp.maximum(dist, 1).astype(F32)
    large = max_exact + (jnp.log(d / max_exact) / math.log(MAX_DISTANCE / max_exact)
                         * (N_BUCKETS - max_exact)).astype(jnp.int32)
    large = jnp.minimum(large, N_BUCKETS - 1)
    return jnp.where(dist < max_exact, dist, large)


def _attn_tables(rel_bias):
    blk = A_BLOCK
    i = jnp.arange(blk)[:, None]
    j = jnp.arange(blk)[None, :]
    rb = rel_bias.astype(F32)

    def bias(dist):
        onehot = jax.nn.one_hot(_t5_bucket(jnp.maximum(dist, 0)), N_BUCKETS, dtype=F32)
        return jnp.einsum("hb,ijb->hij", rb, onehot, precision=lax.Precision.HIGHEST)

    neg = jnp.full((A_HEADS, blk, blk), NEG, F32)
    is_meta = j >= FRONT_PAD
    cur_d = i - j
    prev_d = blk + i - j
    cur = jnp.where(cur_d >= 0, bias(cur_d), NEG)
    prev = jnp.where(prev_d < WINDOW, bias(prev_d), NEG)
    meta0 = jnp.where(is_meta & (cur_d >= 0), bias(cur_d), NEG)
    meta1 = jnp.where(is_meta, bias(prev_d), NEG)
    meta2 = jnp.where(is_meta, bias(2 * blk + i - j), NEG)
    case0 = jnp.concatenate([meta0, neg, neg], axis=-1)
    case1 = jnp.concatenate([meta1, neg, cur], axis=-1)
    case2 = jnp.concatenate([meta2, prev, cur], axis=-1)
    return jnp.stack([case0, case1, case2], axis=0)


def _swa_kernel(q_ref, kc_ref, kp_ref, km_ref, vc_ref, vp_ref, vm_ref, tab_ref, sink_ref, out_ref):
    scale = HEAD_DIM ** -0.5
    blk = q_ref.shape[0]
    kvs = range(A_KV_HEADS)
    heads = [range(kv * A_GROUP, (kv + 1) * A_GROUP) for kv in kvs]
    ksl = [slice(kv * HEAD_DIM, (kv + 1) * HEAD_DIM) for kv in kvs]
    s = [_mm_nt(jnp.concatenate([q_ref[:, h * HEAD_DIM:(h + 1) * HEAD_DIM] for h in heads[kv]], axis=0),
                jnp.concatenate([km_ref[:, ksl[kv]], kp_ref[:, ksl[kv]], kc_ref[:, ksl[kv]]], axis=0))
         for kv in kvs]
    p, denom = [], []
    for kv in kvs:
        tab = jnp.concatenate([tab_ref[h] for h in heads[kv]], axis=0)
        sink = jnp.concatenate([jnp.broadcast_to(sink_ref[h:h + 1, 0:1], (blk, 1)) for h in heads[kv]],
                               axis=0)
        sk = jnp.where(tab > 0.5 * NEG, s[kv] * scale + tab, NEG)
        m = jnp.maximum(jnp.max(sk, axis=-1, keepdims=True), sink)
        pk = jnp.exp(sk - m)
        p.append(pk)
        denom.append(jnp.sum(pk, axis=-1, keepdims=True) + jnp.exp(sink - m))
    o = [_mm(p[kv], jnp.concatenate([vm_ref[:, ksl[kv]], vp_ref[:, ksl[kv]], vc_ref[:, ksl[kv]]], axis=0))
         / denom[kv] for kv in kvs]
    out_ref[...] = jnp.concatenate([o[kv][g * blk:(g + 1) * blk] for kv in kvs for g in range(A_GROUP)],
                                   axis=1)


def _swa(pa, sinks, tables):
    b, lp, _ = pa.shape
    blk = A_BLOCK
    kcol = A_WIDTH // A_KV_WIDTH
    sink = jnp.broadcast_to(sinks.astype(F32)[:, None], (A_HEADS, LANES))
    kv_spec = lambda col, rowmap: pl.BlockSpec((None, blk, A_KV_WIDTH),
                                               lambda i, j: (i, rowmap(j), col))
    cur = lambda j: j
    prv = lambda j: jnp.maximum(j - 1, 0)
    first = lambda j: 0
    return pl.pallas_call(
        _swa_kernel,
        grid=(b, lp // blk),
        in_specs=[pl.BlockSpec((None, blk, A_WIDTH), lambda i, j: (i, j, 0)),
                  kv_spec(kcol, cur), kv_spec(kcol, prv), kv_spec(kcol, first),
                  kv_spec(kcol + 1, cur), kv_spec(kcol + 1, prv), kv_spec(kcol + 1, first),
                  pl.BlockSpec((None, A_HEADS, blk, 3 * blk), lambda i, j: (jnp.minimum(j, 2), 0, 0, 0)),
                  pl.BlockSpec((A_HEADS, LANES), lambda i, j: (0, 0))],
        out_specs=pl.BlockSpec((None, blk, A_WIDTH), lambda i, j: (i, j, 0)),
        out_shape=jax.ShapeDtypeStruct((b, lp, A_WIDTH), F32),
        compiler_params=_cparams(("arbitrary", "arbitrary")),
        name="swa",
    )(pa, pa, pa, pa, pa, pa, pa, tables, sink)


def _layer_norm_rows(z, ln_ref):
    mu = jnp.mean(z, axis=-1, keepdims=True)
    zc = z - mu
    var = jnp.mean(zc * zc, axis=-1, keepdims=True)
    return zc * lax.rsqrt(var + LN_EPS) * ln_ref[0:1, :] + ln_ref[1:2, :]


def _mix_kernel(ym_ref, yr_ref, ya_ref, h_ref, w_ref, ln_ref, out_ref, *, alpha):
    tm = h_ref.shape[0]
    mix = jnp.dot(ym_ref[...].astype(BF16), w_ref[0:M_WIDTH, :], preferred_element_type=F32)
    mix += jnp.dot(yr_ref[...].astype(BF16), w_ref[M_WIDTH:M_WIDTH + R_WIDTH, :],
                   preferred_element_type=F32)
    mix += jnp.dot(ya_ref[...].astype(BF16), w_ref[M_WIDTH + R_WIDTH:, :],
                   preferred_element_type=F32)
    y = _layer_norm_rows(alpha * h_ref[...] + mix, ln_ref)
    row = pl.program_id(1) * tm + _iota2((tm, 1), 0)
    out_ref[...] = jnp.where(row < FRONT_PAD, 0.0, y)


def _mix(ym, yr, ya, h, w, ln, alpha):
    b, lp, d = h.shape
    tm = _row_tile(lp)
    spec = lambda width: pl.BlockSpec((None, tm, width), lambda i, j: (i, j, 0))
    return pl.pallas_call(
        functools.partial(_mix_kernel, alpha=alpha),
        grid=(b, lp // tm),
        in_specs=[spec(M_WIDTH), spec(R_WIDTH), spec(A_WIDTH), spec(d),
                  pl.BlockSpec(w.shape, lambda i, j: (0, 0), pipeline_mode=pl.Buffered(1)),
                  pl.BlockSpec((2, d), lambda i, j: (0, 0))],
        out_specs=spec(d),
        out_shape=jax.ShapeDtypeStruct((b, lp, d), F32),
        compiler_params=_cparams(("arbitrary", "arbitrary")),
        name="out_proj_ln",
    )(ym, yr, ya, h, w, ln)


FF_CHUNK = 256


def _ffn_kernel(h_ref, wi_ref, wo_ref, ln_ref, out_ref, *, alpha, d_ff):
    tm = h_ref.shape[0]
    x = h_ref[...]
    xb = x.astype(BF16)
    acc = alpha * x
    for j in range(d_ff // FF_CHUNK):
        cs = slice(j * FF_CHUNK, (j + 1) * FF_CHUNK)
        us = slice(d_ff + j * FF_CHUNK, d_ff + (j + 1) * FF_CHUNK)
        gate = jnp.dot(xb, wi_ref[:, cs], preferred_element_type=F32)
        up = jnp.dot(xb, wi_ref[:, us], preferred_element_type=F32)
        act = (gate * jax.nn.sigmoid(gate) * up).astype(BF16)
        acc += jnp.dot(act, wo_ref[cs, :], preferred_element_type=F32)
    y = _layer_norm_rows(acc, ln_ref)
    row = pl.program_id(1) * tm + _iota2((tm, 1), 0)
    out_ref[...] = jnp.where(row < FRONT_PAD, 0.0, y)


def _ffn(h, wi, wo, ln, alpha):
    b, lp, d = h.shape
    d_ff = wo.shape[0]
    assert d_ff % FF_CHUNK == 0
    tm = _row_tile(lp)
    spec = pl.BlockSpec((None, tm, d), lambda i, j: (i, j, 0))
    const = lambda a: pl.BlockSpec(a.shape, lambda i, j: (0, 0), pipeline_mode=pl.Buffered(1))
    return pl.pallas_call(
        functools.partial(_ffn_kernel, alpha=alpha, d_ff=d_ff),
        grid=(b, lp // tm),
        in_specs=[spec, const(wi), const(wo), pl.BlockSpec((2, d), lambda i, j: (0, 0))],
        out_specs=spec,
        out_shape=jax.ShapeDtypeStruct((b, lp, d), F32),
        compiler_params=_cparams(("arbitrary", "arbitrary")),
        name="ffn_ln",
    )(h, wi, wo, ln)


def kernel(x, meta_tokens, rel_bias, w_in, m_gate_bias, m_norm_w, r_mu_rkv, r_mu_w, r_mu_a, r_mu_g, r_w0, r_w2, r_a0, r_a2, r_g2, r_k_k, r_k_a, r_r_k, r_ln_x, a_sinks, w_out, ln_mix, w_ff_in, w_ff_out, ln_ffn):
    b, seq, d = x.shape
    depth = w_in.shape[0]
    alpha = (2 * depth) ** 0.25
    assert (TOKEN_START + seq) % A_BLOCK == 0

    meta = jnp.broadcast_to(meta_tokens[None].astype(x.dtype), (b, N_META, d))
    h = jnp.concatenate([jnp.zeros((b, FRONT_PAD, d), x.dtype), meta, x], axis=1)
    tables = _attn_tables(rel_bias)
    a_end = PA_W
    m_end = a_end + M_RAW

    for l in range(depth):
        wl = w_in[l]
        w_pad = jnp.concatenate(
            [wl[:, :m_end], jnp.zeros((d, PM_W - M_RAW), wl.dtype), wl[:, m_end:]], axis=1).astype(BF16)
        pa, pm, pr = _proj(h, w_pad)
        y_m = _mlstm(pm, m_gate_bias[l], m_norm_w[l])
        mu = jnp.concatenate([r_mu_rkv[l].reshape(-1), r_mu_w[l], r_mu_a[l], r_mu_g[l]])
        y_r = _rwkv(pr, mu, r_w0[l], r_w2[l], r_a0[l], r_a2[l], r_g2[l], r_k_k[l], r_k_a[l],
                    r_r_k[l], r_ln_x[l])
        y_a = _swa(pa, a_sinks[l], tables)
        h = _mix(y_m, y_r, y_a, h, w_out[l].astype(BF16), ln_mix[l], alpha)
        h = _ffn(h, w_ff_in[l].astype(BF16), w_ff_out[l].astype(BF16), ln_ffn[l], alpha)
    return h[:, TOKEN_START:]
```

```python
import functools
import math

import jax
import jax.numpy as jnp
from jax import lax
from jax.experimental import pallas as pl
from jax.experimental.pallas import tpu as pltpu

F32 = jnp.float32
BF16 = jnp.bfloat16

HEAD_DIM = 64
N_META = 16
M_HEADS = 4
M_WIDTH = M_HEADS * HEAD_DIM
M_CHUNK = 64
M_NORM_EPS = 1e-6
R_HEADS = 4
R_WIDTH = R_HEADS * HEAD_DIM
R_CHUNK = 64
R_DECAY_RANK = 32
R_A_RANK = 32
R_GATE_RANK = 64
R_GN_EPS = 64e-5
A_HEADS = 8
A_KV_HEADS = 2
A_GROUP = A_HEADS // A_KV_HEADS
A_WIDTH = A_HEADS * HEAD_DIM
A_KV_WIDTH = A_KV_HEADS * HEAD_DIM
WINDOW = 128
A_BLOCK = 128
N_BUCKETS = 32
MAX_DISTANCE = 128
LN_EPS = 1e-5
NEG = -1e30

LANES = 128
FRONT_PAD = A_BLOCK - N_META
TOKEN_START = FRONT_PAD + N_META

PA_W = A_WIDTH + 2 * A_KV_WIDTH
PM_W = 4 * M_WIDTH + LANES
PR_W = 3 * R_WIDTH + R_DECAY_RANK + R_A_RANK + R_GATE_RANK
M_RAW = 4 * M_WIDTH + 2 * M_HEADS

VMEM_LIMIT = 56 * 1024 * 1024
MIX_ROWS = 256


def _row_tile(n):
    for t in (640, 512, 256, 128, 64):
        if n % t == 0:
            return t
    raise ValueError(f"row count {n} has no supported tile")


def _mix_rows(lp):
    return MIX_ROWS if lp % MIX_ROWS == 0 else A_BLOCK


def _cparams(sem):
    return pltpu.CompilerParams(dimension_semantics=sem, vmem_limit_bytes=VMEM_LIMIT)


def _bf(x):
    return x.astype(BF16)


def _mm(a, b):
    return lax.dot_general(_bf(a), _bf(b), (((1,), (0,)), ((), ())), preferred_element_type=F32)


def _mm_nt(a, b):
    return lax.dot_general(_bf(a), _bf(b), (((1,), (1,)), ((), ())), preferred_element_type=F32)


def _mm_tn(a, b):
    return lax.dot_general(_bf(a), _bf(b), (((0,), (0,)), ((), ())), preferred_element_type=F32)


def _iota2(shape, dim):
    return lax.broadcasted_iota(jnp.int32, shape, dim)


def _split3(x):
    hi = _bf(x).astype(F32)
    r1 = x - hi
    mid = _bf(r1).astype(F32)
    return hi, mid, r1 - mid


def _chunk_mask(n, chunk, lower):
    r_i = _iota2((n, n), 0)
    c_i = _iota2((n, n), 1)
    sh = chunk.bit_length() - 1
    same = jnp.right_shift(r_i, sh) == jnp.right_shift(c_i, sh)
    return same & (r_i >= c_i) if lower else same


def _cumsum_rows(tri, x):
    w = x.shape[1]
    res = _mm(tri, jnp.concatenate(_split3(x), axis=1))
    return res[:, :w] + res[:, w:2 * w] + res[:, 2 * w:]


def _cumsum_lanes(tri, x):
    r = x.shape[0]
    res = _mm_nt(jnp.concatenate(_split3(x), axis=0), tri)
    return res[:r] + res[r:2 * r] + res[2 * r:]


def _softplus(x):
    return jnp.maximum(x, 0.0) + jnp.log1p(jnp.exp(-jnp.abs(x)))


def _proj_kernel(x_ref, w_ref, oa_ref, om_ref, or_ref):
    acc = jnp.dot(x_ref[...].astype(BF16), w_ref[...], preferred_element_type=F32)
    oa_ref[...] = acc[:, :PA_W]
    om_ref[...] = acc[:, PA_W:PA_W + PM_W]
    or_ref[...] = acc[:, PA_W + PM_W:]


def _proj(h, w):
    b, lp, d = h.shape
    tm = _row_tile(lp)
    n = w.shape[1]
    spec = lambda width: pl.BlockSpec((None, tm, width), lambda i, j: (i, j, 0))
    return pl.pallas_call(
        _proj_kernel,
        grid=(b, lp // tm),
        in_specs=[spec(d),
                  pl.BlockSpec((d, n), lambda i, j: (0, 0), pipeline_mode=pl.Buffered(1))],
        out_specs=[spec(PA_W), spec(PM_W), spec(PR_W)],
        out_shape=[jax.ShapeDtypeStruct((b, lp, PA_W), F32),
                   jax.ShapeDtypeStruct((b, lp, PM_W), F32),
                   jax.ShapeDtypeStruct((b, lp, PR_W), F32)],
        compiler_params=_cparams(("arbitrary", "arbitrary")),
        name="in_proj",
    )(h, w)


def _mlstm_kernel(pm_ref, bias_ref, nw_ref, out_ref, cst_ref, m_ref):
    blk = pl.program_id(1)
    tb = pm_ref.shape[0]
    ch = M_CHUNK
    hd = HEAD_DIM
    nch = tb // ch
    pair_w = 2 * hd

    @pl.when(blk == 0)
    def _():
        cst_ref[...] = jnp.zeros_like(cst_ref)
        m_ref[...] = jnp.full_like(m_ref, NEG)

    row = blk * tb + _iota2((tb, 1), 0)
    is_pad = row < FRONT_PAD
    gb = pm_ref[:, 4 * M_WIDTH:] + bias_ref[...]
    lane = _iota2((tb, LANES), 1)
    li = jnp.where(is_pad, NEG, gb)
    lf = jnp.where(is_pad, 0.0, -_softplus(-gb))
    gcol = jnp.where(lane < M_HEADS, li, jnp.where(lane < 2 * M_HEADS, lf, 0.0))
    sel = (_iota2((8, LANES), 0) == _iota2((8, LANES), 1)).astype(F32)
    grow = _mm_nt(jnp.concatenate([sel] * 3, axis=1), jnp.concatenate(_split3(gcol), axis=1))
    brow = _cumsum_lanes(_chunk_mask(tb, ch, True).astype(F32), grow)
    r8 = _iota2((8, tb), 0)
    t8 = jnp.bitwise_and(_iota2((8, tb), 1), ch - 1)
    cm = grow - pltpu.roll(brow, M_HEADS, axis=0)
    sh = 1
    while sh < ch:
        cm = jnp.where(t8 >= sh, jnp.maximum(cm, pltpu.roll(cm, sh, axis=1)), cm)
        sh *= 2
    x16 = jnp.concatenate([jnp.where(r8 < M_HEADS, grow, brow), cm], axis=0)
    nrep = 3 * M_HEADS * pair_w
    e_sel = (_iota2((16, nrep), 0) == jnp.right_shift(_iota2((16, nrep), 1), 7)).astype(F32)
    rep = _mm_tn(jnp.concatenate(_split3(x16), axis=0), jnp.concatenate([e_sel] * 3, axis=0))
    rep_of = lambda qn, h, cs: rep[cs, (qn * M_HEADS + h) * pair_w:(qn * M_HEADS + h + 1) * pair_w]

    causal = _iota2((ch, ch), 0) >= _iota2((ch, ch), 1)
    low_half = _iota2((ch, pair_w), 1) < hd

    probs = [(c, h) for c in range(nch) for h in range(M_HEADS)]
    q_p, k_p, vext_p = {}, {}, {}
    for c, h in probs:
        cs = slice(c * ch, (c + 1) * ch)
        ps = slice((h // 2) * pair_w, (h // 2 + 1) * pair_w)
        mine = low_half if h % 2 == 0 else jnp.logical_not(low_half)
        q_p[c, h] = jnp.where(mine, pm_ref[cs, ps], 0.0)
        k_p[c, h] = jnp.where(mine, pm_ref[cs, M_WIDTH + ps.start:M_WIDTH + ps.stop] * (hd ** -0.5), 0.0)
        vext_p[c, h] = jnp.where(mine, pm_ref[cs, 2 * M_WIDTH + ps.start:2 * M_WIDTH + ps.stop], 1.0)
    qk = {p: _mm_nt(q_p[p], k_p[p]) for p in probs}

    dw_p, sint_p, emt_p, wa_p, decay_p = {}, {}, {}, {}, {}
    for h in range(M_HEADS):
        m_prev = m_ref[h][0:1, :]
        for c in range(nch):
            cs = slice(c * ch, (c + 1) * ch)
            li_c, b_c, cm_c = rep_of(0, h, cs), rep_of(1, h, cs), rep_of(2, h, cs)
            b_r = brow[M_HEADS + h:M_HEADS + h + 1, cs]
            d_mat = jnp.where(causal, b_c[:, :ch] - b_r + grow[h:h + 1, cs], NEG)
            inter = b_c + m_prev
            m_t = jnp.maximum(b_c + cm_c, inter)
            dw_p[c, h] = jnp.where(causal, jnp.exp(d_mat - m_t[:, :ch]), 0.0)
            sint_p[c, h] = jnp.exp(inter - m_t)
            emt_p[c, h] = jnp.exp(-m_t)
            g = b_c[ch - 1:ch, :]
            m_new = jnp.maximum(g + m_prev, g + cm_c[ch - 1:ch, :])
            wa_p[c, h] = jnp.exp(g - b_c + li_c - m_new)
            decay_p[c, h] = jnp.exp(g + m_prev - m_new)
            m_prev = m_new
        m_ref[h] = jnp.broadcast_to(m_prev, m_ref.shape[1:])

    upd = {p: _mm_tn(vext_p[p], wa_p[p] * k_p[p]) for p in probs}
    cext_p = {}
    for h in range(M_HEADS):
        cext = cst_ref[h]
        for c in range(nch):
            cext_p[c, h] = cext
            cext = decay_p[c, h] * cext + upd[c, h]
        cst_ref[h] = cext
    inter_p = {p: _mm_nt(q_p[p], cext_p[p]) for p in probs}
    intra_p = {p: _mm(qk[p] * dw_p[p], vext_p[p]) for p in probs}

    rows = []
    for c in range(nch):
        pairs = []
        for j in range(M_HEADS // 2):
            halves = []
            for h in (2 * j, 2 * j + 1):
                tot = intra_p[c, h] + sint_p[c, h] * inter_p[c, h]
                den = pltpu.roll(tot, hd, axis=1)
                halves.append(tot / jnp.maximum(jnp.abs(den), emt_p[c, h]))
            pairs.append(jnp.where(low_half, halves[0], halves[1]))
        rows.append(jnp.concatenate(pairs, axis=1))
    hh = jnp.concatenate(rows, axis=0)

    head_ones = _chunk_mask(M_WIDTH, hd, False).astype(BF16)
    mu = _mm(hh, head_ones) * (1.0 / hd)
    hc = hh - mu
    var = _mm(hc * hc, head_ones) * (1.0 / hd)
    o_pre = pm_ref[:, 3 * M_WIDTH:4 * M_WIDTH]
    out_ref[...] = hc * lax.rsqrt(var + M_NORM_EPS) * nw_ref[...] * jax.nn.sigmoid(o_pre)


def _mlstm(pm, gate_bias, norm_w):
    b, lp, _ = pm.shape
    tb = _mix_rows(lp)
    bias = jnp.zeros((1, LANES), F32).at[0, :2 * M_HEADS].set(gate_bias.reshape(-1))
    return pl.pallas_call(
        _mlstm_kernel,
        grid=(b, lp // tb),
        in_specs=[pl.BlockSpec((None, tb, PM_W), lambda i, j: (i, j, 0)),
                  pl.BlockSpec((1, LANES), lambda i, j: (0, 0)),
                  pl.BlockSpec((1, M_WIDTH), lambda i, j: (0, 0))],
        out_specs=pl.BlockSpec((None, tb, M_WIDTH), lambda i, j: (i, j, 0)),
        out_shape=jax.ShapeDtypeStruct((b, lp, M_WIDTH), F32),
        scratch_shapes=[pltpu.VMEM((M_HEADS, 2 * HEAD_DIM, 2 * HEAD_DIM), F32),
                        pltpu.VMEM((M_HEADS, 8, LANES), F32)],
        compiler_params=_cparams(("arbitrary", "arbitrary")),
        name="mlstm",
    )(pm, bias, norm_w.reshape(1, M_WIDTH))


def _rwkv_kernel(pr_ref, prev_ref, mu_ref, w0_ref, w2_ref, a0_ref, a2_ref, g2_ref,
                 kk_ref, ka_ref, rk_ref, lnx_ref, out_ref, st_ref):
    blk = pl.program_id(1)
    tb = pr_ref.shape[0]
    ch = R_CHUNK
    rw = R_WIDTH
    hd = HEAD_DIM

    @pl.when(blk == 0)
    def _():
        st_ref[...] = jnp.zeros_like(st_ref)

    x = pr_ref[...]
    last = jnp.where(blk == 0, 0.0, prev_ref[7:8, :])
    prev = jnp.where(_iota2((tb, 1), 0) == 0, last, pltpu.roll(x, 1, axis=0))
    t = x + (prev - x) * mu_ref[...]
    r = t[:, 0:rw]
    k = t[:, rw:2 * rw]
    v = t[:, 2 * rw:3 * rw]
    o = 3 * rw
    w_lat = t[:, o:o + R_DECAY_RANK]
    a_lat = t[:, o + R_DECAY_RANK:o + R_DECAY_RANK + R_A_RANK]
    g_lat = t[:, o + R_DECAY_RANK + R_A_RANK:]

    w_log = -_softplus(-(w0_ref[...] + _mm(jnp.tanh(w_lat), w2_ref[...]))) - 0.5
    lw = -jnp.exp(w_log)
    a = jax.nn.sigmoid(a0_ref[...] + _mm(a_lat, a2_ref[...]))
    g = _mm(jax.nn.sigmoid(g_lat), g2_ref[...])

    head_ones = _chunk_mask(rw, hd, False).astype(BF16)
    kk = k * kk_ref[...]
    kk = kk / jnp.maximum(jnp.sqrt(_mm(kk * kk, head_ones)), 1e-12)
    k2 = k * (1.0 + (a - 1.0) * ka_ref[...])
    avec = -kk
    bvec = kk * a
    cum = _cumsum_rows(_chunk_mask(tb, ch, True).astype(F32), lw)

    r_i = _iota2((ch, ch), 0)
    c_i = _iota2((ch, ch), 1)
    incl = r_i >= c_i
    strict = r_i > c_i
    eye = r_i == c_i
    zeros_h = jnp.zeros((ch, hd), F32)
    n_levels = ch.bit_length() - 1

    nch = tb // ch
    probs = [(c, h) for c in range(nch) for h in range(R_HEADS)]
    ah_p, rh_p, vh_p, kt_p, bt_p, kp_p, bp_p, wc_p = ({} for _ in range(8))
    for c in range(nch):
        cs = slice(c * ch, (c + 1) * ch)
        cum_c = cum[cs]
        cl = cum_c[ch - 1:ch, :]
        inv = jnp.exp(-cum_c)
        rel = jnp.exp(cl - cum_c)
        ah_all = avec[cs] * jnp.exp(cum_c - lw[cs])
        rh_all = r[cs] * jnp.exp(cum_c)
        kt_all = k2[cs] * inv
        bt_all = bvec[cs] * inv
        kp_all = k2[cs] * rel
        bp_all = bvec[cs] * rel
        wc_all = jnp.exp(cl)
        v_c = v[cs]
        for h in range(R_HEADS):
            sl = slice(h * hd, (h + 1) * hd)
            key = (c, h)
            ah_p[key], rh_p[key], vh_p[key] = ah_all[:, sl], rh_all[:, sl], v_c[:, sl]
            kt_p[key], bt_p[key] = kt_all[:, sl], bt_all[:, sl]
            kp_p[key], bp_p[key], wc_p[key] = kp_all[:, sl], bp_all[:, sl], wc_all[:, sl]

    gram = {p: _mm_nt(jnp.concatenate([ah_p[p], rh_p[p]], axis=0),
                      jnp.concatenate([kt_p[p], bt_p[p]], axis=0)) for p in probs}
    m_ak = {p: jnp.where(strict, gram[p][:ch, :ch], 0.0) for p in probs}
    pw = {p: jnp.where(strict, gram[p][:ch, ch:], 0.0) for p in probs}
    n_cat = {p: jnp.concatenate([jnp.where(incl, gram[p][ch:, :ch], 0.0),
                                 jnp.where(incl, gram[p][ch:, ch:], 0.0)], axis=1) for p in probs}
    z = {p: jnp.concatenate([ah_p[p], _mm(m_ak[p], vh_p[p])], axis=1) for p in probs}
    for lvl in range(n_levels):
        if lvl + 1 < n_levels:
            res = {p: _mm(pw[p], jnp.concatenate([z[p], pw[p]], axis=1)) for p in probs}
            z = {p: z[p] + res[p][:, :2 * hd] for p in probs}
            pw = {p: res[p][:, 2 * hd:] for p in probs}
        else:
            z = {p: z[p] + _mm(pw[p], z[p]) for p in probs}
    rhs = {p: jnp.concatenate([jnp.concatenate([zeros_h, vh_p[p]], axis=1), z[p]], axis=0)
           for p in probs}
    ry = {p: _mm(n_cat[p], rhs[p]) for p in probs}
    tg = {p: _mm_tn(jnp.concatenate([kp_p[p], bp_p[p]], axis=0), rhs[p]) for p in probs}
    lhs = {p: jnp.concatenate([rh_p[p] + ry[p][:, :hd],
                               jnp.where(eye, wc_p[p], 0.0) + tg[p][:, :hd]], axis=0)
           for p in probs}

    sts = [st_ref[h] for h in range(R_HEADS)]
    y_chunks = []
    for c in range(nch):
        nxt = [_mm(lhs[(c, h)], sts[h]) for h in range(R_HEADS)]
        y_chunks.append(jnp.concatenate(
            [nxt[h][:ch] + ry[(c, h)][:, hd:] for h in range(R_HEADS)], axis=1))
        sts = [nxt[h][ch:] + tg[(c, h)][:, hd:] for h in range(R_HEADS)]
    for h in range(R_HEADS):
        st_ref[h] = sts[h]
    y = jnp.concatenate(y_chunks, axis=0)

    inv_d = 1.0 / hd
    mu = _mm(y, head_ones) * inv_d
    yc = y - mu
    var = _mm(yc * yc, head_ones) * inv_d
    yn = yc * lax.rsqrt(var + R_GN_EPS) * lnx_ref[0:1, :] + lnx_ref[1:2, :]
    bonus = _mm(r * k2 * rk_ref[...], head_ones) * v
    out_ref[...] = (yn + bonus) * g


def _rwkv(pr, mu, w0, w2, a0, a2, g2, k_k, k_a, r_k, ln_x):
    b, lp, _ = pr.shape
    tb = _mix_rows(lp)
    per8 = tb // 8
    row = lambda a: a.reshape(1, -1)
    full = lambda a: pl.BlockSpec(a.shape, lambda i, j: (0,) * a.ndim)
    params = [row(mu), row(w0), w2, row(a0), a2, g2, row(k_k), row(k_a), row(r_k), ln_x]
    return pl.pallas_call(
        _rwkv_kernel,
        grid=(b, lp // tb),
        in_specs=[pl.BlockSpec((None, tb, PR_W), lambda i, j: (i, j, 0)),
                  pl.BlockSpec((None, 8, PR_W), lambda i, j: (i, jnp.maximum(j * per8 - 1, 0), 0))]
                 + [full(a) for a in params],
        out_specs=pl.BlockSpec((None, tb, R_WIDTH), lambda i, j: (i, j, 0)),
        out_shape=jax.ShapeDtypeStruct((b, lp, R_WIDTH), F32),
        scratch_shapes=[pltpu.VMEM((R_HEADS, HEAD_DIM, HEAD_DIM), F32)],
        compiler_params=_cparams(("arbitrary", "arbitrary")),
        name="rwkv7",
    )(pr, pr, *params)


def _t5_bucket(dist):
    max_exact = N_BUCKETS // 2
    d = jnp.maximum(dist, 1).astype(F32)
    large = max_exact + (jnp.log(d / max_exact) / math.log(MAX_DISTANCE / max_exact)
                         * (N_BUCKETS - max_exact)).astype(jnp.int32)
    large = jnp.minimum(large, N_BUCKETS - 1)
    return jnp.where(dist < max_exact, dist, large)


def _attn_tables(rel_bias):
    blk = A_BLOCK
    i = jnp.arange(blk)[:, None]
    j = jnp.arange(blk)[None, :]
    rb = rel_bias.astype(F32)

    def bias(dist):
        onehot = jax.nn.one_hot(_t5_bucket(jnp.maximum(dist, 0)), N_BUCKETS, dtype=F32)
        return jnp.einsum("hb,ijb->hij", rb, onehot, precision=lax.Precision.HIGHEST)

    neg = jnp.full((A_HEADS, blk, blk), NEG, F32)
    is_meta = j >= FRONT_PAD
    cur_d = i - j
    prev_d = blk + i - j
    cur = jnp.where(cur_d >= 0, bias(cur_d), NEG)
    prev = jnp.where(prev_d < WINDOW, bias(prev_d), NEG)
    meta0 = jnp.where(is_meta & (cur_d >= 0), bias(cur_d), NEG)
    meta1 = jnp.where(is_meta, bias(prev_d), NEG)
    meta2 = jnp.where(is_meta, bias(2 * blk + i - j), NEG)
    case0 = jnp.concatenate([meta0, neg, neg], axis=-1)
    case1 = jnp.concatenate([meta1, neg, cur], axis=-1)
    case2 = jnp.concatenate([meta2, prev, cur], axis=-1)
    return jnp.stack([case0, case1, case2], axis=0)


def _swa_kernel(q_ref, kc_ref, kp_ref, km_ref, vc_ref, vp_ref, vm_ref, tab_ref, sink_ref, out_ref):
    scale = HEAD_DIM ** -0.5
    blk = q_ref.shape[0]
    kvs = range(A_KV_HEADS)
    heads = [range(kv * A_GROUP, (kv + 1) * A_GROUP) for kv in kvs]
    ksl = [slice(kv * HEAD_DIM, (kv + 1) * HEAD_DIM) for kv in kvs]
    s = [_mm_nt(jnp.concatenate([q_ref[:, h * HEAD_DIM:(h + 1) * HEAD_DIM] for h in heads[kv]], axis=0),
                jnp.concatenate([km_ref[:, ksl[kv]], kp_ref[:, ksl[kv]], kc_ref[:, ksl[kv]]], axis=0))
         for kv in kvs]
    p, denom = [], []
    for kv in kvs:
        tab = jnp.concatenate([tab_ref[h] for h in heads[kv]], axis=0)
        sink = jnp.concatenate([jnp.broadcast_to(sink_ref[h:h + 1, 0:1], (blk, 1)) for h in heads[kv]],
                               axis=0)
        sk = jnp.where(tab > 0.5 * NEG, s[kv] * scale + tab, NEG)
        m = jnp.maximum(jnp.max(sk, axis=-1, keepdims=True), sink)
        pk = jnp.exp(sk - m)
        p.append(pk)
        denom.append(jnp.sum(pk, axis=-1, keepdims=True) + jnp.exp(sink - m))
    o = [_mm(p[kv], jnp.concatenate([vm_ref[:, ksl[kv]], vp_ref[:, ksl[kv]], vc_ref[:, ksl[kv]]], axis=0))
         / denom[kv] for kv in kvs]
    out_ref[...] = jnp.concatenate([o[kv][g * blk:(g + 1) * blk] for kv in kvs for g in range(A_GROUP)],
                                   axis=1)


def _swa(pa, sinks, tables):
    b, lp, _ = pa.shape
    blk = A_BLOCK
    kcol = A_WIDTH // A_KV_WIDTH
    sink = jnp.broadcast_to(sinks.astype(F32)[:, None], (A_HEADS, LANES))
    kv_spec = lambda col, rowmap: pl.BlockSpec((None, blk, A_KV_WIDTH),
                                               lambda i, j: (i, rowmap(j), col))
    cur = lambda j: j
    prv = lambda j: jnp.maximum(j - 1, 0)
    first = lambda j: 0
    return pl.pallas_call(
        _swa_kernel,
        grid=(b, lp // blk),
        in_specs=[pl.BlockSpec((None, blk, A_WIDTH), lambda i, j: (i, j, 0)),
                  kv_spec(kcol, cur), kv_spec(kcol, prv), kv_spec(kcol, first),
                  kv_spec(kcol + 1, cur), kv_spec(kcol + 1, prv), kv_spec(kcol + 1, first),
                  pl.BlockSpec((None, A_HEADS, blk, 3 * blk), lambda i, j: (jnp.minimum(j, 2), 0, 0, 0)),
                  pl.BlockSpec((A_HEADS, LANES), lambda i, j: (0, 0))],
        out_specs=pl.BlockSpec((None, blk, A_WIDTH), lambda i, j: (i, j, 0)),
        out_shape=jax.ShapeDtypeStruct((b, lp, A_WIDTH), F32),
        compiler_params=_cparams(("arbitrary", "arbitrary")),
        name="swa",
    )(pa, pa, pa, pa, pa, pa, pa, tables, sink)


def _layer_norm_rows(z, ln_ref):
    mu = jnp.mean(z, axis=-1, keepdims=True)
    zc = z - mu
    var = jnp.mean(zc * zc, axis=-1, keepdims=True)
    return zc * lax.rsqrt(var + LN_EPS) * ln_ref[0:1, :] + ln_ref[1:2, :]


def _mix_kernel(ym_ref, yr_ref, ya_ref, h_ref, w_ref, ln_ref, out_ref, *, alpha):
    tm = h_ref.shape[0]
    mix = jnp.dot(ym_ref[...].astype(BF16), w_ref[0:M_WIDTH, :], preferred_element_type=F32)
    mix += jnp.dot(yr_ref[...].astype(BF16), w_ref[M_WIDTH:M_WIDTH + R_WIDTH, :],
                   preferred_element_type=F32)
    mix += jnp.dot(ya_ref[...].astype(BF16), w_ref[M_WIDTH + R_WIDTH:, :],
                   preferred_element_type=F32)
    y = _layer_norm_rows(alpha * h_ref[...] + mix, ln_ref)
    row = pl.program_id(1) * tm + _iota2((tm, 1), 0)
    out_ref[...] = jnp.where(row < FRONT_PAD, 0.0, y)


def _mix(ym, yr, ya, h, w, ln, alpha):
    b, lp, d = h.shape
    tm = _row_tile(lp)
    spec = lambda width: pl.BlockSpec((None, tm, width), lambda i, j: (i, j, 0))
    return pl.pallas_call(
        functools.partial(_mix_kernel, alpha=alpha),
        grid=(b, lp // tm),
        in_specs=[spec(M_WIDTH), spec(R_WIDTH), spec(A_WIDTH), spec(d),
                  pl.BlockSpec(w.shape, lambda i, j: (0, 0), pipeline_mode=pl.Buffered(1)),
                  pl.BlockSpec((2, d), lambda i, j: (0, 0))],
        out_specs=spec(d),
        out_shape=jax.ShapeDtypeStruct((b, lp, d), F32),
        compiler_params=_cparams(("arbitrary", "arbitrary")),
        name="out_proj_ln",
    )(ym, yr, ya, h, w, ln)


FF_CHUNK = 256


def _ffn_kernel(h_ref, wi_ref, wo_ref, ln_ref, out_ref, *, alpha, d_ff):
    tm = h_ref.shape[0]
    x = h_ref[...]
    xb = x.astype(BF16)
    acc = alpha * x
    for j in range(d_ff // FF_CHUNK):
        cs = slice(j * FF_CHUNK, (j + 1) * FF_CHUNK)
        us = slice(d_ff + j * FF_CHUNK, d_ff + (j + 1) * FF_CHUNK)
        gate = jnp.dot(xb, wi_ref[:, cs], preferred_element_type=F32)
        up = jnp.dot(xb, wi_ref[:, us], preferred_element_type=F32)
        act = (gate * jax.nn.sigmoid(gate) * up).astype(BF16)
        acc += jnp.dot(act, wo_ref[cs, :], preferred_element_type=F32)
    y = _layer_norm_rows(acc, ln_ref)
    row = pl.program_id(1) * tm + _iota2((tm, 1), 0)
    out_ref[...] = jnp.where(row < FRONT_PAD, 0.0, y)


def _ffn(h, wi, wo, ln, alpha):
    b, lp, d = h.shape
    d_ff = wo.shape[0]
    assert d_ff % FF_CHUNK == 0
    tm = _row_tile(lp)
    spec = pl.BlockSpec((None, tm, d), lambda i, j: (i, j, 0))
    const = lambda a: pl.BlockSpec(a.shape, lambda i, j: (0, 0), pipeline_mode=pl.Buffered(1))
    return pl.pallas_call(
        functools.partial(_ffn_kernel, alpha=alpha, d_ff=d_ff),
        grid=(b, lp // tm),
        in_specs=[spec, const(wi), const(wo), pl.BlockSpec((2, d), lambda i, j: (0, 0))],
        out_specs=spec,
        out_shape=jax.ShapeDtypeStruct((b, lp, d), F32),
        compiler_params=_cparams(("arbitrary", "arbitrary")),
        name="ffn_ln",
    )(h, wi, wo, ln)


def kernel(x, meta_tokens, rel_bias, w_in, m_gate_bias, m_norm_w, r_mu_rkv, r_mu_w, r_mu_a, r_mu_g, r_w0, r_w2, r_a0, r_a2, r_g2, r_k_k, r_k_a, r_r_k, r_ln_x, a_sinks, w_out, ln_mix, w_ff_in, w_ff_out, ln_ffn):
    b, seq, d = x.shape
    depth = w_in.shape[0]
    alpha = (2 * depth) ** 0.25
    assert (TOKEN_START + seq) % A_BLOCK == 0

    meta = jnp.broadcast_to(meta_tokens[None].astype(x.dtype), (b, N_META, d))
    h = jnp.concatenate([jnp.zeros((b, FRONT_PAD, d), x.dtype), meta, x], axis=1)
    tables = _attn_tables(rel_bias)
    a_end = PA_W
    m_end = a_end + M_RAW

    for l in range(depth):
        wl = w_in[l]
        w_pad = jnp.concatenate(
            [wl[:, :m_end], jnp.zeros((d, PM_W - M_RAW), wl.dtype), wl[:, m_end:]], axis=1).astype(BF16)
        pa, pm, pr = _proj(h, w_pad)
        y_m = _mlstm(pm, m_gate_bias[l], m_norm_w[l])
        mu = jnp.concatenate([r_mu_rkv[l].reshape(-1), r_mu_w[l], r_mu_a[l], r_mu_g[l]])
        y_r = _rwkv(pr, mu, r_w0[l], r_w2[l], r_a0[l], r_a2[l], r_g2[l], r_k_k[l], r_k_a[l],
                    r_r_k[l], r_ln_x[l])
        y_a = _swa(pa, a_sinks[l], tables)
        h = _mix(y_m, y_r, y_a, h, w_out[l].astype(BF16), ln_mix[l], alpha)
        h = _ffn(h, w_ff_in[l].astype(BF16), w_ff_out[l].astype(BF16), ln_ffn[l], alpha)
    return h[:, TOKEN_START:]
```

```python
import functools
import math

import jax
import jax.numpy as jnp
from jax import lax
from jax.experimental import pallas as pl
from jax.experimental.pallas import tpu as pltpu

F32 = jnp.float32
BF16 = jnp.bfloat16

HEAD_DIM = 64
N_META = 16
M_HEADS = 4
M_WIDTH = M_HEADS * HEAD_DIM
M_CHUNK = 64
M_NORM_EPS = 1e-6
R_HEADS = 4
R_WIDTH = R_HEADS * HEAD_DIM
R_CHUNK = 64
R_DECAY_RANK = 32
R_A_RANK = 32
R_GATE_RANK = 64
R_GN_EPS = 64e-5
A_HEADS = 8
A_KV_HEADS = 2
A_GROUP = A_HEADS // A_KV_HEADS
A_WIDTH = A_HEADS * HEAD_DIM
A_KV_WIDTH = A_KV_HEADS * HEAD_DIM
WINDOW = 128
A_BLOCK = 128
N_BUCKETS = 32
MAX_DISTANCE = 128
LN_EPS = 1e-5
NEG = -1e30

LANES = 128
FRONT_PAD = A_BLOCK - N_META
TOKEN_START = FRONT_PAD + N_META

PA_W = A_WIDTH + 2 * A_KV_WIDTH
PM_W = 4 * M_WIDTH + LANES
PR_W = 3 * R_WIDTH + R_DECAY_RANK + R_A_RANK + R_GATE_RANK
M_RAW = 4 * M_WIDTH + 2 * M_HEADS

VMEM_LIMIT = 56 * 1024 * 1024
MIX_ROWS = 640


def _row_tile(n):
    for t in (640, 512, 256, 128, 64):
        if n % t == 0:
            return t
    raise ValueError(f"row count {n} has no supported tile")


def _mix_rows(lp):
    return MIX_ROWS if lp % MIX_ROWS == 0 else A_BLOCK


def _cparams(sem):
    return pltpu.CompilerParams(dimension_semantics=sem, vmem_limit_bytes=VMEM_LIMIT)


def _bf(x):
    return x.astype(BF16)


def _mm(a, b):
    return lax.dot_general(_bf(a), _bf(b), (((1,), (0,)), ((), ())), preferred_element_type=F32)


def _mm_nt(a, b):
    return lax.dot_general(_bf(a), _bf(b), (((1,), (1,)), ((), ())), preferred_element_type=F32)


def _mm_tn(a, b):
    return lax.dot_general(_bf(a), _bf(b), (((0,), (0,)), ((), ())), preferred_element_type=F32)


def _iota2(shape, dim):
    return lax.broadcasted_iota(jnp.int32, shape, dim)


def _split3(x):
    hi = _bf(x).astype(F32)
    r1 = x - hi
    mid = _bf(r1).astype(F32)
    return hi, mid, r1 - mid


def _chunk_mask(n, chunk, lower):
    r_i = _iota2((n, n), 0)
    c_i = _iota2((n, n), 1)
    sh = chunk.bit_length() - 1
    same = jnp.right_shift(r_i, sh) == jnp.right_shift(c_i, sh)
    return same & (r_i >= c_i) if lower else same


def _cumsum_rows(tri, x):
    w = x.shape[1]
    res = _mm(tri, jnp.concatenate(_split3(x), axis=1))
    return res[:, :w] + res[:, w:2 * w] + res[:, 2 * w:]


def _cumsum_lanes(tri, x):
    r = x.shape[0]
    res = _mm_nt(jnp.concatenate(_split3(x), axis=0), tri)
    return res[:r] + res[r:2 * r] + res[2 * r:]


def _softplus(x):
    return jnp.maximum(x, 0.0) + jnp.log1p(jnp.exp(-jnp.abs(x)))


def _proj_kernel(x_ref, w_ref, oa_ref, om_ref, or_ref):
    acc = jnp.dot(x_ref[...].astype(BF16), w_ref[...], preferred_element_type=F32)
    oa_ref[...] = acc[:, :PA_W]
    om_ref[...] = acc[:, PA_W:PA_W + PM_W]
    or_ref[...] = acc[:, PA_W + PM_W:]


def _proj(h, w):
    b, lp, d = h.shape
    tm = _row_tile(lp)
    n = w.shape[1]
    spec = lambda width: pl.BlockSpec((None, tm, width), lambda i, j: (i, j, 0))
    return pl.pallas_call(
        _proj_kernel,
        grid=(b, lp // tm),
        in_specs=[spec(d),
                  pl.BlockSpec((d, n), lambda i, j: (0, 0), pipeline_mode=pl.Buffered(1))],
        out_specs=[spec(PA_W), spec(PM_W), spec(PR_W)],
        out_shape=[jax.ShapeDtypeStruct((b, lp, PA_W), F32),
                   jax.ShapeDtypeStruct((b, lp, PM_W), F32),
                   jax.ShapeDtypeStruct((b, lp, PR_W), F32)],
        compiler_params=_cparams(("arbitrary", "arbitrary")),
        name="in_proj",
    )(h, w)


def _mlstm_kernel(pm_ref, bias_ref, nw_ref, out_ref, cst_ref, m_ref):
    blk = pl.program_id(1)
    tb = pm_ref.shape[0]
    ch = M_CHUNK
    hd = HEAD_DIM
    nch = tb // ch
    pair_w = 2 * hd

    @pl.when(blk == 0)
    def _():
        cst_ref[...] = jnp.zeros_like(cst_ref)
        m_ref[...] = jnp.full_like(m_ref, NEG)

    row = blk * tb + _iota2((tb, 1), 0)
    is_pad = row < FRONT_PAD
    gb = pm_ref[:, 4 * M_WIDTH:] + bias_ref[...]
    lane = _iota2((tb, LANES), 1)
    li = jnp.where(is_pad, NEG, gb)
    lf = jnp.where(is_pad, 0.0, -_softplus(-gb))
    gcol = jnp.where(lane < M_HEADS, li, jnp.where(lane < 2 * M_HEADS, lf, 0.0))
    sel = (_iota2((8, LANES), 0) == _iota2((8, LANES), 1)).astype(F32)
    grow = _mm_nt(jnp.concatenate([sel] * 3, axis=1), jnp.concatenate(_split3(gcol), axis=1))
    brow = _cumsum_lanes(_chunk_mask(tb, ch, True).astype(F32), grow)
    r8 = _iota2((8, tb), 0)
    t8 = jnp.bitwise_and(_iota2((8, tb), 1), ch - 1)
    cm = grow - pltpu.roll(brow, M_HEADS, axis=0)
    sh = 1
    while sh < ch:
        cm = jnp.where(t8 >= sh, jnp.maximum(cm, pltpu.roll(cm, sh, axis=1)), cm)
        sh *= 2
    x16 = jnp.concatenate([jnp.where(r8 < M_HEADS, grow, brow), cm], axis=0)
    nrep = 3 * M_HEADS * pair_w
    e_sel = (_iota2((16, nrep), 0) == jnp.right_shift(_iota2((16, nrep), 1), 7)).astype(F32)
    rep = _mm_tn(jnp.concatenate(_split3(x16), axis=0), jnp.concatenate([e_sel] * 3, axis=0))
    rep_of = lambda qn, h, cs: rep[cs, (qn * M_HEADS + h) * pair_w:(qn * M_HEADS + h + 1) * pair_w]

    causal = _iota2((ch, ch), 0) >= _iota2((ch, ch), 1)
    low_half = _iota2((ch, pair_w), 1) < hd

    probs = [(c, h) for c in range(nch) for h in range(M_HEADS)]
    q_p, k_p, vext_p = {}, {}, {}
    for c, h in probs:
        cs = slice(c * ch, (c + 1) * ch)
        ps = slice((h // 2) * pair_w, (h // 2 + 1) * pair_w)
        mine = low_half if h % 2 == 0 else jnp.logical_not(low_half)
        q_p[c, h] = jnp.where(mine, pm_ref[cs, ps], 0.0)
        k_p[c, h] = jnp.where(mine, pm_ref[cs, M_WIDTH + ps.start:M_WIDTH + ps.stop] * (hd ** -0.5), 0.0)
        vext_p[c, h] = jnp.where(mine, pm_ref[cs, 2 * M_WIDTH + ps.start:2 * M_WIDTH + ps.stop], 1.0)
    qk = {p: _mm_nt(q_p[p], k_p[p]) for p in probs}

    dw_p, sint_p, emt_p, wa_p, decay_p = {}, {}, {}, {}, {}
    for h in range(M_HEADS):
        m_prev = m_ref[h][0:1, :]
        for c in range(nch):
            cs = slice(c * ch, (c + 1) * ch)
            li_c, b_c, cm_c = rep_of(0, h, cs), rep_of(1, h, cs), rep_of(2, h, cs)
            b_r = brow[M_HEADS + h:M_HEADS + h + 1, cs]
            d_mat = jnp.where(causal, b_c[:, :ch] - b_r + grow[h:h + 1, cs], NEG)
            inter = b_c + m_prev
            m_t = jnp.maximum(b_c + cm_c, inter)
            dw_p[c, h] = jnp.where(causal, jnp.exp(d_mat - m_t[:, :ch]), 0.0)
            sint_p[c, h] = jnp.exp(inter - m_t)
            emt_p[c, h] = jnp.exp(-m_t)
            g = b_c[ch - 1:ch, :]
            m_new = jnp.maximum(g + m_prev, g + cm_c[ch - 1:ch, :])
            wa_p[c, h] = jnp.exp(g - b_c + li_c - m_new)
            decay_p[c, h] = jnp.exp(g + m_prev - m_new)
            m_prev = m_new
        m_ref[h] = jnp.broadcast_to(m_prev, m_ref.shape[1:])

    upd = {p: _mm_tn(vext_p[p], wa_p[p] * k_p[p]) for p in probs}
    cext_p = {}
    for h in range(M_HEADS):
        cext = cst_ref[h]
        for c in range(nch):
            cext_p[c, h] = cext
            cext = decay_p[c, h] * cext + upd[c, h]
        cst_ref[h] = cext
    inter_p = {p: _mm_nt(q_p[p], cext_p[p]) for p in probs}
    intra_p = {p: _mm(qk[p] * dw_p[p], vext_p[p]) for p in probs}

    rows = []
    for c in range(nch):
        pairs = []
        for j in range(M_HEADS // 2):
            halves = []
            for h in (2 * j, 2 * j + 1):
                tot = intra_p[c, h] + sint_p[c, h] * inter_p[c, h]
                den = pltpu.roll(tot, hd, axis=1)
                halves.append(tot / jnp.maximum(jnp.abs(den), emt_p[c, h]))
            pairs.append(jnp.where(low_half, halves[0], halves[1]))
        rows.append(jnp.concatenate(pairs, axis=1))
    hh = jnp.concatenate(rows, axis=0)

    head_ones = _chunk_mask(M_WIDTH, hd, False).astype(BF16)
    mu = _mm(hh, head_ones) * (1.0 / hd)
    hc = hh - mu
    var = _mm(hc * hc, head_ones) * (1.0 / hd)
    o_pre = pm_ref[:, 3 * M_WIDTH:4 * M_WIDTH]
    out_ref[...] = hc * lax.rsqrt(var + M_NORM_EPS) * nw_ref[...] * jax.nn.sigmoid(o_pre)


def _mlstm(pm, gate_bias, norm_w):
    b, lp, _ = pm.shape
    tb = _mix_rows(lp)
    bias = jnp.zeros((1, LANES), F32).at[0, :2 * M_HEADS].set(gate_bias.reshape(-1))
    return pl.pallas_call(
        _mlstm_kernel,
        grid=(b, lp // tb),
        in_specs=[pl.BlockSpec((None, tb, PM_W), lambda i, j: (i, j, 0)),
                  pl.BlockSpec((1, LANES), lambda i, j: (0, 0)),
                  pl.BlockSpec((1, M_WIDTH), lambda i, j: (0, 0))],
        out_specs=pl.BlockSpec((None, tb, M_WIDTH), lambda i, j: (i, j, 0)),
        out_shape=jax.ShapeDtypeStruct((b, lp, M_WIDTH), F32),
        scratch_shapes=[pltpu.VMEM((M_HEADS, 2 * HEAD_DIM, 2 * HEAD_DIM), F32),
                        pltpu.VMEM((M_HEADS, 8, LANES), F32)],
        compiler_params=_cparams(("arbitrary", "arbitrary")),
        name="mlstm",
    )(pm, bias, norm_w.reshape(1, M_WIDTH))


def _rwkv_kernel(pr_ref, prev_ref, mu_ref, w0_ref, w2_ref, a0_ref, a2_ref, g2_ref,
                 kk_ref, ka_ref, rk_ref, lnx_ref, out_ref, st_ref):
    blk = pl.program_id(1)
    tb = pr_ref.shape[0]
    ch = R_CHUNK
    rw = R_WIDTH
    hd = HEAD_DIM

    @pl.when(blk == 0)
    def _():
        st_ref[...] = jnp.zeros_like(st_ref)

    x = pr_ref[...]
    last = jnp.where(blk == 0, 0.0, prev_ref[7:8, :])
    prev = jnp.where(_iota2((tb, 1), 0) == 0, last, pltpu.roll(x, 1, axis=0))
    t = x + (prev - x) * mu_ref[...]
    r = t[:, 0:rw]
    k = t[:, rw:2 * rw]
    v = t[:, 2 * rw:3 * rw]
    o = 3 * rw
    w_lat = t[:, o:o + R_DECAY_RANK]
    a_lat = t[:, o + R_DECAY_RANK:o + R_DECAY_RANK + R_A_RANK]
    g_lat = t[:, o + R_DECAY_RANK + R_A_RANK:]

    w_log = -_softplus(-(w0_ref[...] + _mm(jnp.tanh(w_lat), w2_ref[...]))) - 0.5
    lw = -jnp.exp(w_log)
    a = jax.nn.sigmoid(a0_ref[...] + _mm(a_lat, a2_ref[...]))
    g = _mm(jax.nn.sigmoid(g_lat), g2_ref[...])

    head_ones = _chunk_mask(rw, hd, False).astype(BF16)
    kk = k * kk_ref[...]
    kk = kk / jnp.maximum(jnp.sqrt(_mm(kk * kk, head_ones)), 1e-12)
    k2 = k * (1.0 + (a - 1.0) * ka_ref[...])
    avec = -kk
    bvec = kk * a
    cum = _cumsum_rows(_chunk_mask(tb, ch, True).astype(F32), lw)

    r_i = _iota2((ch, ch), 0)
    c_i = _iota2((ch, ch), 1)
    incl = r_i >= c_i
    strict = r_i > c_i
    eye = r_i == c_i
    zeros_h = jnp.zeros((ch, hd), F32)
    n_levels = ch.bit_length() - 1

    nch = tb // ch
    probs = [(c, h) for c in range(nch) for h in range(R_HEADS)]
    ah_p, rh_p, vh_p, kt_p, bt_p, kp_p, bp_p, wc_p = ({} for _ in range(8))
    for c in range(nch):
        cs = slice(c * ch, (c + 1) * ch)
        cum_c = cum[cs]
        cl = cum_c[ch - 1:ch, :]
        inv = jnp.exp(-cum_c)
        rel = jnp.exp(cl - cum_c)
        ah_all = avec[cs] * jnp.exp(cum_c - lw[cs])
        rh_all = r[cs] * jnp.exp(cum_c)
        kt_all = k2[cs] * inv
        bt_all = bvec[cs] * inv
        kp_all = k2[cs] * rel
        bp_all = bvec[cs] * rel
        wc_all = jnp.exp(cl)
        v_c = v[cs]
        for h in range(R_HEADS):
            sl = slice(h * hd, (h + 1) * hd)
            key = (c, h)
            ah_p[key], rh_p[key], vh_p[key] = ah_all[:, sl], rh_all[:, sl], v_c[:, sl]
            kt_p[key], bt_p[key] = kt_all[:, sl], bt_all[:, sl]
            kp_p[key], bp_p[key], wc_p[key] = kp_all[:, sl], bp_all[:, sl], wc_all[:, sl]

    gram = {p: _mm_nt(jnp.concatenate([ah_p[p], rh_p[p]], axis=0),
                      jnp.concatenate([kt_p[p], bt_p[p]], axis=0)) for p in probs}
    m_ak = {p: jnp.where(strict, gram[p][:ch, :ch], 0.0) for p in probs}
    pw = {p: jnp.where(strict, gram[p][:ch, ch:], 0.0) for p in probs}
    n_cat = {p: jnp.concatenate([jnp.where(incl, gram[p][ch:, :ch], 0.0),
                                 jnp.where(incl, gram[p][ch:, ch:], 0.0)], axis=1) for p in probs}
    z = {p: jnp.concatenate([ah_p[p], _mm(m_ak[p], vh_p[p])], axis=1) for p in probs}
    for lvl in range(n_levels):
        if lvl + 1 < n_levels:
            res = {p: _mm(pw[p], jnp.concatenate([z[p], pw[p]], axis=1)) for p in probs}
            z = {p: z[p] + res[p][:, :2 * hd] for p in probs}
            pw = {p: res[p][:, 2 * hd:] for p in probs}
        else:
            z = {p: z[p] + _mm(pw[p], z[p]) for p in probs}
    rhs = {p: jnp.concatenate([jnp.concatenate([zeros_h, vh_p[p]], axis=1), z[p]], axis=0)
           for p in probs}
    ry = {p: _mm(n_cat[p], rhs[p]) for p in probs}
    tg = {p: _mm_tn(jnp.concatenate([kp_p[p], bp_p[p]], axis=0), rhs[p]) for p in probs}
    lhs = {p: jnp.concatenate([rh_p[p] + ry[p][:, :hd],
                               jnp.where(eye, wc_p[p], 0.0) + tg[p][:, :hd]], axis=0)
           for p in probs}

    sts = [st_ref[h] for h in range(R_HEADS)]
    y_chunks = []
    for c in range(nch):
        nxt = [_mm(lhs[(c, h)], sts[h]) for h in range(R_HEADS)]
        y_chunks.append(jnp.concatenate(
            [nxt[h][:ch] + ry[(c, h)][:, hd:] for h in range(R_HEADS)], axis=1))
        sts = [nxt[h][ch:] + tg[(c, h)][:, hd:] for h in range(R_HEADS)]
    for h in range(R_HEADS):
        st_ref[h] = sts[h]
    y = jnp.concatenate(y_chunks, axis=0)

    inv_d = 1.0 / hd
    mu = _mm(y, head_ones) * inv_d
    yc = y - mu
    var = _mm(yc * yc, head_ones) * inv_d
    yn = yc * lax.rsqrt(var + R_GN_EPS) * lnx_ref[0:1, :] + lnx_ref[1:2, :]
    bonus = _mm(r * k2 * rk_ref[...], head_ones) * v
    out_ref[...] = (yn + bonus) * g


def _rwkv(pr, mu, w0, w2, a0, a2, g2, k_k, k_a, r_k, ln_x):
    b, lp, _ = pr.shape
    tb = _mix_rows(lp)
    per8 = tb // 8
    row = lambda a: a.reshape(1, -1)
    full = lambda a: pl.BlockSpec(a.shape, lambda i, j: (0,) * a.ndim)
    params = [row(mu), row(w0), w2, row(a0), a2, g2, row(k_k), row(k_a), row(r_k), ln_x]
    return pl.pallas_call(
        _rwkv_kernel,
        grid=(b, lp // tb),
        in_specs=[pl.BlockSpec((None, tb, PR_W), lambda i, j: (i, j, 0)),
                  pl.BlockSpec((None, 8, PR_W), lambda i, j: (i, jnp.maximum(j * per8 - 1, 0), 0))]
                 + [full(a) for a in params],
        out_specs=pl.BlockSpec((None, tb, R_WIDTH), lambda i, j: (i, j, 0)),
        out_shape=jax.ShapeDtypeStruct((b, lp, R_WIDTH), F32),
        scratch_shapes=[pltpu.VMEM((R_HEADS, HEAD_DIM, HEAD_DIM), F32)],
        compiler_params=_cparams(("arbitrary", "arbitrary")),
        name="rwkv7",
    )(pr, pr, *params)


def _t5_bucket(dist):
    max_exact = N_BUCKETS // 2
    d = jnp.maximum(dist, 1).astype(F32)
    large = max_exact + (jnp.log(d / max_exact) / math.log(MAX_DISTANCE / max_exact)
                         * (N_BUCKETS - max_exact)).astype(jnp.int32)
    large = jnp.minimum(large, N_BUCKETS - 1)
    return jnp.where(dist < max_exact, dist, large)


def _attn_tables(rel_bias):
    blk = A_BLOCK
    i = jnp.arange(blk)[:, None]
    j = jnp.arange(blk)[None, :]
    rb = rel_bias.astype(F32)

    def bias(dist):
        onehot = jax.nn.one_hot(_t5_bucket(jnp.maximum(dist, 0)), N_BUCKETS, dtype=F32)
        return jnp.einsum("hb,ijb->hij", rb, onehot, precision=lax.Precision.HIGHEST)

    neg = jnp.full((A_HEADS, blk, blk), NEG, F32)
    is_meta = j >= FRONT_PAD
    cur_d = i - j
    prev_d = blk + i - j
    cur = jnp.where(cur_d >= 0, bias(cur_d), NEG)
    prev = jnp.where(prev_d < WINDOW, bias(prev_d), NEG)
    meta0 = jnp.where(is_meta & (cur_d >= 0), bias(cur_d), NEG)
    meta1 = jnp.where(is_meta, bias(prev_d), NEG)
    meta2 = jnp.where(is_meta, bias(2 * blk + i - j), NEG)
    case0 = jnp.concatenate([meta0, neg, neg], axis=-1)
    case1 = jnp.concatenate([meta1, neg, cur], axis=-1)
    case2 = jnp.concatenate([meta2, prev, cur], axis=-1)
    return jnp.stack([case0, case1, case2], axis=0)


def _swa_kernel(q_ref, kc_ref, kp_ref, km_ref, vc_ref, vp_ref, vm_ref, tab_ref, sink_ref, out_ref):
    scale = HEAD_DIM ** -0.5
    blk = A_BLOCK
    nb = q_ref.shape[0] // blk
    step = pl.program_id(1)
    kvs = range(A_KV_HEADS)
    heads = [range(kv * A_GROUP, (kv + 1) * A_GROUP) for kv in kvs]
    ksl = [slice(kv * HEAD_DIM, (kv + 1) * HEAD_DIM) for kv in kvs]

    def keys(cur_ref, prev_ref, meta_ref, i, kv):
        prev = prev_ref[:, ksl[kv]] if i == 0 else cur_ref[(i - 1) * blk:i * blk, ksl[kv]]
        return jnp.concatenate([meta_ref[:, ksl[kv]], prev, cur_ref[i * blk:(i + 1) * blk, ksl[kv]]],
                               axis=0)

    probs = [(i, kv) for i in range(nb) for kv in kvs]
    s = {}
    for i, kv in probs:
        q = jnp.concatenate([q_ref[i * blk:(i + 1) * blk, h * HEAD_DIM:(h + 1) * HEAD_DIM]
                             for h in heads[kv]], axis=0)
        s[i, kv] = _mm_nt(q * scale, keys(kc_ref, kp_ref, km_ref, i, kv))
    p, denom = {}, {}
    for i, kv in probs:
        case = 2 if i >= 2 else jnp.minimum(step * nb + i, 2)
        tab = jnp.concatenate([tab_ref[case, h] for h in heads[kv]], axis=0)
        sink = jnp.concatenate([jnp.broadcast_to(sink_ref[h:h + 1, 0:1], (blk, 1)) for h in heads[kv]],
                               axis=0)
        sk = s[i, kv] + tab
        m = jnp.maximum(jnp.max(sk, axis=-1, keepdims=True), sink)
        pk = jnp.exp(sk - m)
        p[i, kv] = pk
        denom[i, kv] = jnp.sum(pk, axis=-1, keepdims=True) + jnp.exp(sink - m)
    o = {pr: _mm(p[pr], keys(vc_ref, vp_ref, vm_ref, *pr)) / denom[pr] for pr in probs}
    for i in range(nb):
        out_ref[i * blk:(i + 1) * blk, :] = jnp.concatenate(
            [o[i, kv][g * blk:(g + 1) * blk] for kv in kvs for g in range(A_GROUP)], axis=1)


def _swa(pa, sinks, tables):
    b, lp, _ = pa.shape
    blk = A_BLOCK
    kcol = A_WIDTH // A_KV_WIDTH
    sink = jnp.broadcast_to(sinks.astype(F32)[:, None], (A_HEADS, LANES))
    rows = _mix_rows(lp)
    nb = rows // blk
    cur_spec = lambda col: pl.BlockSpec((None, rows, A_KV_WIDTH), lambda i, j: (i, j, col))
    prev_spec = lambda col: pl.BlockSpec((None, blk, A_KV_WIDTH),
                                         lambda i, j: (i, jnp.maximum(j * nb - 1, 0), col))
    meta_spec = lambda col: pl.BlockSpec((None, blk, A_KV_WIDTH), lambda i, j: (i, 0, col))
    return pl.pallas_call(
        _swa_kernel,
        grid=(b, lp // rows),
        in_specs=[pl.BlockSpec((None, rows, A_WIDTH), lambda i, j: (i, j, 0)),
                  cur_spec(kcol), prev_spec(kcol), meta_spec(kcol),
                  cur_spec(kcol + 1), prev_spec(kcol + 1), meta_spec(kcol + 1),
                  pl.BlockSpec(tables.shape, lambda i, j: (0, 0, 0, 0), pipeline_mode=pl.Buffered(1)),
                  pl.BlockSpec((A_HEADS, LANES), lambda i, j: (0, 0))],
        out_specs=pl.BlockSpec((None, rows, A_WIDTH), lambda i, j: (i, j, 0)),
        out_shape=jax.ShapeDtypeStruct((b, lp, A_WIDTH), F32),
        compiler_params=_cparams(("arbitrary", "arbitrary")),
        name="swa",
    )(pa, pa, pa, pa, pa, pa, pa, tables, sink)


def _layer_norm_rows(z, ln_ref):
    mu = jnp.mean(z, axis=-1, keepdims=True)
    zc = z - mu
    var = jnp.mean(zc * zc, axis=-1, keepdims=True)
    return zc * lax.rsqrt(var + LN_EPS) * ln_ref[0:1, :] + ln_ref[1:2, :]


def _mix_kernel(ym_ref, yr_ref, ya_ref, h_ref, w_ref, ln_ref, out_ref, *, alpha):
    tm = h_ref.shape[0]
    mix = jnp.dot(ym_ref[...].astype(BF16), w_ref[0:M_WIDTH, :], preferred_element_type=F32)
    mix += jnp.dot(yr_ref[...].astype(BF16), w_ref[M_WIDTH:M_WIDTH + R_WIDTH, :],
                   preferred_element_type=F32)
    mix += jnp.dot(ya_ref[...].astype(BF16), w_ref[M_WIDTH + R_WIDTH:, :],
                   preferred_element_type=F32)
    y = _layer_norm_rows(alpha * h_ref[...] + mix, ln_ref)
    row = pl.program_id(1) * tm + _iota2((tm, 1), 0)
    out_ref[...] = jnp.where(row < FRONT_PAD, 0.0, y)


def _mix(ym, yr, ya, h, w, ln, alpha):
    b, lp, d = h.shape
    tm = _row_tile(lp)
    spec = lambda width: pl.BlockSpec((None, tm, width), lambda i, j: (i, j, 0))
    return pl.pallas_call(
        functools.partial(_mix_kernel, alpha=alpha),
        grid=(b, lp // tm),
        in_specs=[spec(M_WIDTH), spec(R_WIDTH), spec(A_WIDTH), spec(d),
                  pl.BlockSpec(w.shape, lambda i, j: (0, 0), pipeline_mode=pl.Buffered(1)),
                  pl.BlockSpec((2, d), lambda i, j: (0, 0))],
        out_specs=spec(d),
        out_shape=jax.ShapeDtypeStruct((b, lp, d), F32),
        compiler_params=_cparams(("arbitrary", "arbitrary")),
        name="out_proj_ln",
    )(ym, yr, ya, h, w, ln)


FF_CHUNK = 256


def _ffn_kernel(h_ref, wi_ref, wo_ref, ln_ref, out_ref, *, alpha, d_ff):
    tm = h_ref.shape[0]
    x = h_ref[...]
    xb = x.astype(BF16)
    acc = alpha * x
    for j in range(d_ff // FF_CHUNK):
        cs = slice(j * FF_CHUNK, (j + 1) * FF_CHUNK)
        us = slice(d_ff + j * FF_CHUNK, d_ff + (j + 1) * FF_CHUNK)
        gate = jnp.dot(xb, wi_ref[:, cs], preferred_element_type=F32)
        up = jnp.dot(xb, wi_ref[:, us], preferred_element_type=F32)
        act = (gate * jax.nn.sigmoid(gate) * up).astype(BF16)
        acc += jnp.dot(act, wo_ref[cs, :], preferred_element_type=F32)
    y = _layer_norm_rows(acc, ln_ref)
    row = pl.program_id(1) * tm + _iota2((tm, 1), 0)
    out_ref[...] = jnp.where(row < FRONT_PAD, 0.0, y)


def _ffn(h, wi, wo, ln, alpha):
    b, lp, d = h.shape
    d_ff = wo.shape[0]
    assert d_ff % FF_CHUNK == 0
    tm = _row_tile(lp)
    spec = pl.BlockSpec((None, tm, d), lambda i, j: (i, j, 0))
    const = lambda a: pl.BlockSpec(a.shape, lambda i, j: (0, 0), pipeline_mode=pl.Buffered(1))
    return pl.pallas_call(
        functools.partial(_ffn_kernel, alpha=alpha, d_ff=d_ff),
        grid=(b, lp // tm),
        in_specs=[spec, const(wi), const(wo), pl.BlockSpec((2, d), lambda i, j: (0, 0))],
        out_specs=spec,
        out_shape=jax.ShapeDtypeStruct((b, lp, d), F32),
        compiler_params=_cparams(("arbitrary", "arbitrary")),
        name="ffn_ln",
    )(h, wi, wo, ln)


def kernel(x, meta_tokens, rel_bias, w_in, m_gate_bias, m_norm_w, r_mu_rkv, r_mu_w, r_mu_a, r_mu_g, r_w0, r_w2, r_a0, r_a2, r_g2, r_k_k, r_k_a, r_r_k, r_ln_x, a_sinks, w_out, ln_mix, w_ff_in, w_ff_out, ln_ffn):
    b, seq, d = x.shape
    depth = w_in.shape[0]
    alpha = (2 * depth) ** 0.25
    assert (TOKEN_START + seq) % A_BLOCK == 0

    meta = jnp.broadcast_to(meta_tokens[None].astype(x.dtype), (b, N_META, d))
    h = jnp.concatenate([jnp.zeros((b, FRONT_PAD, d), x.dtype), meta, x], axis=1)
    tables = _attn_tables(rel_bias)
    a_end = PA_W
    m_end = a_end + M_RAW

    for l in range(depth):
        wl = w_in[l]
        w_pad = jnp.concatenate(
            [wl[:, :m_end], jnp.zeros((d, PM_W - M_RAW), wl.dtype), wl[:, m_end:]], axis=1).astype(BF16)
        pa, pm, pr = _proj(h, w_pad)
        y_m = _mlstm(pm, m_gate_bias[l], m_norm_w[l])
        mu = jnp.concatenate([r_mu_rkv[l].reshape(-1), r_mu_w[l], r_mu_a[l], r_mu_g[l]])
        y_r = _rwkv(pr, mu, r_w0[l], r_w2[l], r_a0[l], r_a2[l], r_g2[l], r_k_k[l], r_k_a[l],
                    r_r_k[l], r_ln_x[l])
        y_a = _swa(pa, a_sinks[l], tables)
        h = _mix(y_m, y_r, y_a, h, w_out[l].astype(BF16), ln_mix[l], alpha)
        h = _ffn(h, w_ff_in[l].astype(BF16), w_ff_out[l].astype(BF16), ln_ffn[l], alpha)
    return h[:, TOKEN_START:]
```

```python
import functools
import math

import jax
import jax.numpy as jnp
from jax import lax
from jax.experimental import pallas as pl
from jax.experimental.pallas import tpu as pltpu

F32 = jnp.float32
BF16 = jnp.bfloat16

HEAD_DIM = 64
N_META = 16
M_HEADS = 4
M_WIDTH = M_HEADS * HEAD_DIM
M_CHUNK = 64
M_NORM_EPS = 1e-6
R_HEADS = 4
R_WIDTH = R_HEADS * HEAD_DIM
R_CHUNK = 64
R_DECAY_RANK = 32
R_A_RANK = 32
R_GATE_RANK = 64
R_GN_EPS = 64e-5
A_HEADS = 8
A_KV_HEADS = 2
A_GROUP = A_HEADS // A_KV_HEADS
A_WIDTH = A_HEADS * HEAD_DIM
A_KV_WIDTH = A_KV_HEADS * HEAD_DIM
WINDOW = 128
A_BLOCK = 128
N_BUCKETS = 32
MAX_DISTANCE = 128
LN_EPS = 1e-5
NEG = -1e30

LANES = 128
FRONT_PAD = A_BLOCK - N_META
TOKEN_START = FRONT_PAD + N_META

PA_W = A_WIDTH + 2 * A_KV_WIDTH
PM_W = 4 * M_WIDTH + LANES
PR_W = 3 * R_WIDTH + R_DECAY_RANK + R_A_RANK + R_GATE_RANK
M_RAW = 4 * M_WIDTH + 2 * M_HEADS

VMEM_LIMIT = 56 * 1024 * 1024
MIX_ROWS = 640


def _row_tile(n):
    for t in (640, 512, 256, 128, 64):
        if n % t == 0:
            return t
    raise ValueError(f"row count {n} has no supported tile")


def _mix_rows(lp):
    return MIX_ROWS if lp % MIX_ROWS == 0 else A_BLOCK


def _cparams(sem):
    return pltpu.CompilerParams(dimension_semantics=sem, vmem_limit_bytes=VMEM_LIMIT)


def _bf(x):
    return x.astype(BF16)


def _mm(a, b):
    return lax.dot_general(_bf(a), _bf(b), (((1,), (0,)), ((), ())), preferred_element_type=F32)


def _mm_nt(a, b):
    return lax.dot_general(_bf(a), _bf(b), (((1,), (1,)), ((), ())), preferred_element_type=F32)


def _mm_tn(a, b):
    return lax.dot_general(_bf(a), _bf(b), (((0,), (0,)), ((), ())), preferred_element_type=F32)


def _iota2(shape, dim):
    return lax.broadcasted_iota(jnp.int32, shape, dim)


def _split3(x):
    hi = _bf(x).astype(F32)
    r1 = x - hi
    mid = _bf(r1).astype(F32)
    return hi, mid, r1 - mid


def _chunk_mask(n, chunk, lower):
    r_i = _iota2((n, n), 0)
    c_i = _iota2((n, n), 1)
    sh = chunk.bit_length() - 1
    same = jnp.right_shift(r_i, sh) == jnp.right_shift(c_i, sh)
    return same & (r_i >= c_i) if lower else same


def _cumsum_rows(tri, x):
    w = x.shape[1]
    res = _mm(tri, jnp.concatenate(_split3(x), axis=1))
    return res[:, :w] + res[:, w:2 * w] + res[:, 2 * w:]


def _cumsum_lanes(tri, x):
    r = x.shape[0]
    res = _mm_nt(jnp.concatenate(_split3(x), axis=0), tri)
    return res[:r] + res[r:2 * r] + res[2 * r:]


def _softplus(x):
    return jnp.maximum(x, 0.0) + jnp.log1p(jnp.exp(-jnp.abs(x)))


def _proj_kernel(x_ref, w_ref, oa_ref, om_ref, or_ref):
    acc = jnp.dot(x_ref[...].astype(BF16), w_ref[...], preferred_element_type=F32)
    oa_ref[...] = acc[:, :PA_W]
    om_ref[...] = acc[:, PA_W:PA_W + PM_W]
    or_ref[...] = acc[:, PA_W + PM_W:]


def _proj(h, w):
    b, lp, d = h.shape
    tm = _row_tile(lp)
    n = w.shape[1]
    spec = lambda width: pl.BlockSpec((None, tm, width), lambda i, j: (i, j, 0))
    return pl.pallas_call(
        _proj_kernel,
        grid=(b, lp // tm),
        in_specs=[spec(d),
                  pl.BlockSpec((d, n), lambda i, j: (0, 0), pipeline_mode=pl.Buffered(1))],
        out_specs=[spec(PA_W), spec(PM_W), spec(PR_W)],
        out_shape=[jax.ShapeDtypeStruct((b, lp, PA_W), F32),
                   jax.ShapeDtypeStruct((b, lp, PM_W), F32),
                   jax.ShapeDtypeStruct((b, lp, PR_W), F32)],
        compiler_params=_cparams(("arbitrary", "arbitrary")),
        name="in_proj",
    )(h, w)


def _mlstm_kernel(pm_ref, bias_ref, nw_ref, out_ref, cst_ref, m_ref):
    blk = pl.program_id(1)
    tb = pm_ref.shape[0]
    ch = M_CHUNK
    hd = HEAD_DIM
    nch = tb // ch
    pair_w = 2 * hd

    @pl.when(blk == 0)
    def _():
        cst_ref[...] = jnp.zeros_like(cst_ref)
        m_ref[...] = jnp.full_like(m_ref, NEG)

    row = blk * tb + _iota2((tb, 1), 0)
    is_pad = row < FRONT_PAD
    gb = pm_ref[:, 4 * M_WIDTH:] + bias_ref[...]
    lane = _iota2((tb, LANES), 1)
    li = jnp.where(is_pad, NEG, gb)
    lf = jnp.where(is_pad, 0.0, -_softplus(-gb))
    gcol = jnp.where(lane < M_HEADS, li, jnp.where(lane < 2 * M_HEADS, lf, 0.0))
    sel = (_iota2((8, LANES), 0) == _iota2((8, LANES), 1)).astype(F32)
    grow = _mm_nt(jnp.concatenate([sel] * 3, axis=1), jnp.concatenate(_split3(gcol), axis=1))
    brow = _cumsum_lanes(_chunk_mask(tb, ch, True).astype(F32), grow)
    r8 = _iota2((8, tb), 0)
    t8 = jnp.bitwise_and(_iota2((8, tb), 1), ch - 1)
    cm = grow - pltpu.roll(brow, M_HEADS, axis=0)
    sh = 1
    while sh < ch:
        cm = jnp.where(t8 >= sh, jnp.maximum(cm, pltpu.roll(cm, sh, axis=1)), cm)
        sh *= 2
    x16 = jnp.concatenate([jnp.where(r8 < M_HEADS, grow, brow), cm], axis=0)
    nrep = 3 * M_HEADS * pair_w
    e_sel = (_iota2((16, nrep), 0) == jnp.right_shift(_iota2((16, nrep), 1), 7)).astype(F32)
    rep = _mm_tn(jnp.concatenate(_split3(x16), axis=0), jnp.concatenate([e_sel] * 3, axis=0))
    rep_of = lambda qn, h, cs: rep[cs, (qn * M_HEADS + h) * pair_w:(qn * M_HEADS + h + 1) * pair_w]

    causal = _iota2((ch, ch), 0) >= _iota2((ch, ch), 1)
    low_half = _iota2((ch, pair_w), 1) < hd

    probs = [(c, h) for c in range(nch) for h in range(M_HEADS)]
    q_p, k_p, vext_p = {}, {}, {}
    for c, h in probs:
        cs = slice(c * ch, (c + 1) * ch)
        ps = slice((h // 2) * pair_w, (h // 2 + 1) * pair_w)
        mine = low_half if h % 2 == 0 else jnp.logical_not(low_half)
        q_p[c, h] = jnp.where(mine, pm_ref[cs, ps], 0.0)
        k_p[c, h] = jnp.where(mine, pm_ref[cs, M_WIDTH + ps.start:M_WIDTH + ps.stop] * (hd ** -0.5), 0.0)
        vext_p[c, h] = jnp.where(mine, pm_ref[cs, 2 * M_WIDTH + ps.start:2 * M_WIDTH + ps.stop], 1.0)
    qk = {p: _mm_nt(q_p[p], k_p[p]) for p in probs}

    dw_p, sint_p, emt_p, wa_p, decay_p = {}, {}, {}, {}, {}
    for h in range(M_HEADS):
        m_prev = m_ref[h][0:1, :]
        for c in range(nch):
            cs = slice(c * ch, (c + 1) * ch)
            li_c, b_c, cm_c = rep_of(0, h, cs), rep_of(1, h, cs), rep_of(2, h, cs)
            b_r = brow[M_HEADS + h:M_HEADS + h + 1, cs]
            d_mat = jnp.where(causal, b_c[:, :ch] - b_r + grow[h:h + 1, cs], NEG)
            inter = b_c + m_prev
            m_t = jnp.maximum(b_c + cm_c, inter)
            dw_p[c, h] = jnp.where(causal, jnp.exp(d_mat - m_t[:, :ch]), 0.0)
            sint_p[c, h] = jnp.exp(inter - m_t)
            emt_p[c, h] = jnp.exp(-m_t)
            g = b_c[ch - 1:ch, :]
            m_new = jnp.maximum(g + m_prev, g + cm_c[ch - 1:ch, :])
            wa_p[c, h] = jnp.exp(g - b_c + li_c - m_new)
            decay_p[c, h] = jnp.exp(g + m_prev - m_new)
            m_prev = m_new
        m_ref[h] = jnp.broadcast_to(m_prev, m_ref.shape[1:])

    upd = {p: _mm_tn(vext_p[p], wa_p[p] * k_p[p]) for p in probs}
    cext_p = {}
    for h in range(M_HEADS):
        cext = cst_ref[h]
        for c in range(nch):
            cext_p[c, h] = cext
            cext = decay_p[c, h] * cext + upd[c, h]
        cst_ref[h] = cext
    inter_p = {p: _mm_nt(q_p[p], cext_p[p]) for p in probs}
    intra_p = {p: _mm(qk[p] * dw_p[p], vext_p[p]) for p in probs}

    rows = []
    for c in range(nch):
        pairs = []
        for j in range(M_HEADS // 2):
            halves = []
            for h in (2 * j, 2 * j + 1):
                tot = intra_p[c, h] + sint_p[c, h] * inter_p[c, h]
                den = pltpu.roll(tot, hd, axis=1)
                halves.append(tot / jnp.maximum(jnp.abs(den), emt_p[c, h]))
            pairs.append(jnp.where(low_half, halves[0], halves[1]))
        rows.append(jnp.concatenate(pairs, axis=1))
    hh = jnp.concatenate(rows, axis=0)

    head_ones = _chunk_mask(M_WIDTH, hd, False).astype(BF16)
    mu = _mm(hh, head_ones) * (1.0 / hd)
    hc = hh - mu
    var = _mm(hc * hc, head_ones) * (1.0 / hd)
    o_pre = pm_ref[:, 3 * M_WIDTH:4 * M_WIDTH]
    out_ref[...] = hc * lax.rsqrt(var + M_NORM_EPS) * nw_ref[...] * jax.nn.sigmoid(o_pre)


def _mlstm(pm, gate_bias, norm_w):
    b, lp, _ = pm.shape
    tb = _mix_rows(lp)
    bias = jnp.zeros((1, LANES), F32).at[0, :2 * M_HEADS].set(gate_bias.reshape(-1))
    return pl.pallas_call(
        _mlstm_kernel,
        grid=(b, lp // tb),
        in_specs=[pl.BlockSpec((None, tb, PM_W), lambda i, j: (i, j, 0)),
                  pl.BlockSpec((1, LANES), lambda i, j: (0, 0)),
                  pl.BlockSpec((1, M_WIDTH), lambda i, j: (0, 0))],
        out_specs=pl.BlockSpec((None, tb, M_WIDTH), lambda i, j: (i, j, 0)),
        out_shape=jax.ShapeDtypeStruct((b, lp, M_WIDTH), F32),
        scratch_shapes=[pltpu.VMEM((M_HEADS, 2 * HEAD_DIM, 2 * HEAD_DIM), F32),
                        pltpu.VMEM((M_HEADS, 8, LANES), F32)],
        compiler_params=_cparams(("arbitrary", "arbitrary")),
        name="mlstm",
    )(pm, bias, norm_w.reshape(1, M_WIDTH))


def _rwkv_kernel(pr_ref, prev_ref, mu_ref, w0_ref, w2_ref, a0_ref, a2_ref, g2_ref,
                 kk_ref, ka_ref, rk_ref, lnx_ref, out_ref, st_ref):
    blk = pl.program_id(1)
    tb = pr_ref.shape[0]
    ch = R_CHUNK
    rw = R_WIDTH
    hd = HEAD_DIM

    @pl.when(blk == 0)
    def _():
        st_ref[...] = jnp.zeros_like(st_ref)

    x = pr_ref[...]
    last = jnp.where(blk == 0, 0.0, prev_ref[7:8, :])
    prev = jnp.where(_iota2((tb, 1), 0) == 0, last, pltpu.roll(x, 1, axis=0))
    t = x + (prev - x) * mu_ref[...]
    r = t[:, 0:rw]
    k = t[:, rw:2 * rw]
    v = t[:, 2 * rw:3 * rw]
    o = 3 * rw
    w_lat = t[:, o:o + R_DECAY_RANK]
    a_lat = t[:, o + R_DECAY_RANK:o + R_DECAY_RANK + R_A_RANK]
    g_lat = t[:, o + R_DECAY_RANK + R_A_RANK:]

    w_log = -_softplus(-(w0_ref[...] + _mm(jnp.tanh(w_lat), w2_ref[...]))) - 0.5
    lw = -jnp.exp(w_log)
    a = jax.nn.sigmoid(a0_ref[...] + _mm(a_lat, a2_ref[...]))
    g = _mm(jax.nn.sigmoid(g_lat), g2_ref[...])

    head_ones = _chunk_mask(rw, hd, False).astype(BF16)
    kk = k * kk_ref[...]
    kk = kk / jnp.maximum(jnp.sqrt(_mm(kk * kk, head_ones)), 1e-12)
    k2 = k * (1.0 + (a - 1.0) * ka_ref[...])
    avec = -kk
    bvec = kk * a
    cum = _cumsum_rows(_chunk_mask(tb, ch, True).astype(F32), lw)

    r_i = _iota2((ch, ch), 0)
    c_i = _iota2((ch, ch), 1)
    incl = r_i >= c_i
    strict = r_i > c_i
    eye = r_i == c_i
    zeros_h = jnp.zeros((ch, hd), F32)
    n_levels = ch.bit_length() - 1

    nch = tb // ch
    probs = [(c, h) for c in range(nch) for h in range(R_HEADS)]
    ah_p, rh_p, vh_p, kt_p, bt_p, kp_p, bp_p, wc_p = ({} for _ in range(8))
    for c in range(nch):
        cs = slice(c * ch, (c + 1) * ch)
        cum_c = cum[cs]
        cl = cum_c[ch - 1:ch, :]
        inv = jnp.exp(-cum_c)
        rel = jnp.exp(cl - cum_c)
        ah_all = avec[cs] * jnp.exp(cum_c - lw[cs])
        rh_all = r[cs] * jnp.exp(cum_c)
        kt_all = k2[cs] * inv
        bt_all = bvec[cs] * inv
        kp_all = k2[cs] * rel
        bp_all = bvec[cs] * rel
        wc_all = jnp.exp(cl)
        v_c = v[cs]
        for h in range(R_HEADS):
            sl = slice(h * hd, (h + 1) * hd)
            key = (c, h)
            ah_p[key], rh_p[key], vh_p[key] = ah_all[:, sl], rh_all[:, sl], v_c[:, sl]
            kt_p[key], bt_p[key] = kt_all[:, sl], bt_all[:, sl]
            kp_p[key], bp_p[key], wc_p[key] = kp_all[:, sl], bp_all[:, sl], wc_all[:, sl]

    gram = {p: _mm_nt(jnp.concatenate([ah_p[p], rh_p[p]], axis=0),
                      jnp.concatenate([kt_p[p], bt_p[p]], axis=0)) for p in probs}
    m_ak = {p: jnp.where(strict, gram[p][:ch, :ch], 0.0) for p in probs}
    pw = {p: jnp.where(strict, gram[p][:ch, ch:], 0.0) for p in probs}
    n_cat = {p: jnp.concatenate([jnp.where(incl, gram[p][ch:, :ch], 0.0),
                                 jnp.where(incl, gram[p][ch:, ch:], 0.0)], axis=1) for p in probs}
    z = {p: jnp.concatenate([ah_p[p], _mm(m_ak[p], vh_p[p])], axis=1) for p in probs}
    for lvl in range(n_levels):
        if lvl + 1 < n_levels:
            res = {p: _mm(pw[p], jnp.concatenate([z[p], pw[p]], axis=1)) for p in probs}
            z = {p: z[p] + res[p][:, :2 * hd] for p in probs}
            pw = {p: res[p][:, 2 * hd:] for p in probs}
        else:
            z = {p: z[p] + _mm(pw[p], z[p]) for p in probs}
    rhs = {p: jnp.concatenate([jnp.concatenate([zeros_h, vh_p[p]], axis=1), z[p]], axis=0)
           for p in probs}
    ry = {p: _mm(n_cat[p], rhs[p]) for p in probs}
    tg = {p: _mm_tn(jnp.concatenate([kp_p[p], bp_p[p]], axis=0), rhs[p]) for p in probs}
    lhs = {p: jnp.concatenate([rh_p[p] + ry[p][:, :hd],
                               jnp.where(eye, wc_p[p], 0.0) + tg[p][:, :hd]], axis=0)
           for p in probs}

    sts = [st_ref[h] for h in range(R_HEADS)]
    y_chunks = []
    for c in range(nch):
        nxt = [_mm(lhs[(c, h)], sts[h]) for h in range(R_HEADS)]
        y_chunks.append(jnp.concatenate(
            [nxt[h][:ch] + ry[(c, h)][:, hd:] for h in range(R_HEADS)], axis=1))
        sts = [nxt[h][ch:] + tg[(c, h)][:, hd:] for h in range(R_HEADS)]
    for h in range(R_HEADS):
        st_ref[h] = sts[h]
    y = jnp.concatenate(y_chunks, axis=0)

    inv_d = 1.0 / hd
    mu = _mm(y, head_ones) * inv_d
    yc = y - mu
    var = _mm(yc * yc, head_ones) * inv_d
    yn = yc * lax.rsqrt(var + R_GN_EPS) * lnx_ref[0:1, :] + lnx_ref[1:2, :]
    bonus = _mm(r * k2 * rk_ref[...], head_ones) * v
    out_ref[...] = (yn + bonus) * g


def _rwkv(pr, mu, w0, w2, a0, a2, g2, k_k, k_a, r_k, ln_x):
    b, lp, _ = pr.shape
    tb = _mix_rows(lp)
    per8 = tb // 8
    row = lambda a: a.reshape(1, -1)
    full = lambda a: pl.BlockSpec(a.shape, lambda i, j: (0,) * a.ndim)
    params = [row(mu), row(w0), w2, row(a0), a2, g2, row(k_k), row(k_a), row(r_k), ln_x]
    return pl.pallas_call(
        _rwkv_kernel,
        grid=(b, lp // tb),
        in_specs=[pl.BlockSpec((None, tb, PR_W), lambda i, j: (i, j, 0)),
                  pl.BlockSpec((None, 8, PR_W), lambda i, j: (i, jnp.maximum(j * per8 - 1, 0), 0))]
                 + [full(a) for a in params],
        out_specs=pl.BlockSpec((None, tb, R_WIDTH), lambda i, j: (i, j, 0)),
        out_shape=jax.ShapeDtypeStruct((b, lp, R_WIDTH), F32),
        scratch_shapes=[pltpu.VMEM((R_HEADS, HEAD_DIM, HEAD_DIM), F32)],
        compiler_params=_cparams(("arbitrary", "arbitrary")),
        name="rwkv7",
    )(pr, pr, *params)


def _t5_bucket(dist):
    max_exact = N_BUCKETS // 2
    d = jnp.maximum(dist, 1).astype(F32)
    large = max_exact + (jnp.log(d / max_exact) / math.log(MAX_DISTANCE / max_exact)
                         * (N_BUCKETS - max_exact)).astype(jnp.int32)
    large = jnp.minimum(large, N_BUCKETS - 1)
    return jnp.where(dist < max_exact, dist, large)


def _attn_tables(rel_bias):
    blk = A_BLOCK
    i = jnp.arange(blk)[:, None]
    j = jnp.arange(blk)[None, :]
    rb = rel_bias.astype(F32)

    def bias(dist):
        onehot = jax.nn.one_hot(_t5_bucket(jnp.maximum(dist, 0)), N_BUCKETS, dtype=F32)
        return jnp.einsum("hb,ijb->hij", rb, onehot, precision=lax.Precision.HIGHEST)

    neg = jnp.full((A_HEADS, blk, blk), NEG, F32)
    is_meta = j >= FRONT_PAD
    cur_d = i - j
    prev_d = blk + i - j
    cur = jnp.where(cur_d >= 0, bias(cur_d), NEG)
    prev = jnp.where(prev_d < WINDOW, bias(prev_d), NEG)
    meta0 = jnp.where(is_meta & (cur_d >= 0), bias(cur_d), NEG)
    meta1 = jnp.where(is_meta, bias(prev_d), NEG)
    meta2 = jnp.where(is_meta, bias(2 * blk + i - j), NEG)
    case0 = jnp.concatenate([meta0, neg, neg], axis=-1)
    case1 = jnp.concatenate([meta1, neg, cur], axis=-1)
    case2 = jnp.concatenate([meta2, prev, cur], axis=-1)
    tab = jnp.stack([case0, case1, case2], axis=0)
    tab = tab.reshape(3, A_KV_HEADS, A_GROUP, blk, 3 * blk)
    return tab.transpose(0, 1, 4, 2, 3).reshape(3, A_KV_HEADS, 3 * blk, A_GROUP * blk)


assert A_KV_HEADS * HEAD_DIM == LANES
_PAIRED_HEADS = tuple(kv * A_GROUP + g for g in range(A_GROUP) for kv in range(A_KV_HEADS))


def _pair_heads(w):
    rest = w.shape[1:]
    w = w.reshape(A_KV_HEADS, A_GROUP, HEAD_DIM, *rest)
    return jnp.swapaxes(w, 0, 1).reshape(A_WIDTH, *rest)


def _swa_kernel(q_ref, kc_ref, kp_ref, km_ref, vc_ref, vp_ref, vm_ref, tab_ref, sink_ref, out_ref):
    scale = HEAD_DIM ** -0.5
    blk = A_BLOCK
    hd = HEAD_DIM
    nb = q_ref.shape[0] // blk
    step = pl.program_id(1)
    kvs = range(A_KV_HEADS)
    upper = _iota2((blk, LANES), 1) >= hd

    v_t = jnp.concatenate([vm_ref[...], vp_ref[...], vc_ref[...]], axis=0).T
    ones = jnp.ones((hd, 3 * blk), F32)

    probs = [(i, kv) for i in range(nb) for kv in kvs]
    s_t = {}
    for i in range(nb):
        rows = slice(i * blk, (i + 1) * blk)
        prev = kp_ref[...] if i == 0 else kc_ref[(i - 1) * blk:i * blk, :]
        keys = jnp.concatenate([km_ref[...], prev, kc_ref[rows, :]], axis=0)
        for kv in kvs:
            mine = upper if kv == 1 else jnp.logical_not(upper)
            q = jnp.concatenate([jnp.where(mine, q_ref[rows, g * LANES:(g + 1) * LANES], 0.0)
                                 for g in range(A_GROUP)], axis=0) * scale
            s_t[i, kv] = _mm_nt(keys, q)
    p_t, esink = {}, {}
    for i, kv in probs:
        case = 2 if i >= 2 else jnp.minimum(step * nb + i, 2)
        sk = s_t[i, kv] + tab_ref[case, kv]
        sink = sink_ref[kv][0:1, :]
        m = jnp.maximum(jnp.max(sk, axis=0, keepdims=True), sink)
        p_t[i, kv] = jnp.exp(sk - m)
        esink[i, kv] = jnp.exp(sink - m)
    o_t = {}
    for i, kv in probs:
        vr = slice(kv * hd, (kv + 1) * hd)
        prev = v_t[vr, blk:2 * blk] if i == 0 else v_t[vr, (i + 1) * blk:(i + 2) * blk]
        vals = jnp.concatenate([v_t[vr, 0:blk], prev, v_t[vr, (i + 2) * blk:(i + 3) * blk]], axis=1)
        r = _mm(jnp.concatenate([vals, ones], axis=0), p_t[i, kv])
        o_t[i, kv] = r[:hd] / (r[hd:] + esink[i, kv])
    for i in range(nb):
        tiles = [jnp.concatenate([o_t[i, kv][:, g * blk:(g + 1) * blk] for kv in kvs], axis=0).T
                 for g in range(A_GROUP)]
        out_ref[i * blk:(i + 1) * blk, :] = jnp.concatenate(tiles, axis=1)


def _swa(pa, sinks, tables):
    b, lp, _ = pa.shape
    blk = A_BLOCK
    kcol = A_WIDTH // A_KV_WIDTH
    sink = jnp.broadcast_to(jnp.repeat(sinks.astype(F32).reshape(A_KV_HEADS, 1, A_GROUP), blk, axis=2),
                            (A_KV_HEADS, 8, A_GROUP * blk))
    rows = _mix_rows(lp)
    nb = rows // blk
    cur_spec = lambda col: pl.BlockSpec((None, rows, A_KV_WIDTH), lambda i, j: (i, j, col))
    prev_spec = lambda col: pl.BlockSpec((None, blk, A_KV_WIDTH),
                                         lambda i, j: (i, jnp.maximum(j * nb - 1, 0), col))
    meta_spec = lambda col: pl.BlockSpec((None, blk, A_KV_WIDTH), lambda i, j: (i, 0, col))
    return pl.pallas_call(
        _swa_kernel,
        grid=(b, lp // rows),
        in_specs=[pl.BlockSpec((None, rows, A_WIDTH), lambda i, j: (i, j, 0)),
                  cur_spec(kcol), prev_spec(kcol), meta_spec(kcol),
                  cur_spec(kcol + 1), prev_spec(kcol + 1), meta_spec(kcol + 1),
                  pl.BlockSpec(tables.shape, lambda i, j: (0, 0, 0, 0), pipeline_mode=pl.Buffered(1)),
                  pl.BlockSpec(sink.shape, lambda i, j: (0, 0, 0))],
        out_specs=pl.BlockSpec((None, rows, A_WIDTH), lambda i, j: (i, j, 0)),
        out_shape=jax.ShapeDtypeStruct((b, lp, A_WIDTH), F32),
        compiler_params=_cparams(("arbitrary", "arbitrary")),
        name="swa",
    )(pa, pa, pa, pa, pa, pa, pa, tables, sink)


def _layer_norm_rows(z, ln_ref):
    mu = jnp.mean(z, axis=-1, keepdims=True)
    zc = z - mu
    var = jnp.mean(zc * zc, axis=-1, keepdims=True)
    return zc * lax.rsqrt(var + LN_EPS) * ln_ref[0:1, :] + ln_ref[1:2, :]


FF_CHUNK = 256


def _post_kernel(ym_ref, yr_ref, ya_ref, h_ref, w_ref, lnm_ref, wi_ref, wo_ref, lnf_ref, out_ref,
                 *, alpha, d_ff):
    tm = h_ref.shape[0]
    is_pad = (pl.program_id(1) * tm + _iota2((tm, 1), 0)) < FRONT_PAD
    mix = jnp.dot(ym_ref[...].astype(BF16), w_ref[0:M_WIDTH, :], preferred_element_type=F32)
    mix += jnp.dot(yr_ref[...].astype(BF16), w_ref[M_WIDTH:M_WIDTH + R_WIDTH, :],
                   preferred_element_type=F32)
    mix += jnp.dot(ya_ref[...].astype(BF16), w_ref[M_WIDTH + R_WIDTH:, :],
                   preferred_element_type=F32)
    x = jnp.where(is_pad, 0.0, _layer_norm_rows(alpha * h_ref[...] + mix, lnm_ref))
    xb = x.astype(BF16)
    acc = alpha * x
    for j in range(d_ff // FF_CHUNK):
        cs = slice(j * FF_CHUNK, (j + 1) * FF_CHUNK)
        us = slice(d_ff + j * FF_CHUNK, d_ff + (j + 1) * FF_CHUNK)
        gate = jnp.dot(xb, wi_ref[:, cs], preferred_element_type=F32)
        up = jnp.dot(xb, wi_ref[:, us], preferred_element_type=F32)
        act = (gate * jax.nn.sigmoid(gate) * up).astype(BF16)
        acc += jnp.dot(act, wo_ref[cs, :], preferred_element_type=F32)
    out_ref[...] = jnp.where(is_pad, 0.0, _layer_norm_rows(acc, lnf_ref))


def _post(ym, yr, ya, h, w, ln_mix, wi, wo, ln_ffn, alpha):
    b, lp, d = h.shape
    d_ff = wo.shape[0]
    assert d_ff % FF_CHUNK == 0
    tm = _row_tile(lp)
    spec = lambda width: pl.BlockSpec((None, tm, width), lambda i, j: (i, j, 0))
    const = lambda a: pl.BlockSpec(a.shape, lambda i, j: (0, 0), pipeline_mode=pl.Buffered(1))
    ln_spec = pl.BlockSpec((2, d), lambda i, j: (0, 0))
    return pl.pallas_call(
        functools.partial(_post_kernel, alpha=alpha, d_ff=d_ff),
        grid=(b, lp // tm),
        in_specs=[spec(M_WIDTH), spec(R_WIDTH), spec(A_WIDTH), spec(d), const(w), ln_spec,
                  const(wi), const(wo), ln_spec],
        out_specs=spec(d),
        out_shape=jax.ShapeDtypeStruct((b, lp, d), F32),
        compiler_params=_cparams(("arbitrary", "arbitrary")),
        name="out_proj_ffn",
    )(ym, yr, ya, h, w, ln_mix, wi, wo, ln_ffn)


def kernel(x, meta_tokens, rel_bias, w_in, m_gate_bias, m_norm_w, r_mu_rkv, r_mu_w, r_mu_a, r_mu_g, r_w0, r_w2, r_a0, r_a2, r_g2, r_k_k, r_k_a, r_r_k, r_ln_x, a_sinks, w_out, ln_mix, w_ff_in, w_ff_out, ln_ffn):
    b, seq, d = x.shape
    depth = w_in.shape[0]
    alpha = (2 * depth) ** 0.25
    assert (TOKEN_START + seq) % A_BLOCK == 0

    meta = jnp.broadcast_to(meta_tokens[None].astype(x.dtype), (b, N_META, d))
    h = jnp.concatenate([jnp.zeros((b, FRONT_PAD, d), x.dtype), meta, x], axis=1)
    tables = _attn_tables(rel_bias)
    a_end = PA_W
    m_end = a_end + M_RAW

    for l in range(depth):
        wl = w_in[l]
        w_q = _pair_heads(wl[:, :A_WIDTH].T).T
        w_pad = jnp.concatenate(
            [w_q, wl[:, A_WIDTH:m_end], jnp.zeros((d, PM_W - M_RAW), wl.dtype), wl[:, m_end:]],
            axis=1).astype(BF16)
        wo = w_out[l]
        wo = jnp.concatenate([wo[:M_WIDTH + R_WIDTH], _pair_heads(wo[M_WIDTH + R_WIDTH:])], axis=0)
        pa, pm, pr = _proj(h, w_pad)
        y_m = _mlstm(pm, m_gate_bias[l], m_norm_w[l])
        mu = jnp.concatenate([r_mu_rkv[l].reshape(-1), r_mu_w[l], r_mu_a[l], r_mu_g[l]])
        y_r = _rwkv(pr, mu, r_w0[l], r_w2[l], r_a0[l], r_a2[l], r_g2[l], r_k_k[l], r_k_a[l],
                    r_r_k[l], r_ln_x[l])
        y_a = _swa(pa, a_sinks[l], tables)
        h = _post(y_m, y_r, y_a, h, wo.astype(BF16), ln_mix[l],
                  w_ff_in[l].astype(BF16), w_ff_out[l].astype(BF16), ln_ffn[l], alpha)
    return h[:, TOKEN_START:]
```

```python
import functools
import math

import jax
import jax.numpy as jnp
from jax import lax
from jax.experimental import pallas as pl
from jax.experimental.pallas import tpu as pltpu

F32 = jnp.float32
BF16 = jnp.bfloat16

HEAD_DIM = 64
N_META = 16
M_HEADS = 4
M_WIDTH = M_HEADS * HEAD_DIM
M_CHUNK = 64
M_NORM_EPS = 1e-6
R_HEADS = 4
R_WIDTH = R_HEADS * HEAD_DIM
R_CHUNK = 128
R_DECAY_RANK = 32
R_A_RANK = 32
R_GATE_RANK = 64
R_GN_EPS = 64e-5
A_HEADS = 8
A_KV_HEADS = 2
A_GROUP = A_HEADS // A_KV_HEADS
A_WIDTH = A_HEADS * HEAD_DIM
A_KV_WIDTH = A_KV_HEADS * HEAD_DIM
WINDOW = 128
A_BLOCK = 128
N_BUCKETS = 32
MAX_DISTANCE = 128
LN_EPS = 1e-5
NEG = -1e30

LANES = 128
FRONT_PAD = A_BLOCK - N_META
TOKEN_START = FRONT_PAD + N_META

PA_W = A_WIDTH + 2 * A_KV_WIDTH
PM_W = 4 * M_WIDTH + LANES
PR_W = 3 * R_WIDTH + R_DECAY_RANK + R_A_RANK + R_GATE_RANK
M_RAW = 4 * M_WIDTH + 2 * M_HEADS

VMEM_LIMIT = 56 * 1024 * 1024
MIX_ROWS = 640


def _row_tile(n):
    for t in (640, 512, 256, 128, 64):
        if n % t == 0:
            return t
    raise ValueError(f"row count {n} has no supported tile")


def _mix_rows(lp):
    return MIX_ROWS if lp % MIX_ROWS == 0 else A_BLOCK


def _cparams(sem):
    return pltpu.CompilerParams(dimension_semantics=sem, vmem_limit_bytes=VMEM_LIMIT)


def _bf(x):
    return x.astype(BF16)


def _mm(a, b):
    return lax.dot_general(_bf(a), _bf(b), (((1,), (0,)), ((), ())), preferred_element_type=F32)


def _mm_nt(a, b):
    return lax.dot_general(_bf(a), _bf(b), (((1,), (1,)), ((), ())), preferred_element_type=F32)


def _mm_tn(a, b):
    return lax.dot_general(_bf(a), _bf(b), (((0,), (0,)), ((), ())), preferred_element_type=F32)


def _iota2(shape, dim):
    return lax.broadcasted_iota(jnp.int32, shape, dim)


def _split3(x):
    hi = _bf(x).astype(F32)
    r1 = x - hi
    mid = _bf(r1).astype(F32)
    return hi, mid, r1 - mid


def _chunk_mask(n, chunk, lower):
    r_i = _iota2((n, n), 0)
    c_i = _iota2((n, n), 1)
    sh = chunk.bit_length() - 1
    same = jnp.right_shift(r_i, sh) == jnp.right_shift(c_i, sh)
    return same & (r_i >= c_i) if lower else same


def _cumsum_rows(tri, x):
    w = x.shape[1]
    res = _mm(tri, jnp.concatenate(_split3(x), axis=1))
    return res[:, :w] + res[:, w:2 * w] + res[:, 2 * w:]


def _cumsum_lanes(tri, x):
    r = x.shape[0]
    res = _mm_nt(jnp.concatenate(_split3(x), axis=0), tri)
    return res[:r] + res[r:2 * r] + res[2 * r:]


def _softplus(x):
    return jnp.maximum(x, 0.0) + jnp.log1p(jnp.exp(-jnp.abs(x)))


def _proj_kernel(x_ref, w_ref, oa_ref, om_ref, or_ref):
    acc = jnp.dot(x_ref[...].astype(BF16), w_ref[...], preferred_element_type=F32)
    oa_ref[...] = acc[:, :PA_W]
    om_ref[...] = acc[:, PA_W:PA_W + PM_W]
    or_ref[...] = acc[:, PA_W + PM_W:]


def _proj(h, w):
    b, lp, d = h.shape
    tm = _row_tile(lp)
    n = w.shape[1]
    spec = lambda width: pl.BlockSpec((None, tm, width), lambda i, j: (i, j, 0))
    return pl.pallas_call(
        _proj_kernel,
        grid=(b, lp // tm),
        in_specs=[spec(d),
                  pl.BlockSpec((d, n), lambda i, j: (0, 0), pipeline_mode=pl.Buffered(1))],
        out_specs=[spec(PA_W), spec(PM_W), spec(PR_W)],
        out_shape=[jax.ShapeDtypeStruct((b, lp, PA_W), F32),
                   jax.ShapeDtypeStruct((b, lp, PM_W), F32),
                   jax.ShapeDtypeStruct((b, lp, PR_W), F32)],
        compiler_params=_cparams(("arbitrary", "arbitrary")),
        name="in_proj",
    )(h, w)


def _mlstm_kernel(pm_ref, bias_ref, nw_ref, out_ref, cst_ref, m_ref):
    blk = pl.program_id(1)
    tb = pm_ref.shape[0]
    ch = M_CHUNK
    hd = HEAD_DIM
    nch = tb // ch
    pair_w = 2 * hd

    @pl.when(blk == 0)
    def _():
        cst_ref[...] = jnp.zeros_like(cst_ref)
        m_ref[...] = jnp.full_like(m_ref, NEG)

    row = blk * tb + _iota2((tb, 1), 0)
    is_pad = row < FRONT_PAD
    gb = pm_ref[:, 4 * M_WIDTH:] + bias_ref[...]
    lane = _iota2((tb, LANES), 1)
    li = jnp.where(is_pad, NEG, gb)
    lf = jnp.where(is_pad, 0.0, -_softplus(-gb))
    gcol = jnp.where(lane < M_HEADS, li, jnp.where(lane < 2 * M_HEADS, lf, 0.0))
    sel = (_iota2((8, LANES), 0) == _iota2((8, LANES), 1)).astype(F32)
    grow = _mm_nt(jnp.concatenate([sel] * 3, axis=1), jnp.concatenate(_split3(gcol), axis=1))
    brow = _cumsum_lanes(_chunk_mask(tb, ch, True).astype(F32), grow)
    r8 = _iota2((8, tb), 0)
    t8 = jnp.bitwise_and(_iota2((8, tb), 1), ch - 1)
    cm = grow - pltpu.roll(brow, M_HEADS, axis=0)
    sh = 1
    while sh < ch:
        cm = jnp.where(t8 >= sh, jnp.maximum(cm, pltpu.roll(cm, sh, axis=1)), cm)
        sh *= 2
    x16 = jnp.concatenate([jnp.where(r8 < M_HEADS, grow, brow), cm], axis=0)
    nrep = 3 * M_HEADS * pair_w
    e_sel = (_iota2((16, nrep), 0) == jnp.right_shift(_iota2((16, nrep), 1), 7)).astype(F32)
    rep = _mm_tn(jnp.concatenate(_split3(x16), axis=0), jnp.concatenate([e_sel] * 3, axis=0))
    rep_of = lambda qn, h, cs: rep[cs, (qn * M_HEADS + h) * pair_w:(qn * M_HEADS + h + 1) * pair_w]

    causal = _iota2((ch, ch), 0) >= _iota2((ch, ch), 1)
    low_half = _iota2((ch, pair_w), 1) < hd

    probs = [(c, h) for c in range(nch) for h in range(M_HEADS)]
    q_p, k_p, vext_p = {}, {}, {}
    for c, h in probs:
        cs = slice(c * ch, (c + 1) * ch)
        ps = slice((h // 2) * pair_w, (h // 2 + 1) * pair_w)
        mine = low_half if h % 2 == 0 else jnp.logical_not(low_half)
        q_p[c, h] = jnp.where(mine, pm_ref[cs, ps], 0.0)
        k_p[c, h] = jnp.where(mine, pm_ref[cs, M_WIDTH + ps.start:M_WIDTH + ps.stop] * (hd ** -0.5), 0.0)
        vext_p[c, h] = jnp.where(mine, pm_ref[cs, 2 * M_WIDTH + ps.start:2 * M_WIDTH + ps.stop], 1.0)
    qk = {p: _mm_nt(q_p[p], k_p[p]) for p in probs}

    dw_p, sint_p, emt_p, wa_p, decay_p = {}, {}, {}, {}, {}
    for h in range(M_HEADS):
        m_prev = m_ref[h][0:1, :]
        for c in range(nch):
            cs = slice(c * ch, (c + 1) * ch)
            li_c, b_c, cm_c = rep_of(0, h, cs), rep_of(1, h, cs), rep_of(2, h, cs)
            b_r = brow[M_HEADS + h:M_HEADS + h + 1, cs]
            d_mat = jnp.where(causal, b_c[:, :ch] - b_r + grow[h:h + 1, cs], NEG)
            inter = b_c + m_prev
            m_t = jnp.maximum(b_c + cm_c, inter)
            dw_p[c, h] = jnp.where(causal, jnp.exp(d_mat - m_t[:, :ch]), 0.0)
            sint_p[c, h] = jnp.exp(inter - m_t)
            emt_p[c, h] = jnp.exp(-m_t)
            g = b_c[ch - 1:ch, :]
            m_new = jnp.maximum(g + m_prev, g + cm_c[ch - 1:ch, :])
            wa_p[c, h] = jnp.exp(g - b_c + li_c - m_new)
            decay_p[c, h] = jnp.exp(g + m_prev - m_new)
            m_prev = m_new
        m_ref[h] = jnp.broadcast_to(m_prev, m_ref.shape[1:])

    upd = {p: _mm_tn(vext_p[p], wa_p[p] * k_p[p]) for p in probs}
    cext_p = {}
    for h in range(M_HEADS):
        cext = cst_ref[h]
        for c in range(nch):
            cext_p[c, h] = cext
            cext = decay_p[c, h] * cext + upd[c, h]
        cst_ref[h] = cext
    inter_p = {p: _mm_nt(q_p[p], cext_p[p]) for p in probs}
    intra_p = {p: _mm(qk[p] * dw_p[p], vext_p[p]) for p in probs}

    rows = []
    for c in range(nch):
        pairs = []
        for j in range(M_HEADS // 2):
            halves = []
            for h in (2 * j, 2 * j + 1):
                tot = intra_p[c, h] + sint_p[c, h] * inter_p[c, h]
                den = pltpu.roll(tot, hd, axis=1)
                halves.append(tot / jnp.maximum(jnp.abs(den), emt_p[c, h]))
            pairs.append(jnp.where(low_half, halves[0], halves[1]))
        rows.append(jnp.concatenate(pairs, axis=1))
    hh = jnp.concatenate(rows, axis=0)

    head_ones = _chunk_mask(M_WIDTH, hd, False).astype(BF16)
    mu = _mm(hh, head_ones) * (1.0 / hd)
    hc = hh - mu
    var = _mm(hc * hc, head_ones) * (1.0 / hd)
    o_pre = pm_ref[:, 3 * M_WIDTH:4 * M_WIDTH]
    out_ref[...] = hc * lax.rsqrt(var + M_NORM_EPS) * nw_ref[...] * jax.nn.sigmoid(o_pre)


def _mlstm(pm, gate_bias, norm_w):
    b, lp, _ = pm.shape
    tb = _mix_rows(lp)
    bias = jnp.zeros((1, LANES), F32).at[0, :2 * M_HEADS].set(gate_bias.reshape(-1))
    return pl.pallas_call(
        _mlstm_kernel,
        grid=(b, lp // tb),
        in_specs=[pl.BlockSpec((None, tb, PM_W), lambda i, j: (i, j, 0)),
                  pl.BlockSpec((1, LANES), lambda i, j: (0, 0)),
                  pl.BlockSpec((1, M_WIDTH), lambda i, j: (0, 0))],
        out_specs=pl.BlockSpec((None, tb, M_WIDTH), lambda i, j: (i, j, 0)),
        out_shape=jax.ShapeDtypeStruct((b, lp, M_WIDTH), F32),
        scratch_shapes=[pltpu.VMEM((M_HEADS, 2 * HEAD_DIM, 2 * HEAD_DIM), F32),
                        pltpu.VMEM((M_HEADS, 8, LANES), F32)],
        compiler_params=_cparams(("arbitrary", "arbitrary")),
        name="mlstm",
    )(pm, bias, norm_w.reshape(1, M_WIDTH))


def _swap_pairs(w):
    rest = w.shape[1:]
    return jnp.flip(w.reshape(R_HEADS // 2, 2, HEAD_DIM, *rest), axis=1).reshape(R_WIDTH, *rest)


def _rwkv_kernel(pr_ref, prev_ref, mu_ref, w0_ref, w2_ref, a0_ref, a2_ref, g2_ref,
                 kk_ref, ka_ref, rk_ref, lnx_ref, out_ref, st_ref):
    blk = pl.program_id(1)
    tb = pr_ref.shape[0]
    ch = R_CHUNK
    rw = R_WIDTH
    hd = HEAD_DIM
    nch = tb // ch
    assert ch == LANES and 2 * hd == LANES

    @pl.when(blk == 0)
    def _():
        st_ref[...] = jnp.zeros_like(st_ref)

    x = pr_ref[...]
    last = jnp.where(blk == 0, 0.0, prev_ref[7:8, :])
    prev = jnp.where(_iota2((tb, 1), 0) == 0, last, pltpu.roll(x, 1, axis=0))
    t = x + (prev - x) * mu_ref[...]
    r = t[:, 0:rw]
    k = t[:, rw:2 * rw]
    v = t[:, 2 * rw:3 * rw]
    o = 3 * rw
    w_lat = t[:, o:o + R_DECAY_RANK]
    a_lat = t[:, o + R_DECAY_RANK:o + R_DECAY_RANK + R_A_RANK]
    g_lat = t[:, o + R_DECAY_RANK + R_A_RANK:]

    w_log = -_softplus(-(w0_ref[...] + _mm(jnp.tanh(w_lat), w2_ref[...]))) - 0.5
    lw = -jnp.exp(w_log)
    a = jax.nn.sigmoid(a0_ref[...] + _mm(a_lat, a2_ref[...]))
    g = _mm(jax.nn.sigmoid(g_lat), g2_ref[...])

    head_ones = _chunk_mask(rw, hd, False).astype(BF16)
    sh = hd.bit_length() - 1
    swap_ones = ((jnp.right_shift(_iota2((rw, rw), 0), sh) ^ 1)
                 == jnp.right_shift(_iota2((rw, rw), 1), sh)).astype(BF16)
    kk = k * kk_ref[...]
    kk = kk / jnp.maximum(jnp.sqrt(_mm(kk * kk, head_ones)), 1e-12)
    k2 = k * (1.0 + (a - 1.0) * ka_ref[...])
    avec = -kk
    bvec = kk * a
    cum = _cumsum_rows(_chunk_mask(tb, ch, True).astype(F32), lw)

    r_i = _iota2((ch, ch), 0)
    c_i = _iota2((ch, ch), 1)
    strict = r_i > c_i
    eye = r_i == c_i
    incl2 = jnp.concatenate([r_i >= c_i] * 2, axis=1)
    low_cols = c_i < hd
    low_rows = r_i < hd
    n_levels = ch.bit_length() - 1

    probs = [(c, h) for c in range(nch) for h in range(R_HEADS)]
    at_p, rt_p, ac_p, rc_p, vo_p, kb_t, kb_p, wc_p = ({} for _ in range(8))
    for c in range(nch):
        cs = slice(c * ch, (c + 1) * ch)
        cum_c = cum[cs]
        mid = cum_c[ch // 2 - 1:ch // 2, :]
        cl = cum_c[ch - 1:ch, :]
        e_mid = jnp.exp(-mid)
        a_true = avec[cs] * jnp.exp(cum_c - lw[cs])
        r_true = r[cs] * jnp.exp(cum_c)
        a_cen = a_true * e_mid
        r_cen = r_true * e_mid
        inv = jnp.exp(mid - cum_c)
        rel = jnp.exp(cl - cum_c)
        kt_c, bt_c = k2[cs] * inv, bvec[cs] * inv
        kp_c, bp_c = k2[cs] * rel, bvec[cs] * rel
        wc_c = jnp.exp(cl)
        v_c = v[cs]
        for h in range(R_HEADS):
            ps = slice((h // 2) * LANES, (h // 2 + 1) * LANES)
            mine = low_cols if h % 2 == 0 else jnp.logical_not(low_cols)
            key = (c, h)
            at_p[key] = jnp.where(mine, a_true[:, ps], 0.0)
            rt_p[key] = jnp.where(mine, r_true[:, ps], 0.0)
            ac_p[key] = jnp.where(mine, a_cen[:, ps], 0.0)
            rc_p[key] = jnp.where(mine, r_cen[:, ps], 0.0)
            vo_p[key] = jnp.where(mine, 0.0, v_c[:, ps])
            kb_t[key] = jnp.concatenate([kt_c[:, ps], bt_c[:, ps]], axis=0)
            kb_p[key] = jnp.concatenate([kp_c[:, ps], bp_c[:, ps]], axis=0)
            wc_p[key] = wc_c[:, ps]

    gram = {p: _mm_nt(jnp.concatenate([ac_p[p], rc_p[p]], axis=0), kb_t[p]) for p in probs}
    m_ak = {p: jnp.where(strict, gram[p][:ch, :ch], 0.0) for p in probs}
    pw = {p: jnp.where(strict, gram[p][:ch, ch:], 0.0) for p in probs}
    n_cat = {p: jnp.where(incl2, gram[p][ch:], 0.0) for p in probs}
    z = {p: at_p[p] + _mm(m_ak[p], vo_p[p]) for p in probs}
    for lvl in range(n_levels):
        if lvl + 1 < n_levels:
            res = {p: _mm(pw[p], jnp.concatenate([z[p], pw[p]], axis=1)) for p in probs}
            z = {p: z[p] + res[p][:, :LANES] for p in probs}
            pw = {p: res[p][:, LANES:] for p in probs}
        else:
            z = {p: z[p] + _mm(pw[p], z[p]) for p in probs}
    rhs = {p: jnp.concatenate([vo_p[p], z[p]], axis=0) for p in probs}
    ry = {p: _mm(n_cat[p], rhs[p]) for p in probs}
    tg = {p: _mm_tn(kb_p[p], rhs[p]) for p in probs}

    sts = [st_ref[h] for h in range(R_HEADS)]
    y_chunks = []
    for c in range(nch):
        nxt, g_full = [], []
        for h in range(R_HEADS):
            p = (c, h)
            mine = low_cols if h % 2 == 0 else jnp.logical_not(low_cols)
            my_rows = low_rows if h % 2 == 0 else jnp.logical_not(low_rows)
            r_eff = rt_p[p] + jnp.where(mine, ry[p], 0.0)
            t_full = (jnp.where(my_rows & mine, tg[p], 0.0)
                      + jnp.where(eye & mine, wc_p[p], 0.0))
            g_full.append(jnp.where(my_rows & jnp.logical_not(mine), tg[p], 0.0))
            nxt.append(_mm(jnp.concatenate([r_eff, t_full], axis=0), sts[h]))
        ys = [jnp.where(low_cols if h % 2 else jnp.logical_not(low_cols), nxt[h][:ch] + ry[(c, h)], 0.0)
              for h in range(R_HEADS)]
        y_chunks.append(jnp.concatenate([ys[2 * j] + ys[2 * j + 1] for j in range(R_HEADS // 2)], axis=1))
        sts = [nxt[h][ch:] + g_full[h] for h in range(R_HEADS)]
    for h in range(R_HEADS):
        st_ref[h] = sts[h]
    y = jnp.concatenate(y_chunks, axis=0)

    inv_d = 1.0 / hd
    mu = _mm(y, head_ones) * inv_d
    yc = y - mu
    var = _mm(yc * yc, head_ones) * inv_d
    yn = yc * lax.rsqrt(var + R_GN_EPS) * lnx_ref[0:1, :] + lnx_ref[1:2, :]
    bonus = _mm(r * k2 * rk_ref[...], swap_ones) * v
    out_ref[...] = (yn + bonus) * g


def _rwkv(pr, mu, w0, w2, a0, a2, g2, k_k, k_a, r_k, ln_x):
    b, lp, _ = pr.shape
    tb = _mix_rows(lp)
    per8 = tb // 8
    row = lambda a: a.reshape(1, -1)
    full = lambda a: pl.BlockSpec(a.shape, lambda i, j: (0,) * a.ndim)
    params = [row(mu), row(w0), w2, row(a0), a2, _swap_pairs(g2.T).T, row(k_k), row(k_a), row(r_k),
              _swap_pairs(ln_x.T).T]
    return pl.pallas_call(
        _rwkv_kernel,
        grid=(b, lp // tb),
        in_specs=[pl.BlockSpec((None, tb, PR_W), lambda i, j: (i, j, 0)),
                  pl.BlockSpec((None, 8, PR_W), lambda i, j: (i, jnp.maximum(j * per8 - 1, 0), 0))]
                 + [full(a) for a in params],
        out_specs=pl.BlockSpec((None, tb, R_WIDTH), lambda i, j: (i, j, 0)),
        out_shape=jax.ShapeDtypeStruct((b, lp, R_WIDTH), F32),
        scratch_shapes=[pltpu.VMEM((R_HEADS, LANES, LANES), F32)],
        compiler_params=_cparams(("arbitrary", "arbitrary")),
        name="rwkv7",
    )(pr, pr, *params)


def _t5_bucket(dist):
    max_exact = N_BUCKETS // 2
    d = jnp.maximum(dist, 1).astype(F32)
    large = max_exact + (jnp.log(d / max_exact) / math.log(MAX_DISTANCE / max_exact)
                         * (N_BUCKETS - max_exact)).astype(jnp.int32)
    large = jnp.minimum(large, N_BUCKETS - 1)
    return jnp.where(dist < max_exact, dist, large)


def _attn_tables(rel_bias):
    blk = A_BLOCK
    i = jnp.arange(blk)[:, None]
    j = jnp.arange(blk)[None, :]
    top = 3 * blk - 1
    d_rev = top - jnp.arange(4 * blk - 1)
    by_dist_rev = rel_bias.astype(F32)[:, _t5_bucket(jnp.maximum(d_rev, 0))]

    def bias(offset):
        rows = [by_dist_rev[:, top - offset - t:top - offset - t + blk] for t in range(blk)]
        return jnp.stack(rows, axis=1)

    neg = jnp.full((A_HEADS, blk, blk), NEG, F32)
    is_meta = j >= FRONT_PAD
    cur_d = i - j
    prev_d = blk + i - j
    cur = jnp.where(cur_d >= 0, bias(0), NEG)
    prev = jnp.where(prev_d < WINDOW, bias(blk), NEG)
    meta0 = jnp.where(is_meta & (cur_d >= 0), bias(0), NEG)
    meta1 = jnp.where(is_meta, bias(blk), NEG)
    meta2 = jnp.where(is_meta, bias(2 * blk), NEG)
    case0 = jnp.concatenate([meta0, neg, neg], axis=-1)
    case1 = jnp.concatenate([meta1, neg, cur], axis=-1)
    case2 = jnp.concatenate([meta2, prev, cur], axis=-1)
    tab = jnp.stack([case0, case1, case2], axis=0)
    tab = tab.reshape(3, A_KV_HEADS, A_GROUP, blk, 3 * blk)
    return tab.transpose(0, 1, 4, 2, 3).reshape(3, A_KV_HEADS, 3 * blk, A_GROUP * blk)


assert A_KV_HEADS * HEAD_DIM == LANES
_PAIRED_HEADS = tuple(kv * A_GROUP + g for g in range(A_GROUP) for kv in range(A_KV_HEADS))


def _pair_heads(w):
    rest = w.shape[1:]
    w = w.reshape(A_KV_HEADS, A_GROUP, HEAD_DIM, *rest)
    return jnp.swapaxes(w, 0, 1).reshape(A_WIDTH, *rest)


def _swa_kernel(q_ref, kc_ref, kp_ref, km_ref, vc_ref, vp_ref, vm_ref, tab_ref, sink_ref, out_ref):
    scale = HEAD_DIM ** -0.5
    blk = A_BLOCK
    hd = HEAD_DIM
    nb = q_ref.shape[0] // blk
    step = pl.program_id(1)
    kvs = range(A_KV_HEADS)
    upper = _iota2((blk, LANES), 1) >= hd

    v_t = jnp.concatenate([vm_ref[...], vp_ref[...], vc_ref[...]], axis=0).T
    ones = jnp.ones((hd, 3 * blk), F32)

    probs = [(i, kv) for i in range(nb) for kv in kvs]
    s_t = {}
    for i in range(nb):
        rows = slice(i * blk, (i + 1) * blk)
        prev = kp_ref[...] if i == 0 else kc_ref[(i - 1) * blk:i * blk, :]
        keys = jnp.concatenate([km_ref[...], prev, kc_ref[rows, :]], axis=0)
        for kv in kvs:
            mine = upper if kv == 1 else jnp.logical_not(upper)
            q = jnp.concatenate([jnp.where(mine, q_ref[rows, g * LANES:(g + 1) * LANES], 0.0)
                                 for g in range(A_GROUP)], axis=0) * scale
            s_t[i, kv] = _mm_nt(keys, q)
    p_t, esink = {}, {}
    for i, kv in probs:
        case = 2 if i >= 2 else jnp.minimum(step * nb + i, 2)
        sk = s_t[i, kv] + tab_ref[case, kv]
        sink = sink_ref[kv][0:1, :]
        m = jnp.maximum(jnp.max(sk, axis=0, keepdims=True), sink)
        p_t[i, kv] = jnp.exp(sk - m)
        esink[i, kv] = jnp.exp(sink - m)
    o_t = {}
    for i, kv in probs:
        vr = slice(kv * hd, (kv + 1) * hd)
        prev = v_t[vr, blk:2 * blk] if i == 0 else v_t[vr, (i + 1) * blk:(i + 2) * blk]
        vals = jnp.concatenate([v_t[vr, 0:blk], prev, v_t[vr, (i + 2) * blk:(i + 3) * blk]], axis=1)
        r = _mm(jnp.concatenate([vals, ones], axis=0), p_t[i, kv])
        o_t[i, kv] = r[:hd] / (r[hd:] + esink[i, kv])
    for i in range(nb):
        tiles = [jnp.concatenate([o_t[i, kv][:, g * blk:(g + 1) * blk] for kv in kvs], axis=0).T
                 for g in range(A_GROUP)]
        out_ref[i * blk:(i + 1) * blk, :] = jnp.concatenate(tiles, axis=1)


def _swa(pa, sinks, tables):
    b, lp, _ = pa.shape
    blk = A_BLOCK
    kcol = A_WIDTH // A_KV_WIDTH
    sink = jnp.broadcast_to(jnp.repeat(sinks.astype(F32).reshape(A_KV_HEADS, 1, A_GROUP), blk, axis=2),
                            (A_KV_HEADS, 8, A_GROUP * blk))
    rows = _mix_rows(lp)
    nb = rows // blk
    cur_spec = lambda col: pl.BlockSpec((None, rows, A_KV_WIDTH), lambda i, j: (i, j, col))
    prev_spec = lambda col: pl.BlockSpec((None, blk, A_KV_WIDTH),
                                         lambda i, j: (i, jnp.maximum(j * nb - 1, 0), col))
    meta_spec = lambda col: pl.BlockSpec((None, blk, A_KV_WIDTH), lambda i, j: (i, 0, col))
    return pl.pallas_call(
        _swa_kernel,
        grid=(b, lp // rows),
        in_specs=[pl.BlockSpec((None, rows, A_WIDTH), lambda i, j: (i, j, 0)),
                  cur_spec(kcol), prev_spec(kcol), meta_spec(kcol),
                  cur_spec(kcol + 1), prev_spec(kcol + 1), meta_spec(kcol + 1),
                  pl.BlockSpec(tables.shape, lambda i, j: (0, 0, 0, 0), pipeline_mode=pl.Buffered(1)),
                  pl.BlockSpec(sink.shape, lambda i, j: (0, 0, 0))],
        out_specs=pl.BlockSpec((None, rows, A_WIDTH), lambda i, j: (i, j, 0)),
        out_shape=jax.ShapeDtypeStruct((b, lp, A_WIDTH), F32),
        compiler_params=_cparams(("arbitrary", "arbitrary")),
        name="swa",
    )(pa, pa, pa, pa, pa, pa, pa, tables, sink)


def _layer_norm_rows(z, ln_ref):
    mu = jnp.mean(z, axis=-1, keepdims=True)
    zc = z - mu
    var = jnp.mean(zc * zc, axis=-1, keepdims=True)
    return zc * lax.rsqrt(var + LN_EPS) * ln_ref[0:1, :] + ln_ref[1:2, :]


FF_CHUNK = 256


def _post_kernel(ym_ref, yr_ref, ya_ref, h_ref, w_ref, lnm_ref, wi_ref, wo_ref, lnf_ref, out_ref,
                 *, alpha, d_ff):
    tm = h_ref.shape[0]
    is_pad = (pl.program_id(1) * tm + _iota2((tm, 1), 0)) < FRONT_PAD
    mix = jnp.dot(ym_ref[...].astype(BF16), w_ref[0:M_WIDTH, :], preferred_element_type=F32)
    mix += jnp.dot(yr_ref[...].astype(BF16), w_ref[M_WIDTH:M_WIDTH + R_WIDTH, :],
                   preferred_element_type=F32)
    mix += jnp.dot(ya_ref[...].astype(BF16), w_ref[M_WIDTH + R_WIDTH:, :],
                   preferred_element_type=F32)
    x = jnp.where(is_pad, 0.0, _layer_norm_rows(alpha * h_ref[...] + mix, lnm_ref))
    xb = x.astype(BF16)
    acc = alpha * x
    for j in range(d_ff // FF_CHUNK):
        cs = slice(j * FF_CHUNK, (j + 1) * FF_CHUNK)
        us = slice(d_ff + j * FF_CHUNK, d_ff + (j + 1) * FF_CHUNK)
        gate = jnp.dot(xb, wi_ref[:, cs], preferred_element_type=F32)
        up = jnp.dot(xb, wi_ref[:, us], preferred_element_type=F32)
        act = (gate * jax.nn.sigmoid(gate) * up).astype(BF16)
        acc += jnp.dot(act, wo_ref[cs, :], preferred_element_type=F32)
    out_ref[...] = jnp.where(is_pad, 0.0, _layer_norm_rows(acc, lnf_ref))


def _post(ym, yr, ya, h, w, ln_mix, wi, wo, ln_ffn, alpha):
    b, lp, d = h.shape
    d_ff = wo.shape[0]
    assert d_ff % FF_CHUNK == 0
    tm = _row_tile(lp)
    spec = lambda width: pl.BlockSpec((None, tm, width), lambda i, j: (i, j, 0))
    const = lambda a: pl.BlockSpec(a.shape, lambda i, j: (0, 0), pipeline_mode=pl.Buffered(1))
    ln_spec = pl.BlockSpec((2, d), lambda i, j: (0, 0))
    return pl.pallas_call(
        functools.partial(_post_kernel, alpha=alpha, d_ff=d_ff),
        grid=(b, lp // tm),
        in_specs=[spec(M_WIDTH), spec(R_WIDTH), spec(A_WIDTH), spec(d), const(w), ln_spec,
                  const(wi), const(wo), ln_spec],
        out_specs=spec(d),
        out_shape=jax.ShapeDtypeStruct((b, lp, d), F32),
        compiler_params=_cparams(("arbitrary", "arbitrary")),
        name="out_proj_ffn",
    )(ym, yr, ya, h, w, ln_mix, wi, wo, ln_ffn)


def kernel(x, meta_tokens, rel_bias, w_in, m_gate_bias, m_norm_w, r_mu_rkv, r_mu_w, r_mu_a, r_mu_g, r_w0, r_w2, r_a0, r_a2, r_g2, r_k_k, r_k_a, r_r_k, r_ln_x, a_sinks, w_out, ln_mix, w_ff_in, w_ff_out, ln_ffn):
    b, seq, d = x.shape
    depth = w_in.shape[0]
    alpha = (2 * depth) ** 0.25
    assert (TOKEN_START + seq) % A_BLOCK == 0

    meta = jnp.broadcast_to(meta_tokens[None].astype(x.dtype), (b, N_META, d))
    h = jnp.concatenate([jnp.zeros((b, FRONT_PAD, d), x.dtype), meta, x], axis=1)
    tables = _attn_tables(rel_bias)
    a_end = PA_W
    m_end = a_end + M_RAW

    for l in range(depth):
        wl = w_in[l]
        w_q = _pair_heads(wl[:, :A_WIDTH].T).T
        rv0 = m_end + 2 * R_WIDTH
        w_rv = _swap_pairs(wl[:, rv0:rv0 + R_WIDTH].T).T
        w_pad = jnp.concatenate(
            [w_q, wl[:, A_WIDTH:m_end], jnp.zeros((d, PM_W - M_RAW), wl.dtype), wl[:, m_end:rv0], w_rv,
             wl[:, rv0 + R_WIDTH:]], axis=1).astype(BF16)
        wo = w_out[l]
        wo = jnp.concatenate([wo[:M_WIDTH], _swap_pairs(wo[M_WIDTH:M_WIDTH + R_WIDTH]),
                              _pair_heads(wo[M_WIDTH + R_WIDTH:])], axis=0)
        pa, pm, pr = _proj(h, w_pad)
        y_m = _mlstm(pm, m_gate_bias[l], m_norm_w[l])
        mu = jnp.concatenate([r_mu_rkv[l, 0], r_mu_rkv[l, 1], _swap_pairs(r_mu_rkv[l, 2]),
                              r_mu_w[l], r_mu_a[l], r_mu_g[l]])
        y_r = _rwkv(pr, mu, r_w0[l], r_w2[l], r_a0[l], r_a2[l], r_g2[l], r_k_k[l], r_k_a[l],
                    r_r_k[l], r_ln_x[l])
        y_a = _swa(pa, a_sinks[l], tables)
        h = _post(y_m, y_r, y_a, h, wo.astype(BF16), ln_mix[l],
                  w_ff_in[l].astype(BF16), w_ff_out[l].astype(BF16), ln_ffn[l], alpha)
    return h[:, TOKEN_START:]
```

```python
import functools
import math

import jax
import jax.numpy as jnp
from jax import lax
from jax.experimental import pallas as pl
from jax.experimental.pallas import tpu as pltpu

F32 = jnp.float32
BF16 = jnp.bfloat16

HEAD_DIM = 64
N_META = 16
M_HEADS = 4
M_WIDTH = M_HEADS * HEAD_DIM
M_CHUNK = 64
M_NORM_EPS = 1e-6
R_HEADS = 4
R_WIDTH = R_HEADS * HEAD_DIM
R_CHUNK = 128
R_DECAY_RANK = 32
R_A_RANK = 32
R_GATE_RANK = 64
R_GN_EPS = 64e-5
A_HEADS = 8
A_KV_HEADS = 2
A_GROUP = A_HEADS // A_KV_HEADS
A_WIDTH = A_HEADS * HEAD_DIM
A_KV_WIDTH = A_KV_HEADS * HEAD_DIM
WINDOW = 128
A_BLOCK = 128
N_BUCKETS = 32
MAX_DISTANCE = 128
LN_EPS = 1e-5
NEG = -1e30

LANES = 128
FRONT_PAD = A_BLOCK - N_META
TOKEN_START = FRONT_PAD + N_META

PA_W = A_WIDTH + 2 * A_KV_WIDTH
PM_W = 4 * M_WIDTH + LANES
PR_W = 3 * R_WIDTH + R_DECAY_RANK + R_A_RANK + R_GATE_RANK
M_RAW = 4 * M_WIDTH + 2 * M_HEADS

VMEM_LIMIT = 56 * 1024 * 1024
MIX_ROWS = 640


def _row_tile(n):
    for t in (640, 512, 256, 128, 64):
        if n % t == 0:
            return t
    raise ValueError(f"row count {n} has no supported tile")


def _mix_rows(lp):
    return MIX_ROWS if lp % MIX_ROWS == 0 else A_BLOCK


def _cparams(sem):
    return pltpu.CompilerParams(dimension_semantics=sem, vmem_limit_bytes=VMEM_LIMIT)


def _bf(x):
    return x.astype(BF16)


def _mm(a, b):
    return lax.dot_general(_bf(a), _bf(b), (((1,), (0,)), ((), ())), preferred_element_type=F32)


def _mm_nt(a, b):
    return lax.dot_general(_bf(a), _bf(b), (((1,), (1,)), ((), ())), preferred_element_type=F32)


def _mm_tn(a, b):
    return lax.dot_general(_bf(a), _bf(b), (((0,), (0,)), ((), ())), preferred_element_type=F32)


def _iota2(shape, dim):
    return lax.broadcasted_iota(jnp.int32, shape, dim)


def _split3(x):
    hi = _bf(x).astype(F32)
    r1 = x - hi
    mid = _bf(r1).astype(F32)
    return hi, mid, r1 - mid


def _chunk_mask(n, chunk, lower):
    r_i = _iota2((n, n), 0)
    c_i = _iota2((n, n), 1)
    sh = chunk.bit_length() - 1
    same = jnp.right_shift(r_i, sh) == jnp.right_shift(c_i, sh)
    return same & (r_i >= c_i) if lower else same


def _cumsum_rows(tri, x):
    w = x.shape[1]
    res = _mm(tri, jnp.concatenate(_split3(x), axis=1))
    return res[:, :w] + res[:, w:2 * w] + res[:, 2 * w:]


def _cumsum_lanes(tri, x):
    r = x.shape[0]
    res = _mm_nt(jnp.concatenate(_split3(x), axis=0), tri)
    return res[:r] + res[r:2 * r] + res[2 * r:]


def _softplus(x):
    return jnp.maximum(x, 0.0) + jnp.log1p(jnp.exp(-jnp.abs(x)))


def _padded_tile_inputs(h, x, prefix, tm):
    if h is not None:
        return [h], [pl.BlockSpec((None, tm, h.shape[-1]), lambda i, j: (i, j, 0))]
    d = x.shape[-1]
    nb = tm // A_BLOCK
    specs = [pl.BlockSpec((A_BLOCK, d), lambda i, j: (0, 0))]
    specs += [pl.BlockSpec((None, A_BLOCK, d), lambda i, j, r=r: (i, jnp.maximum(j * nb - 1 + r, 0), 0))
              for r in range(nb)]
    return [prefix] + [x] * nb, specs


def _token_tile_inputs(a, tm):
    nb = tm // A_BLOCK
    first = TOKEN_START // A_BLOCK
    specs = [pl.BlockSpec((None, A_BLOCK, a.shape[-1]), lambda i, j, r=r: (i, first + j * nb + r, 0))
             for r in range(nb)]
    return [a] * nb, specs


def _read_tile(refs, from_x):
    if from_x:
        first = jnp.where(pl.program_id(1) == 0, refs[0][...], refs[1][...])
        return jnp.concatenate([first] + [r[...] for r in refs[2:]], axis=0)
    if len(refs) == 1:
        return refs[0][...]
    return jnp.concatenate([r[...] for r in refs], axis=0)


def _proj_kernel(*refs, n_in, from_x):
    w_ref, oa_ref, om_ref, or_ref = refs[n_in:]
    x = _read_tile(refs[:n_in], from_x)
    acc = jnp.dot(x.astype(BF16), w_ref[...], preferred_element_type=F32)
    oa_ref[...] = acc[:, :PA_W]
    om_ref[...] = acc[:, PA_W:PA_W + PM_W]
    or_ref[...] = acc[:, PA_W + PM_W:]


def _proj(h, x, prefix, w, lp):
    b = x.shape[0]
    d, n = w.shape
    tm = _row_tile(lp)
    arrays, specs = _padded_tile_inputs(h, x, prefix, tm)
    spec = lambda width: pl.BlockSpec((None, tm, width), lambda i, j: (i, j, 0))
    return pl.pallas_call(
        functools.partial(_proj_kernel, n_in=len(arrays), from_x=h is None),
        grid=(b, lp // tm),
        in_specs=specs + [pl.BlockSpec((d, n), lambda i, j: (0, 0), pipeline_mode=pl.Buffered(1))],
        out_specs=[spec(PA_W), spec(PM_W), spec(PR_W)],
        out_shape=[jax.ShapeDtypeStruct((b, lp, PA_W), F32),
                   jax.ShapeDtypeStruct((b, lp, PM_W), F32),
                   jax.ShapeDtypeStruct((b, lp, PR_W), F32)],
        compiler_params=_cparams(("arbitrary", "arbitrary")),
        name="in_proj",
    )(*arrays, w)


def _mlstm_kernel(pm_ref, bias_ref, nw_ref, out_ref, cst_ref, m_ref):
    blk = pl.program_id(1)
    tb = pm_ref.shape[0]
    ch = M_CHUNK
    hd = HEAD_DIM
    nch = tb // ch
    npair = M_HEADS // 2
    assert 2 * hd == LANES and 2 * ch == LANES

    @pl.when(blk == 0)
    def _():
        cst_ref[...] = jnp.zeros_like(cst_ref)
        m_ref[...] = jnp.full_like(m_ref, NEG)

    row = blk * tb + _iota2((tb, 1), 0)
    is_pad = row < FRONT_PAD
    gb = pm_ref[:, 4 * M_WIDTH:] + bias_ref[...]
    lane = _iota2((tb, LANES), 1)
    li = jnp.where(is_pad, NEG, gb)
    lf = jnp.where(is_pad, 0.0, -_softplus(-gb))
    gcol = jnp.where(lane < M_HEADS, li, jnp.where(lane < 2 * M_HEADS, lf, 0.0))
    sel = (_iota2((8, LANES), 0) == _iota2((8, LANES), 1)).astype(F32)
    grow = _mm_nt(jnp.concatenate([sel] * 3, axis=1), jnp.concatenate(_split3(gcol), axis=1))
    brow = _cumsum_lanes(_chunk_mask(tb, ch, True).astype(F32), grow)
    r8 = _iota2((8, tb), 0)
    t8 = jnp.bitwise_and(_iota2((8, tb), 1), ch - 1)
    cm = grow - pltpu.roll(brow, M_HEADS, axis=0)
    sh = 1
    while sh < ch:
        cm = jnp.where(t8 >= sh, jnp.maximum(cm, pltpu.roll(cm, sh, axis=1)), cm)
        sh *= 2
    rows_lb = jnp.where(r8 < M_HEADS, grow, brow)
    x16 = jnp.concatenate([rows_lb, cm], axis=0)
    nrep = 3 * M_WIDTH
    e_sel = (_iota2((16, nrep), 0) == jnp.right_shift(_iota2((16, nrep), 1), hd.bit_length() - 1)
             ).astype(F32)
    rep = _mm_tn(jnp.concatenate(_split3(x16), axis=0), jnp.concatenate([e_sel] * 3, axis=0))
    rep_of = lambda qn, j, cs: rep[cs, qn * M_WIDTH + j * LANES:qn * M_WIDTH + (j + 1) * LANES]
    rows_up = pltpu.roll(rows_lb, ch, axis=1)
    rows_dn = pltpu.roll(rows_lb, tb - ch, axis=1)
    low1 = _iota2((1, LANES), 1) < hd

    def row_pair(base, c, j):
        ts = slice((c // 2) * LANES, (c // 2 + 1) * LANES)
        h0, h1 = base + 2 * j, base + 2 * j + 1
        if c % 2 == 0:
            return jnp.where(low1, rows_lb[h0:h0 + 1, ts], rows_up[h1:h1 + 1, ts])
        return jnp.where(low1, rows_dn[h0:h0 + 1, ts], rows_lb[h1:h1 + 1, ts])

    low = _iota2((ch, LANES), 1) < hd
    t_i = _iota2((ch, LANES), 0)
    causal2 = t_i >= jnp.bitwise_and(_iota2((ch, LANES), 1), ch - 1)
    r128 = _iota2((LANES, LANES), 0)
    c128 = _iota2((LANES, LANES), 1)
    ones_bd = ((r128 < hd) == (c128 < hd)).astype(F32)
    bd2 = jnp.concatenate([ones_bd, ones_bd], axis=1) > 0.5
    ones_t = jnp.ones((ch, LANES), F32)

    probs = [(c, j) for c in range(nch) for j in range(npair)]
    q_p, k_p, v_p = {}, {}, {}
    for c, j in probs:
        cs = slice(c * ch, (c + 1) * ch)
        ps = slice(j * LANES, (j + 1) * LANES)
        q_p[c, j] = pm_ref[cs, ps]
        k_p[c, j] = pm_ref[cs, M_WIDTH + ps.start:M_WIDTH + ps.stop] * (hd ** -0.5)
        v_p[c, j] = pm_ref[cs, 2 * M_WIDTH + ps.start:2 * M_WIDTH + ps.stop]
    bdiag = lambda a: jnp.concatenate([jnp.where(low, a, 0.0), jnp.where(low, 0.0, a)], axis=0)
    qk = {p: _mm_nt(q_p[p], bdiag(k_p[p])) for p in probs}

    dw_p, sint_p, emt_p, wa_p, decay_p = {}, {}, {}, {}, {}
    for j in range(npair):
        m_prev = m_ref[j][0:1, :]
        for c in range(nch):
            cs = slice(c * ch, (c + 1) * ch)
            li_c, b_c, cm_c = rep_of(0, j, cs), rep_of(1, j, cs), rep_of(2, j, cs)
            d_mat = jnp.where(causal2, b_c - row_pair(M_HEADS, c, j) + row_pair(0, c, j), NEG)
            inter = b_c + m_prev
            m_t = jnp.maximum(b_c + cm_c, inter)
            dw_p[c, j] = jnp.where(causal2, jnp.exp(d_mat - m_t), 0.0)
            sint_p[c, j] = jnp.exp(inter - m_t)
            emt_p[c, j] = jnp.exp(-m_t)
            g = b_c[ch - 1:ch, :]
            m_new = jnp.maximum(g + m_prev, g + cm_c[ch - 1:ch, :])
            wa_p[c, j] = jnp.exp(g - b_c + li_c - m_new)
            decay_p[c, j] = jnp.exp(g + m_prev - m_new)
            m_prev = m_new
        m_ref[j] = jnp.broadcast_to(m_prev, m_ref.shape[1:])

    upd = {p: jnp.where(bd2, _mm_tn(wa_p[p] * k_p[p], jnp.concatenate([v_p[p], ones_t], axis=1)), 0.0)
           for p in probs}
    st_p = {}
    for j in range(npair):
        st = cst_ref[j]
        for c in range(nch):
            st_p[c, j] = st
            dec = decay_p[c, j]
            st = jnp.concatenate([dec, dec], axis=1) * st + upd[c, j]
        cst_ref[j] = st
    inter_p = {p: _mm(q_p[p], st_p[p]) for p in probs}
    intra_p = {p: _mm(qk[p] * dw_p[p], jnp.concatenate([bdiag(v_p[p]), ones_bd], axis=1)) for p in probs}

    rows = []
    for c in range(nch):
        pairs = []
        for j in range(npair):
            p = (c, j)
            tot = intra_p[p] + jnp.concatenate([sint_p[p], sint_p[p]], axis=1) * inter_p[p]
            pairs.append(tot[:, :LANES] / jnp.maximum(jnp.abs(tot[:, LANES:]), emt_p[p]))
        rows.append(jnp.concatenate(pairs, axis=1))
    hh = jnp.concatenate(rows, axis=0)

    head_ones = _chunk_mask(M_WIDTH, hd, False).astype(BF16)
    mu = _mm(hh, head_ones) * (1.0 / hd)
    hc = hh - mu
    var = _mm(hc * hc, head_ones) * (1.0 / hd)
    o_pre = pm_ref[:, 3 * M_WIDTH:4 * M_WIDTH]
    out_ref[...] = hc * lax.rsqrt(var + M_NORM_EPS) * nw_ref[...] * jax.nn.sigmoid(o_pre)


def _mlstm(pm, gate_bias, norm_w):
    b, lp, _ = pm.shape
    tb = _mix_rows(lp)
    bias = jnp.zeros((1, LANES), F32).at[0, :2 * M_HEADS].set(gate_bias.reshape(-1))
    return pl.pallas_call(
        _mlstm_kernel,
        grid=(b, lp // tb),
        in_specs=[pl.BlockSpec((None, tb, PM_W), lambda i, j: (i, j, 0)),
                  pl.BlockSpec((1, LANES), lambda i, j: (0, 0)),
                  pl.BlockSpec((1, M_WIDTH), lambda i, j: (0, 0))],
        out_specs=pl.BlockSpec((None, tb, M_WIDTH), lambda i, j: (i, j, 0)),
        out_shape=jax.ShapeDtypeStruct((b, lp, M_WIDTH), F32),
        scratch_shapes=[pltpu.VMEM((M_HEADS // 2, LANES, 2 * LANES), F32),
                        pltpu.VMEM((M_HEADS // 2, 8, LANES), F32)],
        compiler_params=_cparams(("arbitrary", "arbitrary")),
        name="mlstm",
    )(pm, bias, norm_w.reshape(1, M_WIDTH))


def _swap_pairs(w):
    rest = w.shape[1:]
    return jnp.flip(w.reshape(R_HEADS // 2, 2, HEAD_DIM, *rest), axis=1).reshape(R_WIDTH, *rest)


def _rwkv_kernel(pr_ref, prev_ref, mu_ref, w0_ref, w2_ref, a0_ref, a2_ref, g2_ref,
                 kk_ref, ka_ref, rk_ref, lnx_ref, out_ref, st_ref):
    blk = pl.program_id(1)
    tb = pr_ref.shape[0]
    ch = R_CHUNK
    rw = R_WIDTH
    hd = HEAD_DIM
    nch = tb // ch
    assert ch == LANES and 2 * hd == LANES

    @pl.when(blk == 0)
    def _():
        st_ref[...] = jnp.zeros_like(st_ref)

    x = pr_ref[...]
    last = jnp.where(blk == 0, 0.0, prev_ref[7:8, :])
    prev = jnp.where(_iota2((tb, 1), 0) == 0, last, pltpu.roll(x, 1, axis=0))
    t = x + (prev - x) * mu_ref[...]
    r = t[:, 0:rw]
    k = t[:, rw:2 * rw]
    v = t[:, 2 * rw:3 * rw]
    o = 3 * rw
    w_lat = t[:, o:o + R_DECAY_RANK]
    a_lat = t[:, o + R_DECAY_RANK:o + R_DECAY_RANK + R_A_RANK]
    g_lat = t[:, o + R_DECAY_RANK + R_A_RANK:]

    w_log = -_softplus(-(w0_ref[...] + _mm(jnp.tanh(w_lat), w2_ref[...]))) - 0.5
    lw = -jnp.exp(w_log)
    a = jax.nn.sigmoid(a0_ref[...] + _mm(a_lat, a2_ref[...]))
    g = _mm(jax.nn.sigmoid(g_lat), g2_ref[...])

    head_ones = _chunk_mask(rw, hd, False).astype(BF16)
    sh = hd.bit_length() - 1
    swap_ones = ((jnp.right_shift(_iota2((rw, rw), 0), sh) ^ 1)
                 == jnp.right_shift(_iota2((rw, rw), 1), sh)).astype(BF16)
    kk = k * kk_ref[...]
    kk = kk / jnp.maximum(jnp.sqrt(_mm(kk * kk, head_ones)), 1e-12)
    k2 = k * (1.0 + (a - 1.0) * ka_ref[...])
    avec = -kk
    bvec = kk * a
    cum = _cumsum_rows(_chunk_mask(tb, ch, True).astype(F32), lw)

    r_i = _iota2((ch, ch), 0)
    c_i = _iota2((ch, ch), 1)
    strict = r_i > c_i
    eye = r_i == c_i
    incl2 = jnp.concatenate([r_i >= c_i] * 2, axis=1)
    low_cols = c_i < hd
    low_rows = r_i < hd
    n_levels = ch.bit_length() - 1

    probs = [(c, h) for c in range(nch) for h in range(R_HEADS)]
    at_p, rt_p, ac_p, rc_p, vo_p, kb_t, kb_p, wc_p = ({} for _ in range(8))
    for c in range(nch):
        cs = slice(c * ch, (c + 1) * ch)
        cum_c = cum[cs]
        mid = cum_c[ch // 2 - 1:ch // 2, :]
        cl = cum_c[ch - 1:ch, :]
        e_mid = jnp.exp(-mid)
        a_true = avec[cs] * jnp.exp(cum_c - lw[cs])
        r_true = r[cs] * jnp.exp(cum_c)
        a_cen = a_true * e_mid
        r_cen = r_true * e_mid
        inv = jnp.exp(mid - cum_c)
        rel = jnp.exp(cl - cum_c)
        kt_c, bt_c = k2[cs] * inv, bvec[cs] * inv
        kp_c, bp_c = k2[cs] * rel, bvec[cs] * rel
        wc_c = jnp.exp(cl)
        v_c = v[cs]
        for h in range(R_HEADS):
            ps = slice((h // 2) * LANES, (h // 2 + 1) * LANES)
            mine = low_cols if h % 2 == 0 else jnp.logical_not(low_cols)
            key = (c, h)
            at_p[key] = jnp.where(mine, a_true[:, ps], 0.0)
            rt_p[key] = jnp.where(mine, r_true[:, ps], 0.0)
            ac_p[key] = jnp.where(mine, a_cen[:, ps], 0.0)
            rc_p[key] = jnp.where(mine, r_cen[:, ps], 0.0)
            vo_p[key] = jnp.where(mine, 0.0, v_c[:, ps])
            kb_t[key] = jnp.concatenate([kt_c[:, ps], bt_c[:, ps]], axis=0)
            kb_p[key] = jnp.concatenate([kp_c[:, ps], bp_c[:, ps]], axis=0)
            wc_p[key] = wc_c[:, ps]

    gram = {p: _mm_nt(jnp.concatenate([ac_p[p], rc_p[p]], axis=0), kb_t[p]) for p in probs}
    m_ak = {p: jnp.where(strict, gram[p][:ch, :ch], 0.0) for p in probs}
    pw = {p: jnp.where(strict, gram[p][:ch, ch:], 0.0) for p in probs}
    n_cat = {p: jnp.where(incl2, gram[p][ch:], 0.0) for p in probs}
    z = {p: at_p[p] + _mm(m_ak[p], vo_p[p]) for p in probs}
    for lvl in range(n_levels):
        if lvl + 1 < n_levels:
            res = {p: _mm(pw[p], jnp.concatenate([z[p], pw[p]], axis=1)) for p in probs}
            z = {p: z[p] + res[p][:, :LANES] for p in probs}
            pw = {p: res[p][:, LANES:] for p in probs}
        else:
            z = {p: z[p] + _mm(pw[p], z[p]) for p in probs}
    rhs = {p: jnp.concatenate([vo_p[p], z[p]], axis=0) for p in probs}
    ry = {p: _mm(n_cat[p], rhs[p]) for p in probs}
    tg = {p: _mm_tn(kb_p[p], rhs[p]) for p in probs}

    sts = [st_ref[h] for h in range(R_HEADS)]
    y_chunks = []
    for c in range(nch):
        nxt, g_full = [], []
        for h in range(R_HEADS):
            p = (c, h)
            mine = low_cols if h % 2 == 0 else jnp.logical_not(low_cols)
            my_rows = low_rows if h % 2 == 0 else jnp.logical_not(low_rows)
            r_eff = rt_p[p] + jnp.where(mine, ry[p], 0.0)
            t_full = (jnp.where(my_rows & mine, tg[p], 0.0)
                      + jnp.where(eye & mine, wc_p[p], 0.0))
            g_full.append(jnp.where(my_rows & jnp.logical_not(mine), tg[p], 0.0))
            nxt.append(_mm(jnp.concatenate([r_eff, t_full], axis=0), sts[h]))
        ys = [jnp.where(low_cols if h % 2 else jnp.logical_not(low_cols), nxt[h][:ch] + ry[(c, h)], 0.0)
              for h in range(R_HEADS)]
        y_chunks.append(jnp.concatenate([ys[2 * j] + ys[2 * j + 1] for j in range(R_HEADS // 2)], axis=1))
        sts = [nxt[h][ch:] + g_full[h] for h in range(R_HEADS)]
    for h in range(R_HEADS):
        st_ref[h] = sts[h]
    y = jnp.concatenate(y_chunks, axis=0)

    inv_d = 1.0 / hd
    mu = _mm(y, head_ones) * inv_d
    yc = y - mu
    var = _mm(yc * yc, head_ones) * inv_d
    yn = yc * lax.rsqrt(var + R_GN_EPS) * lnx_ref[0:1, :] + lnx_ref[1:2, :]
    bonus = _mm(r * k2 * rk_ref[...], swap_ones) * v
    out_ref[...] = (yn + bonus) * g


def _rwkv(pr, mu, w0, w2, a0, a2, g2, k_k, k_a, r_k, ln_x):
    b, lp, _ = pr.shape
    tb = _mix_rows(lp)
    per8 = tb // 8
    row = lambda a: a.reshape(1, -1)
    full = lambda a: pl.BlockSpec(a.shape, lambda i, j: (0,) * a.ndim)
    params = [row(mu), row(w0), w2, row(a0), a2, _swap_pairs(g2.T).T, row(k_k), row(k_a), row(r_k),
              _swap_pairs(ln_x.T).T]
    return pl.pallas_call(
        _rwkv_kernel,
        grid=(b, lp // tb),
        in_specs=[pl.BlockSpec((None, tb, PR_W), lambda i, j: (i, j, 0)),
                  pl.BlockSpec((None, 8, PR_W), lambda i, j: (i, jnp.maximum(j * per8 - 1, 0), 0))]
                 + [full(a) for a in params],
        out_specs=pl.BlockSpec((None, tb, R_WIDTH), lambda i, j: (i, j, 0)),
        out_shape=jax.ShapeDtypeStruct((b, lp, R_WIDTH), F32),
        scratch_shapes=[pltpu.VMEM((R_HEADS, LANES, LANES), F32)],
        compiler_params=_cparams(("arbitrary", "arbitrary")),
        name="rwkv7",
    )(pr, pr, *params)


def _t5_bucket(dist):
    max_exact = N_BUCKETS // 2
    d = jnp.maximum(dist, 1).astype(F32)
    large = max_exact + (jnp.log(d / max_exact) / math.log(MAX_DISTANCE / max_exact)
                         * (N_BUCKETS - max_exact)).astype(jnp.int32)
    large = jnp.minimum(large, N_BUCKETS - 1)
    return jnp.where(dist < max_exact, dist, large)


def _attn_tables(rel_bias):
    blk = A_BLOCK
    i = jnp.arange(blk)[:, None]
    j = jnp.arange(blk)[None, :]
    top = 3 * blk - 1
    d_rev = top - jnp.arange(4 * blk - 1)
    by_dist_rev = rel_bias.astype(F32)[:, _t5_bucket(jnp.maximum(d_rev, 0))]

    def bias(offset):
        rows = [by_dist_rev[:, top - offset - t:top - offset - t + blk] for t in range(blk)]
        return jnp.stack(rows, axis=1)

    neg = jnp.full((A_HEADS, blk, blk), NEG, F32)
    is_meta = j >= FRONT_PAD
    cur_d = i - j
    prev_d = blk + i - j
    cur = jnp.where(cur_d >= 0, bias(0), NEG)
    prev = jnp.where(prev_d < WINDOW, bias(blk), NEG)
    meta0 = jnp.where(is_meta & (cur_d >= 0), bias(0), NEG)
    meta1 = jnp.where(is_meta, bias(blk), NEG)
    meta2 = jnp.where(is_meta, bias(2 * blk), NEG)
    case0 = jnp.concatenate([meta0, neg, neg], axis=-1)
    case1 = jnp.concatenate([meta1, neg, cur], axis=-1)
    case2 = jnp.concatenate([meta2, prev, cur], axis=-1)
    tab = jnp.stack([case0, case1, case2], axis=0)
    tab = tab.reshape(3, A_KV_HEADS, A_GROUP, blk, 3 * blk)
    return tab.transpose(0, 1, 4, 2, 3).reshape(3, A_KV_HEADS, 3 * blk, A_GROUP * blk)


assert A_KV_HEADS * HEAD_DIM == LANES
_PAIRED_HEADS = tuple(kv * A_GROUP + g for g in range(A_GROUP) for kv in range(A_KV_HEADS))


def _pair_heads(w):
    rest = w.shape[1:]
    w = w.reshape(A_KV_HEADS, A_GROUP, HEAD_DIM, *rest)
    return jnp.swapaxes(w, 0, 1).reshape(A_WIDTH, *rest)


def _swa_kernel(q_ref, kc_ref, kp_ref, km_ref, vc_ref, vp_ref, vm_ref, tab_ref, sink_ref, out_ref):
    scale = HEAD_DIM ** -0.5
    blk = A_BLOCK
    hd = HEAD_DIM
    nb = q_ref.shape[0] // blk
    step = pl.program_id(1)
    kvs = range(A_KV_HEADS)
    upper = _iota2((blk, LANES), 1) >= hd

    v_t = jnp.concatenate([vm_ref[...], vp_ref[...], vc_ref[...]], axis=0).T
    ones = jnp.ones((hd, 3 * blk), F32)

    probs = [(i, kv) for i in range(nb) for kv in kvs]
    s_t = {}
    for i in range(nb):
        rows = slice(i * blk, (i + 1) * blk)
        prev = kp_ref[...] if i == 0 else kc_ref[(i - 1) * blk:i * blk, :]
        keys = jnp.concatenate([km_ref[...], prev, kc_ref[rows, :]], axis=0)
        for kv in kvs:
            mine = upper if kv == 1 else jnp.logical_not(upper)
            q = jnp.concatenate([jnp.where(mine, q_ref[rows, g * LANES:(g + 1) * LANES], 0.0)
                                 for g in range(A_GROUP)], axis=0) * scale
            s_t[i, kv] = _mm_nt(keys, q)
    p_t, esink = {}, {}
    for i, kv in probs:
        case = 2 if i >= 2 else jnp.minimum(step * nb + i, 2)
        sk = s_t[i, kv] + tab_ref[case, kv]
        sink = sink_ref[kv][0:1, :]
        m = jnp.maximum(jnp.max(sk, axis=0, keepdims=True), sink)
        p_t[i, kv] = jnp.exp(sk - m)
        esink[i, kv] = jnp.exp(sink - m)
    o_t = {}
    for i, kv in probs:
        vr = slice(kv * hd, (kv + 1) * hd)
        prev = v_t[vr, blk:2 * blk] if i == 0 else v_t[vr, (i + 1) * blk:(i + 2) * blk]
        vals = jnp.concatenate([v_t[vr, 0:blk], prev, v_t[vr, (i + 2) * blk:(i + 3) * blk]], axis=1)
        r = _mm(jnp.concatenate([vals, ones], axis=0), p_t[i, kv])
        o_t[i, kv] = r[:hd] / (r[hd:] + esink[i, kv])
    for i in range(nb):
        tiles = [jnp.concatenate([o_t[i, kv][:, g * blk:(g + 1) * blk] for kv in kvs], axis=0).T
                 for g in range(A_GROUP)]
        out_ref[i * blk:(i + 1) * blk, :] = jnp.concatenate(tiles, axis=1)


def _swa(pa, sinks, tables):
    b, lp, _ = pa.shape
    blk = A_BLOCK
    kcol = A_WIDTH // A_KV_WIDTH
    sink = jnp.broadcast_to(jnp.repeat(sinks.astype(F32).reshape(A_KV_HEADS, 1, A_GROUP), blk, axis=2),
                            (A_KV_HEADS, 8, A_GROUP * blk))
    rows = _mix_rows(lp)
    nb = rows // blk
    cur_spec = lambda col: pl.BlockSpec((None, rows, A_KV_WIDTH), lambda i, j: (i, j, col))
    prev_spec = lambda col: pl.BlockSpec((None, blk, A_KV_WIDTH),
                                         lambda i, j: (i, jnp.maximum(j * nb - 1, 0), col))
    meta_spec = lambda col: pl.BlockSpec((None, blk, A_KV_WIDTH), lambda i, j: (i, 0, col))
    return pl.pallas_call(
        _swa_kernel,
        grid=(b, lp // rows),
        in_specs=[pl.BlockSpec((None, rows, A_WIDTH), lambda i, j: (i, j, 0)),
                  cur_spec(kcol), prev_spec(kcol), meta_spec(kcol),
                  cur_spec(kcol + 1), prev_spec(kcol + 1), meta_spec(kcol + 1),
                  pl.BlockSpec(tables.shape, lambda i, j: (0, 0, 0, 0), pipeline_mode=pl.Buffered(1)),
                  pl.BlockSpec(sink.shape, lambda i, j: (0, 0, 0))],
        out_specs=pl.BlockSpec((None, rows, A_WIDTH), lambda i, j: (i, j, 0)),
        out_shape=jax.ShapeDtypeStruct((b, lp, A_WIDTH), F32),
        compiler_params=_cparams(("arbitrary", "arbitrary")),
        name="swa",
    )(pa, pa, pa, pa, pa, pa, pa, tables, sink)


def _layer_norm_rows(z, ln_ref):
    mu = jnp.mean(z, axis=-1, keepdims=True)
    zc = z - mu
    var = jnp.mean(zc * zc, axis=-1, keepdims=True)
    return zc * lax.rsqrt(var + LN_EPS) * ln_ref[0:1, :] + ln_ref[1:2, :]


FF_CHUNK = 256


def _post_kernel(*refs, n_each, from_x, final, alpha, d_ff):
    groups, pos = [], 0
    for n in n_each:
        groups.append(refs[pos:pos + n])
        pos += n
    w_ref, lnm_ref, wi_ref, wo_ref, lnf_ref, out_ref = refs[pos:]
    ym, yr, ya = (_read_tile(g, False) for g in groups[:3])
    h = _read_tile(groups[3], from_x)
    tm = h.shape[0]
    mix = jnp.dot(ym.astype(BF16), w_ref[0:M_WIDTH, :], preferred_element_type=F32)
    mix += jnp.dot(yr.astype(BF16), w_ref[M_WIDTH:M_WIDTH + R_WIDTH, :], preferred_element_type=F32)
    mix += jnp.dot(ya.astype(BF16), w_ref[M_WIDTH + R_WIDTH:, :], preferred_element_type=F32)
    x = _layer_norm_rows(alpha * h + mix, lnm_ref)
    if not final:
        is_pad = (pl.program_id(1) * tm + _iota2((tm, 1), 0)) < FRONT_PAD
        x = jnp.where(is_pad, 0.0, x)
    xb = x.astype(BF16)
    acc = alpha * x
    for j in range(d_ff // FF_CHUNK):
        cs = slice(j * FF_CHUNK, (j + 1) * FF_CHUNK)
        us = slice(d_ff + j * FF_CHUNK, d_ff + (j + 1) * FF_CHUNK)
        gate = jnp.dot(xb, wi_ref[:, cs], preferred_element_type=F32)
        up = jnp.dot(xb, wi_ref[:, us], preferred_element_type=F32)
        act = (gate * jax.nn.sigmoid(gate) * up).astype(BF16)
        acc += jnp.dot(act, wo_ref[cs, :], preferred_element_type=F32)
    y = _layer_norm_rows(acc, lnf_ref)
    out_ref[...] = y if final else jnp.where(is_pad, 0.0, y)


def _post(ym, yr, ya, h, x, prefix, w, ln_mix, wi, wo, ln_ffn, alpha, final):
    b, lp, _ = ym.shape
    seq, d = x.shape[1:]
    d_ff = wo.shape[0]
    assert d_ff % FF_CHUNK == 0
    from_x = h is None
    if final:
        tm = _row_tile(seq)
        groups = [_token_tile_inputs(a, tm) for a in (ym, yr, ya)]
        groups.append(([x], [pl.BlockSpec((None, tm, d), lambda i, j: (i, j, 0))]) if from_x
                      else _token_tile_inputs(h, tm))
        rows_out = seq
    else:
        tm = _row_tile(lp)
        groups = [_padded_tile_inputs(a, None, None, tm) for a in (ym, yr, ya)]
        groups.append(_padded_tile_inputs(h, x, prefix, tm))
        rows_out = lp
    arrays = [a for g in groups for a in g[0]]
    specs = [s for g in groups for s in g[1]]
    const = lambda a: pl.BlockSpec(a.shape, lambda i, j: (0, 0), pipeline_mode=pl.Buffered(1))
    ln_spec = pl.BlockSpec((2, d), lambda i, j: (0, 0))
    return pl.pallas_call(
        functools.partial(_post_kernel, n_each=tuple(len(g[0]) for g in groups),
                          from_x=from_x and not final, final=final, alpha=alpha, d_ff=d_ff),
        grid=(b, rows_out // tm),
        in_specs=specs + [const(w), ln_spec, const(wi), const(wo), ln_spec],
        out_specs=pl.BlockSpec((None, tm, d), lambda i, j: (i, j, 0)),
        out_shape=jax.ShapeDtypeStruct((b, rows_out, d), F32),
        compiler_params=_cparams(("arbitrary", "arbitrary")),
        name="out_proj_ffn",
    )(*arrays, w, ln_mix, wi, wo, ln_ffn)


def kernel(x, meta_tokens, rel_bias, w_in, m_gate_bias, m_norm_w, r_mu_rkv, r_mu_w, r_mu_a, r_mu_g, r_w0, r_w2, r_a0, r_a2, r_g2, r_k_k, r_k_a, r_r_k, r_ln_x, a_sinks, w_out, ln_mix, w_ff_in, w_ff_out, ln_ffn):
    b, seq, d = x.shape
    depth = w_in.shape[0]
    alpha = (2 * depth) ** 0.25
    assert (TOKEN_START + seq) % A_BLOCK == 0

    lp = TOKEN_START + seq
    prefix = jnp.concatenate([jnp.zeros((FRONT_PAD, d), x.dtype), meta_tokens.astype(x.dtype)], axis=0)
    tables = _attn_tables(rel_bias)
    a_end = PA_W
    m_end = a_end + M_RAW

    h = None
    for l in range(depth):
        wl = w_in[l]
        w_q = _pair_heads(wl[:, :A_WIDTH].T).T
        rv0 = m_end + 2 * R_WIDTH
        w_rv = _swap_pairs(wl[:, rv0:rv0 + R_WIDTH].T).T
        w_pad = jnp.concatenate(
            [w_q, wl[:, A_WIDTH:m_end], jnp.zeros((d, PM_W - M_RAW), wl.dtype), wl[:, m_end:rv0], w_rv,
             wl[:, rv0 + R_WIDTH:]], axis=1).astype(BF16)
        wo = w_out[l]
        wo = jnp.concatenate([wo[:M_WIDTH], _swap_pairs(wo[M_WIDTH:M_WIDTH + R_WIDTH]),
                              _pair_heads(wo[M_WIDTH + R_WIDTH:])], axis=0)
        pa, pm, pr = _proj(h, x, prefix, w_pad, lp)
        y_m = _mlstm(pm, m_gate_bias[l], m_norm_w[l])
        mu = jnp.concatenate([r_mu_rkv[l, 0], r_mu_rkv[l, 1], _swap_pairs(r_mu_rkv[l, 2]),
                              r_mu_w[l], r_mu_a[l], r_mu_g[l]])
        y_r = _rwkv(pr, mu, r_w0[l], r_w2[l], r_a0[l], r_a2[l], r_g2[l], r_k_k[l], r_k_a[l],
                    r_r_k[l], r_ln_x[l])
        y_a = _swa(pa, a_sinks[l], tables)
        h = _post(y_m, y_r, y_a, h, x, prefix, wo.astype(BF16), ln_mix[l],
                  w_ff_in[l].astype(BF16), w_ff_out[l].astype(BF16), ln_ffn[l], alpha,
                  final=l == depth - 1)
    return h
```

```python
import functools
import math

import jax
import jax.numpy as jnp
from jax import lax
from jax.experimental import pallas as pl
from jax.experimental.pallas import tpu as pltpu

F32 = jnp.float32
BF16 = jnp.bfloat16

HEAD_DIM = 64
N_META = 16
M_HEADS = 4
M_WIDTH = M_HEADS * HEAD_DIM
M_CHUNK = 64
M_NORM_EPS = 1e-6
R_HEADS = 4
R_WIDTH = R_HEADS * HEAD_DIM
R_CHUNK = 128
R_DECAY_RANK = 32
R_A_RANK = 32
R_GATE_RANK = 64
R_GN_EPS = 64e-5
A_HEADS = 8
A_KV_HEADS = 2
A_GROUP = A_HEADS // A_KV_HEADS
A_WIDTH = A_HEADS * HEAD_DIM
A_KV_WIDTH = A_KV_HEADS * HEAD_DIM
WINDOW = 128
A_BLOCK = 128
N_BUCKETS = 32
MAX_DISTANCE = 128
LN_EPS = 1e-5
NEG = -1e30

LANES = 128
FRONT_PAD = A_BLOCK - N_META
TOKEN_START = FRONT_PAD + N_META

PA_W = A_WIDTH + 2 * A_KV_WIDTH
PM_W = 4 * M_WIDTH + LANES
PR_W = 3 * R_WIDTH + R_DECAY_RANK + R_A_RANK + R_GATE_RANK
M_RAW = 4 * M_WIDTH + 2 * M_HEADS

VMEM_LIMIT = 56 * 1024 * 1024
MIX_ROWS = 640


def _row_tile(n):
    for t in (640, 512, 256, 128, 64):
        if n % t == 0:
            return t
    raise ValueError(f"row count {n} has no supported tile")


def _mix_rows(lp):
    return MIX_ROWS if lp % MIX_ROWS == 0 else A_BLOCK


def _cparams(sem):
    return pltpu.CompilerParams(dimension_semantics=sem, vmem_limit_bytes=VMEM_LIMIT)


def _bf(x):
    return x.astype(BF16)


def _mm(a, b):
    return lax.dot_general(_bf(a), _bf(b), (((1,), (0,)), ((), ())), preferred_element_type=F32)


def _mm_nt(a, b):
    return lax.dot_general(_bf(a), _bf(b), (((1,), (1,)), ((), ())), preferred_element_type=F32)


def _mm_tn(a, b):
    return lax.dot_general(_bf(a), _bf(b), (((0,), (0,)), ((), ())), preferred_element_type=F32)


def _iota2(shape, dim):
    return lax.broadcasted_iota(jnp.int32, shape, dim)


def _split3(x):
    hi = _bf(x).astype(F32)
    r1 = x - hi
    mid = _bf(r1).astype(F32)
    return hi, mid, r1 - mid


def _chunk_mask(n, chunk, lower):
    r_i = _iota2((n, n), 0)
    c_i = _iota2((n, n), 1)
    sh = chunk.bit_length() - 1
    same = jnp.right_shift(r_i, sh) == jnp.right_shift(c_i, sh)
    return same & (r_i >= c_i) if lower else same


def _cumsum_rows(tri, x):
    w = x.shape[1]
    res = _mm(tri, jnp.concatenate(_split3(x), axis=1))
    return res[:, :w] + res[:, w:2 * w] + res[:, 2 * w:]


def _cumsum_lanes(tri, x):
    r = x.shape[0]
    res = _mm_nt(jnp.concatenate(_split3(x), axis=0), tri)
    return res[:r] + res[r:2 * r] + res[2 * r:]


def _softplus(x):
    return jnp.maximum(x, 0.0) + jnp.log1p(jnp.exp(-jnp.abs(x)))


def _padded_tile_inputs(h, x, prefix, tm):
    if h is not None:
        return [h], [pl.BlockSpec((None, tm, h.shape[-1]), lambda i, j: (i, j, 0))]
    d = x.shape[-1]
    nb = tm // A_BLOCK
    specs = [pl.BlockSpec((A_BLOCK, d), lambda i, j: (0, 0))]
    specs += [pl.BlockSpec((None, A_BLOCK, d), lambda i, j, r=r: (i, jnp.maximum(j * nb - 1 + r, 0), 0))
              for r in range(nb)]
    return [prefix] + [x] * nb, specs


def _token_tile_inputs(a, tm):
    nb = tm // A_BLOCK
    first = TOKEN_START // A_BLOCK
    specs = [pl.BlockSpec((None, A_BLOCK, a.shape[-1]), lambda i, j, r=r: (i, first + j * nb + r, 0))
             for r in range(nb)]
    return [a] * nb, specs


def _read_tile(refs, from_x):
    if from_x:
        first = jnp.where(pl.program_id(1) == 0, refs[0][...], refs[1][...])
        return jnp.concatenate([first] + [r[...] for r in refs[2:]], axis=0)
    if len(refs) == 1:
        return refs[0][...]
    return jnp.concatenate([r[...] for r in refs], axis=0)


def _proj_kernel(*refs, n_in, from_x):
    w_ref, oa_ref, om_ref, or_ref = refs[n_in:]
    x = _read_tile(refs[:n_in], from_x)
    acc = jnp.dot(x.astype(BF16), w_ref[...], preferred_element_type=F32)
    oa_ref[...] = acc[:, :PA_W]
    om_ref[...] = acc[:, PA_W:PA_W + PM_W]
    or_ref[...] = acc[:, PA_W + PM_W:]


def _proj(h, x, prefix, w, lp):
    b = x.shape[0]
    d, n = w.shape
    tm = _row_tile(lp)
    arrays, specs = _padded_tile_inputs(h, x, prefix, tm)
    spec = lambda width: pl.BlockSpec((None, tm, width), lambda i, j: (i, j, 0))
    return pl.pallas_call(
        functools.partial(_proj_kernel, n_in=len(arrays), from_x=h is None),
        grid=(b, lp // tm),
        in_specs=specs + [pl.BlockSpec((d, n), lambda i, j: (0, 0), pipeline_mode=pl.Buffered(1))],
        out_specs=[spec(PA_W), spec(PM_W), spec(PR_W)],
        out_shape=[jax.ShapeDtypeStruct((b, lp, PA_W), F32),
                   jax.ShapeDtypeStruct((b, lp, PM_W), F32),
                   jax.ShapeDtypeStruct((b, lp, PR_W), F32)],
        compiler_params=_cparams(("arbitrary", "arbitrary")),
        name="in_proj",
    )(*arrays, w)


def _mlstm_stages(pm_ref, bias_ref, nw_ref, out_ref, cst_ref, m_ref):
    blk = pl.program_id(1)
    tb = pm_ref.shape[0]
    ch = M_CHUNK
    hd = HEAD_DIM
    nch = tb // ch
    npair = M_HEADS // 2
    assert 2 * hd == LANES and 2 * ch == LANES
    yield 8

    @pl.when(blk == 0)
    def _():
        cst_ref[...] = jnp.zeros_like(cst_ref)
        m_ref[...] = jnp.full_like(m_ref, NEG)

    row = blk * tb + _iota2((tb, 1), 0)
    is_pad = row < FRONT_PAD
    gb = pm_ref[:, 4 * M_WIDTH:] + bias_ref[...]
    lane = _iota2((tb, LANES), 1)
    li = jnp.where(is_pad, NEG, gb)
    lf = jnp.where(is_pad, 0.0, -_softplus(-gb))
    gcol = jnp.where(lane < M_HEADS, li, jnp.where(lane < 2 * M_HEADS, lf, 0.0))
    sel = (_iota2((8, LANES), 0) == _iota2((8, LANES), 1)).astype(F32)
    grow = _mm_nt(jnp.concatenate([sel] * 3, axis=1), jnp.concatenate(_split3(gcol), axis=1))
    brow = _cumsum_lanes(_chunk_mask(tb, ch, True).astype(F32), grow)
    r8 = _iota2((8, tb), 0)
    t8 = jnp.bitwise_and(_iota2((8, tb), 1), ch - 1)
    cm = grow - pltpu.roll(brow, M_HEADS, axis=0)
    sh = 1
    while sh < ch:
        cm = jnp.where(t8 >= sh, jnp.maximum(cm, pltpu.roll(cm, sh, axis=1)), cm)
        sh *= 2
    rows_lb = jnp.where(r8 < M_HEADS, grow, brow)
    x16 = jnp.concatenate([rows_lb, cm], axis=0)
    nrep = 3 * M_WIDTH
    e_sel = (_iota2((16, nrep), 0) == jnp.right_shift(_iota2((16, nrep), 1), hd.bit_length() - 1)
             ).astype(F32)
    rep = _mm_tn(jnp.concatenate(_split3(x16), axis=0), jnp.concatenate([e_sel] * 3, axis=0))
    rep_of = lambda qn, j, cs: rep[cs, qn * M_WIDTH + j * LANES:qn * M_WIDTH + (j + 1) * LANES]
    rows_up = pltpu.roll(rows_lb, ch, axis=1)
    rows_dn = pltpu.roll(rows_lb, tb - ch, axis=1)
    low1 = _iota2((1, LANES), 1) < hd

    def row_pair(base, c, j):
        ts = slice((c // 2) * LANES, (c // 2 + 1) * LANES)
        h0, h1 = base + 2 * j, base + 2 * j + 1
        if c % 2 == 0:
            return jnp.where(low1, rows_lb[h0:h0 + 1, ts], rows_up[h1:h1 + 1, ts])
        return jnp.where(low1, rows_dn[h0:h0 + 1, ts], rows_lb[h1:h1 + 1, ts])

    low = _iota2((ch, LANES), 1) < hd
    t_i = _iota2((ch, LANES), 0)
    causal2 = t_i >= jnp.bitwise_and(_iota2((ch, LANES), 1), ch - 1)
    r128 = _iota2((LANES, LANES), 0)
    c128 = _iota2((LANES, LANES), 1)
    ones_bd = ((r128 < hd) == (c128 < hd)).astype(F32)
    bd2 = jnp.concatenate([ones_bd, ones_bd], axis=1) > 0.5
    ones_t = jnp.ones((ch, LANES), F32)

    probs = [(c, j) for c in range(nch) for j in range(npair)]
    q_p, k_p, v_p = {}, {}, {}
    for c, j in probs:
        cs = slice(c * ch, (c + 1) * ch)
        ps = slice(j * LANES, (j + 1) * LANES)
        q_p[c, j] = pm_ref[cs, ps]
        k_p[c, j] = pm_ref[cs, M_WIDTH + ps.start:M_WIDTH + ps.stop] * (hd ** -0.5)
        v_p[c, j] = pm_ref[cs, 2 * M_WIDTH + ps.start:2 * M_WIDTH + ps.stop]
    bdiag = lambda a: jnp.concatenate([jnp.where(low, a, 0.0), jnp.where(low, 0.0, a)], axis=0)
    yield
    qk = {p: _mm_nt(q_p[p], bdiag(k_p[p])) for p in probs}
    yield

    dw_p, sint_p, emt_p, wa_p, decay_p = {}, {}, {}, {}, {}
    for j in range(npair):
        m_prev = m_ref[j][0:1, :]
        for c in range(nch):
            cs = slice(c * ch, (c + 1) * ch)
            li_c, b_c, cm_c = rep_of(0, j, cs), rep_of(1, j, cs), rep_of(2, j, cs)
            d_mat = jnp.where(causal2, b_c - row_pair(M_HEADS, c, j) + row_pair(0, c, j), NEG)
            inter = b_c + m_prev
            m_t = jnp.maximum(b_c + cm_c, inter)
            dw_p[c, j] = jnp.where(causal2, jnp.exp(d_mat - m_t), 0.0)
            sint_p[c, j] = jnp.exp(inter - m_t)
            emt_p[c, j] = jnp.exp(-m_t)
            g = b_c[ch - 1:ch, :]
            m_new = jnp.maximum(g + m_prev, g + cm_c[ch - 1:ch, :])
            wa_p[c, j] = jnp.exp(g - b_c + li_c - m_new)
            decay_p[c, j] = jnp.exp(g + m_prev - m_new)
            m_prev = m_new
        m_ref[j] = jnp.broadcast_to(m_prev, m_ref.shape[1:])

    yield
    upd = {p: jnp.where(bd2, _mm_tn(wa_p[p] * k_p[p], jnp.concatenate([v_p[p], ones_t], axis=1)), 0.0)
           for p in probs}
    yield
    st_p = {}
    for j in range(npair):
        st = cst_ref[j]
        for c in range(nch):
            st_p[c, j] = st
            dec = decay_p[c, j]
            st = jnp.concatenate([dec, dec], axis=1) * st + upd[c, j]
        cst_ref[j] = st
    yield
    inter_p = {p: _mm(q_p[p], st_p[p]) for p in probs}
    yield
    intra_p = {p: _mm(qk[p] * dw_p[p], jnp.concatenate([bdiag(v_p[p]), ones_bd], axis=1)) for p in probs}
    yield

    rows = []
    for c in range(nch):
        pairs = []
        for j in range(npair):
            p = (c, j)
            tot = intra_p[p] + jnp.concatenate([sint_p[p], sint_p[p]], axis=1) * inter_p[p]
            pairs.append(tot[:, :LANES] / jnp.maximum(jnp.abs(tot[:, LANES:]), emt_p[p]))
        rows.append(jnp.concatenate(pairs, axis=1))
    hh = jnp.concatenate(rows, axis=0)

    head_ones = _chunk_mask(M_WIDTH, hd, False).astype(BF16)
    mu = _mm(hh, head_ones) * (1.0 / hd)
    hc = hh - mu
    var = _mm(hc * hc, head_ones) * (1.0 / hd)
    o_pre = pm_ref[:, 3 * M_WIDTH:4 * M_WIDTH]
    out_ref[...] = hc * lax.rsqrt(var + M_NORM_EPS) * nw_ref[...] * jax.nn.sigmoid(o_pre)


def _swap_pairs(w):
    rest = w.shape[1:]
    return jnp.flip(w.reshape(R_HEADS // 2, 2, HEAD_DIM, *rest), axis=1).reshape(R_WIDTH, *rest)


def _rwkv_stages(pr_ref, prev_ref, mu_ref, w0_ref, w2_ref, a0_ref, a2_ref, g2_ref,
                 kk_ref, ka_ref, rk_ref, lnx_ref, out_ref, st_ref):
    blk = pl.program_id(1)
    tb = pr_ref.shape[0]
    ch = R_CHUNK
    rw = R_WIDTH
    hd = HEAD_DIM
    nch = tb // ch
    assert ch == LANES and 2 * hd == LANES
    n_levels = ch.bit_length() - 1
    yield 6 + n_levels + nch

    @pl.when(blk == 0)
    def _():
        st_ref[...] = jnp.zeros_like(st_ref)

    x = pr_ref[...]
    last = jnp.where(blk == 0, 0.0, prev_ref[7:8, :])
    prev = jnp.where(_iota2((tb, 1), 0) == 0, last, pltpu.roll(x, 1, axis=0))
    t = x + (prev - x) * mu_ref[...]
    r = t[:, 0:rw]
    k = t[:, rw:2 * rw]
    v = t[:, 2 * rw:3 * rw]
    o = 3 * rw
    w_lat = t[:, o:o + R_DECAY_RANK]
    a_lat = t[:, o + R_DECAY_RANK:o + R_DECAY_RANK + R_A_RANK]
    g_lat = t[:, o + R_DECAY_RANK + R_A_RANK:]

    w_log = -_softplus(-(w0_ref[...] + _mm(jnp.tanh(w_lat), w2_ref[...]))) - 0.5
    lw = -jnp.exp(w_log)
    a = jax.nn.sigmoid(a0_ref[...] + _mm(a_lat, a2_ref[...]))
    g = _mm(jax.nn.sigmoid(g_lat), g2_ref[...])

    head_ones = _chunk_mask(rw, hd, False).astype(BF16)
    sh = hd.bit_length() - 1
    swap_ones = ((jnp.right_shift(_iota2((rw, rw), 0), sh) ^ 1)
                 == jnp.right_shift(_iota2((rw, rw), 1), sh)).astype(BF16)
    kk = k * kk_ref[...]
    kk = kk / jnp.maximum(jnp.sqrt(_mm(kk * kk, head_ones)), 1e-12)
    k2 = k * (1.0 + (a - 1.0) * ka_ref[...])
    avec = -kk
    bvec = kk * a
    cum = _cumsum_rows(_chunk_mask(tb, ch, True).astype(F32), lw)

    r_i = _iota2((ch, ch), 0)
    c_i = _iota2((ch, ch), 1)
    strict = r_i > c_i
    eye = r_i == c_i
    incl2 = jnp.concatenate([r_i >= c_i] * 2, axis=1)
    low_cols = c_i < hd
    low_rows = r_i < hd

    probs = [(c, h) for c in range(nch) for h in range(R_HEADS)]
    at_p, rt_p, ac_p, rc_p, vo_p, kb_t, kb_p, wc_p = ({} for _ in range(8))
    for c in range(nch):
        cs = slice(c * ch, (c + 1) * ch)
        cum_c = cum[cs]
        mid = cum_c[ch // 2 - 1:ch // 2, :]
        cl = cum_c[ch - 1:ch, :]
        e_mid = jnp.exp(-mid)
        a_true = avec[cs] * jnp.exp(cum_c - lw[cs])
        r_true = r[cs] * jnp.exp(cum_c)
        a_cen = a_true * e_mid
        r_cen = r_true * e_mid
        inv = jnp.exp(mid - cum_c)
        rel = jnp.exp(cl - cum_c)
        kt_c, bt_c = k2[cs] * inv, bvec[cs] * inv
        kp_c, bp_c = k2[cs] * rel, bvec[cs] * rel
        wc_c = jnp.exp(cl)
        v_c = v[cs]
        for h in range(R_HEADS):
            ps = slice((h // 2) * LANES, (h // 2 + 1) * LANES)
            mine = low_cols if h % 2 == 0 else jnp.logical_not(low_cols)
            key = (c, h)
            at_p[key] = jnp.where(mine, a_true[:, ps], 0.0)
            rt_p[key] = jnp.where(mine, r_true[:, ps], 0.0)
            ac_p[key] = _bf(jnp.where(mine, a_cen[:, ps], 0.0))
            rc_p[key] = _bf(jnp.where(mine, r_cen[:, ps], 0.0))
            vo_p[key] = _bf(jnp.where(mine, 0.0, v_c[:, ps]))
            kb_t[key] = _bf(jnp.concatenate([kt_c[:, ps], bt_c[:, ps]], axis=0))
            kb_p[key] = _bf(jnp.concatenate([kp_c[:, ps], bp_c[:, ps]], axis=0))
            wc_p[key] = wc_c[:, ps]

    yield
    gram = {p: _mm_nt(jnp.concatenate([ac_p[p], rc_p[p]], axis=0), kb_t[p]) for p in probs}
    m_ak = {p: _bf(jnp.where(strict, gram[p][:ch, :ch], 0.0)) for p in probs}
    pw = {p: _bf(jnp.where(strict, gram[p][:ch, ch:], 0.0)) for p in probs}
    n_cat = {p: _bf(jnp.where(incl2, gram[p][ch:], 0.0)) for p in probs}
    yield
    z = {p: at_p[p] + _mm(m_ak[p], vo_p[p]) for p in probs}
    yield
    for lvl in range(n_levels):
        if lvl + 1 < n_levels:
            res = {p: _mm(pw[p], jnp.concatenate([_bf(z[p]), pw[p]], axis=1)) for p in probs}
            z = {p: z[p] + res[p][:, :LANES] for p in probs}
            pw = {p: _bf(res[p][:, LANES:]) for p in probs}
        else:
            z = {p: z[p] + _mm(pw[p], z[p]) for p in probs}
        yield
    rhs = {p: jnp.concatenate([vo_p[p], _bf(z[p])], axis=0) for p in probs}
    ry = {p: _mm(n_cat[p], rhs[p]) for p in probs}
    yield
    tg = {p: _mm_tn(kb_p[p], rhs[p]) for p in probs}
    yield

    sts = [st_ref[h] for h in range(R_HEADS)]
    y_chunks = []
    for c in range(nch):
        nxt, g_full = [], []
        for h in range(R_HEADS):
            p = (c, h)
            mine = low_cols if h % 2 == 0 else jnp.logical_not(low_cols)
            my_rows = low_rows if h % 2 == 0 else jnp.logical_not(low_rows)
            r_eff = rt_p[p] + jnp.where(mine, ry[p], 0.0)
            t_full = (jnp.where(my_rows & mine, tg[p], 0.0)
                      + jnp.where(eye & mine, wc_p[p], 0.0))
            g_full.append(jnp.where(my_rows & jnp.logical_not(mine), tg[p], 0.0))
            nxt.append(_mm(jnp.concatenate([r_eff, t_full], axis=0), sts[h]))
        ys = [jnp.where(low_cols if h % 2 else jnp.logical_not(low_cols), nxt[h][:ch] + ry[(c, h)], 0.0)
              for h in range(R_HEADS)]
        y_chunks.append(jnp.concatenate([ys[2 * j] + ys[2 * j + 1] for j in range(R_HEADS // 2)], axis=1))
        sts = [nxt[h][ch:] + g_full[h] for h in range(R_HEADS)]
        yield
    for h in range(R_HEADS):
        st_ref[h] = sts[h]
    y = jnp.concatenate(y_chunks, axis=0)

    inv_d = 1.0 / hd
    mu = _mm(y, head_ones) * inv_d
    yc = y - mu
    var = _mm(yc * yc, head_ones) * inv_d
    yn = yc * lax.rsqrt(var + R_GN_EPS) * lnx_ref[0:1, :] + lnx_ref[1:2, :]
    bonus = _mm(r * k2 * rk_ref[...], swap_ones) * v
    out_ref[...] = (yn + bonus) * g


def _t5_bucket(dist):
    max_exact = N_BUCKETS // 2
    d = jnp.maximum(dist, 1).astype(F32)
    large = max_exact + (jnp.log(d / max_exact) / math.log(MAX_DISTANCE / max_exact)
                         * (N_BUCKETS - max_exact)).astype(jnp.int32)
    large = jnp.minimum(large, N_BUCKETS - 1)
    return jnp.where(dist < max_exact, dist, large)


def _attn_tables(rel_bias):
    blk = A_BLOCK
    i = jnp.arange(blk)[:, None]
    j = jnp.arange(blk)[None, :]
    top = 3 * blk - 1
    d_rev = top - jnp.arange(4 * blk - 1)
    by_dist_rev = rel_bias.astype(F32)[:, _t5_bucket(jnp.maximum(d_rev, 0))]

    def bias(offset):
        rows = [by_dist_rev[:, top - offset - t:top - offset - t + blk] for t in range(blk)]
        return jnp.stack(rows, axis=1)

    neg = jnp.full((A_HEADS, blk, blk), NEG, F32)
    is_meta = j >= FRONT_PAD
    cur_d = i - j
    prev_d = blk + i - j
    cur = jnp.where(cur_d >= 0, bias(0), NEG)
    prev = jnp.where(prev_d < WINDOW, bias(blk), NEG)
    meta0 = jnp.where(is_meta & (cur_d >= 0), bias(0), NEG)
    meta1 = jnp.where(is_meta, bias(blk), NEG)
    meta2 = jnp.where(is_meta, bias(2 * blk), NEG)
    case0 = jnp.concatenate([meta0, neg, neg], axis=-1)
    case1 = jnp.concatenate([meta1, neg, cur], axis=-1)
    case2 = jnp.concatenate([meta2, prev, cur], axis=-1)
    tab = jnp.stack([case0, case1, case2], axis=0)
    tab = tab.reshape(3, A_KV_HEADS, A_GROUP, blk, 3 * blk)
    return tab.transpose(0, 1, 4, 2, 3).reshape(3, A_KV_HEADS, 3 * blk, A_GROUP * blk)


assert A_KV_HEADS * HEAD_DIM == LANES
_PAIRED_HEADS = tuple(kv * A_GROUP + g for g in range(A_GROUP) for kv in range(A_KV_HEADS))


def _pair_heads(w):
    rest = w.shape[1:]
    w = w.reshape(A_KV_HEADS, A_GROUP, HEAD_DIM, *rest)
    return jnp.swapaxes(w, 0, 1).reshape(A_WIDTH, *rest)


def _swa_stages(q_ref, kc_ref, kp_ref, km_ref, vc_ref, vp_ref, vm_ref, tab_ref, sink_ref, out_ref):
    scale = HEAD_DIM ** -0.5
    blk = A_BLOCK
    hd = HEAD_DIM
    nb = q_ref.shape[0] // blk
    step = pl.program_id(1)
    kvs = range(A_KV_HEADS)
    upper = _iota2((blk, LANES), 1) >= hd
    yield 4

    v_t = jnp.concatenate([vm_ref[...], vp_ref[...], vc_ref[...]], axis=0).T
    ones = jnp.ones((hd, 3 * blk), F32)

    probs = [(i, kv) for i in range(nb) for kv in kvs]
    s_t = {}
    for i in range(nb):
        rows = slice(i * blk, (i + 1) * blk)
        prev = kp_ref[...] if i == 0 else kc_ref[(i - 1) * blk:i * blk, :]
        keys = jnp.concatenate([km_ref[...], prev, kc_ref[rows, :]], axis=0)
        for kv in kvs:
            mine = upper if kv == 1 else jnp.logical_not(upper)
            q = jnp.concatenate([jnp.where(mine, q_ref[rows, g * LANES:(g + 1) * LANES], 0.0)
                                 for g in range(A_GROUP)], axis=0) * scale
            s_t[i, kv] = _mm_nt(keys, q)
    yield
    p_t, esink = {}, {}
    for i, kv in probs:
        case = 2 if i >= 2 else jnp.minimum(step * nb + i, 2)
        sk = s_t[i, kv] + tab_ref[case, kv]
        sink = sink_ref[kv][0:1, :]
        m = jnp.maximum(jnp.max(sk, axis=0, keepdims=True), sink)
        p_t[i, kv] = _bf(jnp.exp(sk - m))
        esink[i, kv] = jnp.exp(sink - m)
    yield
    o_t = {}
    for i, kv in probs:
        vr = slice(kv * hd, (kv + 1) * hd)
        prev = v_t[vr, blk:2 * blk] if i == 0 else v_t[vr, (i + 1) * blk:(i + 2) * blk]
        vals = jnp.concatenate([v_t[vr, 0:blk], prev, v_t[vr, (i + 2) * blk:(i + 3) * blk]], axis=1)
        r = _mm(jnp.concatenate([vals, ones], axis=0), p_t[i, kv])
        o_t[i, kv] = r[:hd] / (r[hd:] + esink[i, kv])
    yield
    for i in range(nb):
        tiles = [jnp.concatenate([o_t[i, kv][:, g * blk:(g + 1) * blk] for kv in kvs], axis=0).T
                 for g in range(A_GROUP)]
        out_ref[i * blk:(i + 1) * blk, :] = jnp.concatenate(tiles, axis=1)


def _interleave(gens):
    total = [next(g) for g in gens]
    done = [0] * len(gens)
    live = set(range(len(gens)))
    while live:
        i = min(live, key=lambda n: ((done[n] + 1) / total[n], n))
        try:
            next(gens[i])
            done[i] += 1
        except StopIteration:
            live.discard(i)


def _mixers_kernel(*refs, n_m, n_r, n_s):
    m_in, r_in, s_in = refs[:n_m], refs[n_m:n_m + n_r], refs[n_m + n_r:n_m + n_r + n_s]
    ym_ref, yr_ref, ya_ref, cst_ref, m_ref, st_ref = refs[n_m + n_r + n_s:]
    _interleave([_rwkv_stages(*r_in, yr_ref, st_ref),
                 _mlstm_stages(*m_in, ym_ref, cst_ref, m_ref),
                 _swa_stages(*s_in, ya_ref)])


def _mixers(pa, pm, pr, tables, sinks, gate_bias, norm_w, mu, w0, w2, a0, a2, g2, k_k, k_a, r_k, ln_x):
    b, lp, _ = pa.shape
    blk = A_BLOCK
    tb = _mix_rows(lp)
    nb = tb // blk
    row = lambda a: a.reshape(1, -1)
    full = lambda a: pl.BlockSpec(a.shape, lambda i, j: (0,) * a.ndim)
    tile = lambda width: pl.BlockSpec((None, tb, width), lambda i, j: (i, j, 0))

    bias = jnp.zeros((1, LANES), F32).at[0, :2 * M_HEADS].set(gate_bias.reshape(-1))
    m_in = [pm, bias, row(norm_w)]
    m_specs = [tile(PM_W), full(bias), full(m_in[2])]

    r_params = [row(mu), row(w0), w2, row(a0), a2, _swap_pairs(g2.T).T, row(k_k), row(k_a), row(r_k),
                _swap_pairs(ln_x.T).T]
    r_in = [pr, pr] + r_params
    r_specs = [tile(PR_W),
               pl.BlockSpec((None, 8, PR_W), lambda i, j: (i, jnp.maximum(j * (tb // 8) - 1, 0), 0))]
    r_specs += [full(a) for a in r_params]

    kcol = A_WIDTH // A_KV_WIDTH
    sink = jnp.broadcast_to(jnp.repeat(sinks.astype(F32).reshape(A_KV_HEADS, 1, A_GROUP), blk, axis=2),
                            (A_KV_HEADS, 8, A_GROUP * blk))
    cur_spec = lambda col: pl.BlockSpec((None, tb, A_KV_WIDTH), lambda i, j: (i, j, col))
    prev_spec = lambda col: pl.BlockSpec((None, blk, A_KV_WIDTH),
                                         lambda i, j: (i, jnp.maximum(j * nb - 1, 0), col))
    meta_spec = lambda col: pl.BlockSpec((None, blk, A_KV_WIDTH), lambda i, j: (i, 0, col))
    s_in = [pa] * 7 + [tables, sink]
    s_specs = [tile(A_WIDTH), cur_spec(kcol), prev_spec(kcol), meta_spec(kcol),
               cur_spec(kcol + 1), prev_spec(kcol + 1), meta_spec(kcol + 1),
               pl.BlockSpec(tables.shape, lambda i, j: (0, 0, 0, 0), pipeline_mode=pl.Buffered(1)),
               full(sink)]

    return pl.pallas_call(
        functools.partial(_mixers_kernel, n_m=len(m_in), n_r=len(r_in), n_s=len(s_in)),
        grid=(b, lp // tb),
        in_specs=m_specs + r_specs + s_specs,
        out_specs=[tile(M_WIDTH), tile(R_WIDTH), tile(A_WIDTH)],
        out_shape=[jax.ShapeDtypeStruct((b, lp, M_WIDTH), F32),
                   jax.ShapeDtypeStruct((b, lp, R_WIDTH), F32),
                   jax.ShapeDtypeStruct((b, lp, A_WIDTH), F32)],
        scratch_shapes=[pltpu.VMEM((M_HEADS // 2, LANES, 2 * LANES), F32),
                        pltpu.VMEM((M_HEADS // 2, 8, LANES), F32),
                        pltpu.VMEM((R_HEADS, LANES, LANES), F32)],
        compiler_params=_cparams(("arbitrary", "arbitrary")),
        name="mixers",
    )(*m_in, *r_in, *s_in)


def _layer_norm_rows(z, ln_ref):
    mu = jnp.mean(z, axis=-1, keepdims=True)
    zc = z - mu
    var = jnp.mean(zc * zc, axis=-1, keepdims=True)
    return zc * lax.rsqrt(var + LN_EPS) * ln_ref[0:1, :] + ln_ref[1:2, :]


FF_CHUNK = 256


def _post_kernel(*refs, n_each, from_x, final, alpha, d_ff):
    groups, pos = [], 0
    for n in n_each:
        groups.append(refs[pos:pos + n])
        pos += n
    w_ref, lnm_ref, wi_ref, wo_ref, lnf_ref, out_ref = refs[pos:]
    ym, yr, ya = (_read_tile(g, False) for g in groups[:3])
    h = _read_tile(groups[3], from_x)
    tm = h.shape[0]
    mix = jnp.dot(ym.astype(BF16), w_ref[0:M_WIDTH, :], preferred_element_type=F32)
    mix += jnp.dot(yr.astype(BF16), w_ref[M_WIDTH:M_WIDTH + R_WIDTH, :], preferred_element_type=F32)
    mix += jnp.dot(ya.astype(BF16), w_ref[M_WIDTH + R_WIDTH:, :], preferred_element_type=F32)
    x = _layer_norm_rows(alpha * h + mix, lnm_ref)
    if not final:
        is_pad = (pl.program_id(1) * tm + _iota2((tm, 1), 0)) < FRONT_PAD
        x = jnp.where(is_pad, 0.0, x)
    xb = x.astype(BF16)
    acc = alpha * x
    for j in range(d_ff // FF_CHUNK):
        cs = slice(j * FF_CHUNK, (j + 1) * FF_CHUNK)
        us = slice(d_ff + j * FF_CHUNK, d_ff + (j + 1) * FF_CHUNK)
        gate = jnp.dot(xb, wi_ref[:, cs], preferred_element_type=F32)
        up = jnp.dot(xb, wi_ref[:, us], preferred_element_type=F32)
        act = (gate * jax.nn.sigmoid(gate) * up).astype(BF16)
        acc += jnp.dot(act, wo_ref[cs, :], preferred_element_type=F32)
    y = _layer_norm_rows(acc, lnf_ref)
    out_ref[...] = y if final else jnp.where(is_pad, 0.0, y)


def _post(ym, yr, ya, h, x, prefix, w, ln_mix, wi, wo, ln_ffn, alpha, final):
    b, lp, _ = ym.shape
    seq, d = x.shape[1:]
    d_ff = wo.shape[0]
    assert d_ff % FF_CHUNK == 0
    from_x = h is None
    if final:
        tm = _row_tile(seq)
        groups = [_token_tile_inputs(a, tm) for a in (ym, yr, ya)]
        groups.append(([x], [pl.BlockSpec((None, tm, d), lambda i, j: (i, j, 0))]) if from_x
                      else _token_tile_inputs(h, tm))
        rows_out = seq
    else:
        tm = _row_tile(lp)
        groups = [_padded_tile_inputs(a, None, None, tm) for a in (ym, yr, ya)]
        groups.append(_padded_tile_inputs(h, x, prefix, tm))
        rows_out = lp
    arrays = [a for g in groups for a in g[0]]
    specs = [s for g in groups for s in g[1]]
    const = lambda a: pl.BlockSpec(a.shape, lambda i, j: (0, 0), pipeline_mode=pl.Buffered(1))
    ln_spec = pl.BlockSpec((2, d), lambda i, j: (0, 0))
    return pl.pallas_call(
        functools.partial(_post_kernel, n_each=tuple(len(g[0]) for g in groups),
                          from_x=from_x and not final, final=final, alpha=alpha, d_ff=d_ff),
        grid=(b, rows_out // tm),
        in_specs=specs + [const(w), ln_spec, const(wi), const(wo), ln_spec],
        out_specs=pl.BlockSpec((None, tm, d), lambda i, j: (i, j, 0)),
        out_shape=jax.ShapeDtypeStruct((b, rows_out, d), F32),
        compiler_params=_cparams(("arbitrary", "arbitrary")),
        name="out_proj_ffn",
    )(*arrays, w, ln_mix, wi, wo, ln_ffn)


def kernel(x, meta_tokens, rel_bias, w_in, m_gate_bias, m_norm_w, r_mu_rkv, r_mu_w, r_mu_a, r_mu_g, r_w0, r_w2, r_a0, r_a2, r_g2, r_k_k, r_k_a, r_r_k, r_ln_x, a_sinks, w_out, ln_mix, w_ff_in, w_ff_out, ln_ffn):
    b, seq, d = x.shape
    depth = w_in.shape[0]
    alpha = (2 * depth) ** 0.25
    assert (TOKEN_START + seq) % A_BLOCK == 0

    lp = TOKEN_START + seq
    prefix = jnp.concatenate([jnp.zeros((FRONT_PAD, d), x.dtype), meta_tokens.astype(x.dtype)], axis=0)
    tables = _attn_tables(rel_bias)
    a_end = PA_W
    m_end = a_end + M_RAW

    h = None
    for l in range(depth):
        wl = w_in[l]
        w_q = _pair_heads(wl[:, :A_WIDTH].T).T
        rv0 = m_end + 2 * R_WIDTH
        w_rv = _swap_pairs(wl[:, rv0:rv0 + R_WIDTH].T).T
        w_pad = jnp.concatenate(
            [w_q, wl[:, A_WIDTH:m_end], jnp.zeros((d, PM_W - M_RAW), wl.dtype), wl[:, m_end:rv0], w_rv,
             wl[:, rv0 + R_WIDTH:]], axis=1).astype(BF16)
        wo = w_out[l]
        wo = jnp.concatenate([wo[:M_WIDTH], _swap_pairs(wo[M_WIDTH:M_WIDTH + R_WIDTH]),
                              _pair_heads(wo[M_WIDTH + R_WIDTH:])], axis=0)
        pa, pm, pr = _proj(h, x, prefix, w_pad, lp)
        mu = jnp.concatenate([r_mu_rkv[l, 0], r_mu_rkv[l, 1], _swap_pairs(r_mu_rkv[l, 2]),
                              r_mu_w[l], r_mu_a[l], r_mu_g[l]])
        y_m, y_r, y_a = _mixers(pa, pm, pr, tables, a_sinks[l], m_gate_bias[l], m_norm_w[l], mu,
                                r_w0[l], r_w2[l], r_a0[l], r_a2[l], r_g2[l], r_k_k[l], r_k_a[l],
                                r_r_k[l], r_ln_x[l])
        h = _post(y_m, y_r, y_a, h, x, prefix, wo.astype(BF16), ln_mix[l],
                  w_ff_in[l].astype(BF16), w_ff_out[l].astype(BF16), ln_ffn[l], alpha,
                  final=l == depth - 1)
    return h
```

```python
import functools
import math

import jax
import jax.numpy as jnp
from jax import lax
from jax.experimental import pallas as pl
from jax.experimental.pallas import tpu as pltpu

F32 = jnp.float32
BF16 = jnp.bfloat16

HEAD_DIM = 64
N_META = 16
M_HEADS = 4
M_WIDTH = M_HEADS * HEAD_DIM
M_CHUNK = 64
M_NORM_EPS = 1e-6
R_HEADS = 4
R_WIDTH = R_HEADS * HEAD_DIM
R_CHUNK = 128
R_DECAY_RANK = 32
R_A_RANK = 32
R_GATE_RANK = 64
R_GN_EPS = 64e-5
A_HEADS = 8
A_KV_HEADS = 2
A_GROUP = A_HEADS // A_KV_HEADS
A_WIDTH = A_HEADS * HEAD_DIM
A_KV_WIDTH = A_KV_HEADS * HEAD_DIM
WINDOW = 128
A_BLOCK = 128
N_BUCKETS = 32
MAX_DISTANCE = 128
LN_EPS = 1e-5
NEG = -1e30

LANES = 128
FRONT_PAD = A_BLOCK - N_META
TOKEN_START = FRONT_PAD + N_META

PA_W = A_WIDTH + 2 * A_KV_WIDTH
PM_W = 4 * M_WIDTH + LANES
PR_W = 3 * R_WIDTH + R_DECAY_RANK + R_A_RANK + R_GATE_RANK
M_RAW = 4 * M_WIDTH + 2 * M_HEADS

VMEM_LIMIT = 56 * 1024 * 1024
MIX_ROWS = 640


def _row_tile(n):
    for t in (640, 512, 256, 128, 64):
        if n % t == 0:
            return t
    raise ValueError(f"row count {n} has no supported tile")


def _mix_rows(lp):
    return MIX_ROWS if lp % MIX_ROWS == 0 else A_BLOCK


def _cparams(sem):
    return pltpu.CompilerParams(dimension_semantics=sem, vmem_limit_bytes=VMEM_LIMIT)


def _bf(x):
    return x.astype(BF16)


def _mm(a, b):
    return lax.dot_general(_bf(a), _bf(b), (((1,), (0,)), ((), ())), preferred_element_type=F32)


def _mm_nt(a, b):
    return lax.dot_general(_bf(a), _bf(b), (((1,), (1,)), ((), ())), preferred_element_type=F32)


def _mm_tn(a, b):
    return lax.dot_general(_bf(a), _bf(b), (((0,), (0,)), ((), ())), preferred_element_type=F32)


def _iota2(shape, dim):
    return lax.broadcasted_iota(jnp.int32, shape, dim)


def _split3(x):
    hi = _bf(x).astype(F32)
    r1 = x - hi
    mid = _bf(r1).astype(F32)
    return hi, mid, r1 - mid


def _chunk_mask(n, chunk, lower):
    r_i = _iota2((n, n), 0)
    c_i = _iota2((n, n), 1)
    sh = chunk.bit_length() - 1
    same = jnp.right_shift(r_i, sh) == jnp.right_shift(c_i, sh)
    return same & (r_i >= c_i) if lower else same


def _cumsum_rows(tri, x):
    w = x.shape[1]
    res = _mm(tri, jnp.concatenate(_split3(x), axis=1))
    return res[:, :w] + res[:, w:2 * w] + res[:, 2 * w:]


def _cumsum_lanes(tri, x):
    r = x.shape[0]
    res = _mm_nt(jnp.concatenate(_split3(x), axis=0), tri)
    return res[:r] + res[r:2 * r] + res[2 * r:]


def _softplus(x):
    return jnp.maximum(x, 0.0) + jnp.log1p(jnp.exp(-jnp.abs(x)))


def _padded_tile_inputs(h, x, prefix, tm):
    if h is not None:
        return [h], [pl.BlockSpec((None, tm, h.shape[-1]), lambda i, j: (i, j, 0))]
    d = x.shape[-1]
    nb = tm // A_BLOCK
    specs = [pl.BlockSpec((A_BLOCK, d), lambda i, j: (0, 0))]
    specs += [pl.BlockSpec((None, A_BLOCK, d), lambda i, j, r=r: (i, jnp.maximum(j * nb - 1 + r, 0), 0))
              for r in range(nb)]
    return [prefix] + [x] * nb, specs


def _token_tile_inputs(a, tm):
    nb = tm // A_BLOCK
    first = TOKEN_START // A_BLOCK
    specs = [pl.BlockSpec((None, A_BLOCK, a.shape[-1]), lambda i, j, r=r: (i, first + j * nb + r, 0))
             for r in range(nb)]
    return [a] * nb, specs


def _read_tile(refs, from_x):
    if from_x:
        first = jnp.where(pl.program_id(1) == 0, refs[0][...], refs[1][...])
        return jnp.concatenate([first] + [r[...] for r in refs[2:]], axis=0)
    if len(refs) == 1:
        return refs[0][...]
    return jnp.concatenate([r[...] for r in refs], axis=0)


def _proj_kernel(*refs, n_in, from_x):
    w_ref, oa_ref, om_ref, or_ref = refs[n_in:]
    x = _read_tile(refs[:n_in], from_x)
    acc = jnp.dot(x.astype(BF16), w_ref[...], preferred_element_type=F32)
    oa_ref[...] = acc[:, :PA_W]
    om_ref[...] = acc[:, PA_W:PA_W + PM_W]
    or_ref[...] = acc[:, PA_W + PM_W:]


def _proj(h, x, prefix, w, lp):
    b = x.shape[0]
    d, n = w.shape
    tm = _row_tile(lp)
    arrays, specs = _padded_tile_inputs(h, x, prefix, tm)
    spec = lambda width: pl.BlockSpec((None, tm, width), lambda i, j: (i, j, 0))
    return pl.pallas_call(
        functools.partial(_proj_kernel, n_in=len(arrays), from_x=h is None),
        grid=(b, lp // tm),
        in_specs=specs + [pl.BlockSpec((d, n), lambda i, j: (0, 0), pipeline_mode=pl.Buffered(1))],
        out_specs=[spec(PA_W), spec(PM_W), spec(PR_W)],
        out_shape=[jax.ShapeDtypeStruct((b, lp, PA_W), F32),
                   jax.ShapeDtypeStruct((b, lp, PM_W), F32),
                   jax.ShapeDtypeStruct((b, lp, PR_W), F32)],
        compiler_params=_cparams(("arbitrary", "arbitrary")),
        name="in_proj",
    )(*arrays, w)


def _mlstm_stages(pm_ref, bias_ref, nw_ref, out_ref, cst_ref, m_ref):
    blk = pl.program_id(1)
    tb = pm_ref.shape[0]
    ch = M_CHUNK
    hd = HEAD_DIM
    nch = tb // ch
    npair = M_HEADS // 2
    assert 2 * hd == LANES and 2 * ch == LANES
    yield 8

    @pl.when(blk == 0)
    def _():
        cst_ref[...] = jnp.zeros_like(cst_ref)
        m_ref[...] = jnp.full_like(m_ref, NEG)

    row = blk * tb + _iota2((tb, 1), 0)
    is_pad = row < FRONT_PAD
    gb = pm_ref[:, 4 * M_WIDTH:] + bias_ref[...]
    lane = _iota2((tb, LANES), 1)
    li = jnp.where(is_pad, NEG, gb)
    lf = jnp.where(is_pad, 0.0, -_softplus(-gb))
    gcol = jnp.where(lane < M_HEADS, li, jnp.where(lane < 2 * M_HEADS, lf, 0.0))
    sel = (_iota2((8, LANES), 0) == _iota2((8, LANES), 1)).astype(F32)
    grow = _mm_nt(jnp.concatenate([sel] * 3, axis=1), jnp.concatenate(_split3(gcol), axis=1))
    brow = _cumsum_lanes(_chunk_mask(tb, ch, True).astype(F32), grow)
    r8 = _iota2((8, tb), 0)
    t8 = jnp.bitwise_and(_iota2((8, tb), 1), ch - 1)
    cm = grow - pltpu.roll(brow, M_HEADS, axis=0)
    sh = 1
    while sh < ch:
        cm = jnp.where(t8 >= sh, jnp.maximum(cm, pltpu.roll(cm, sh, axis=1)), cm)
        sh *= 2
    rows_lb = jnp.where(r8 < M_HEADS, grow, brow)
    x16 = jnp.concatenate([rows_lb, cm], axis=0)
    nrep = 3 * M_WIDTH
    e_sel = (_iota2((16, nrep), 0) == jnp.right_shift(_iota2((16, nrep), 1), hd.bit_length() - 1)
             ).astype(F32)
    rep = _mm_tn(jnp.concatenate(_split3(x16), axis=0), jnp.concatenate([e_sel] * 3, axis=0))
    rep_of = lambda qn, j, cs: rep[cs, qn * M_WIDTH + j * LANES:qn * M_WIDTH + (j + 1) * LANES]
    rows_up = pltpu.roll(rows_lb, ch, axis=1)
    rows_dn = pltpu.roll(rows_lb, tb - ch, axis=1)
    low1 = _iota2((1, LANES), 1) < hd

    def row_pair(base, c, j):
        ts = slice((c // 2) * LANES, (c // 2 + 1) * LANES)
        h0, h1 = base + 2 * j, base + 2 * j + 1
        if c % 2 == 0:
            return jnp.where(low1, rows_lb[h0:h0 + 1, ts], rows_up[h1:h1 + 1, ts])
        return jnp.where(low1, rows_dn[h0:h0 + 1, ts], rows_lb[h1:h1 + 1, ts])

    low = _iota2((ch, LANES), 1) < hd
    t_i = _iota2((ch, LANES), 0)
    causal2 = t_i >= jnp.bitwise_and(_iota2((ch, LANES), 1), ch - 1)
    r128 = _iota2((LANES, LANES), 0)
    c128 = _iota2((LANES, LANES), 1)
    ones_bd = ((r128 < hd) == (c128 < hd)).astype(F32)
    bd2 = jnp.concatenate([ones_bd, ones_bd], axis=1) > 0.5
    ones_t = jnp.ones((ch, LANES), F32)

    probs = [(c, j) for c in range(nch) for j in range(npair)]
    q_p, k_p, v_p = {}, {}, {}
    for c, j in probs:
        cs = slice(c * ch, (c + 1) * ch)
        ps = slice(j * LANES, (j + 1) * LANES)
        q_p[c, j] = pm_ref[cs, ps]
        k_p[c, j] = pm_ref[cs, M_WIDTH + ps.start:M_WIDTH + ps.stop] * (hd ** -0.5)
        v_p[c, j] = pm_ref[cs, 2 * M_WIDTH + ps.start:2 * M_WIDTH + ps.stop]
    bdiag = lambda a: jnp.concatenate([jnp.where(low, a, 0.0), jnp.where(low, 0.0, a)], axis=0)
    yield
    qk = {p: _mm_nt(q_p[p], bdiag(k_p[p])) for p in probs}
    yield

    dw_p, sint_p, emt_p, wa_p, decay_p = {}, {}, {}, {}, {}
    for j in range(npair):
        m_prev = m_ref[j][0:1, :]
        for c in range(nch):
            cs = slice(c * ch, (c + 1) * ch)
            li_c, b_c, cm_c = rep_of(0, j, cs), rep_of(1, j, cs), rep_of(2, j, cs)
            d_mat = jnp.where(causal2, b_c - row_pair(M_HEADS, c, j) + row_pair(0, c, j), NEG)
            inter = b_c + m_prev
            m_t = jnp.maximum(b_c + cm_c, inter)
            dw_p[c, j] = jnp.where(causal2, jnp.exp(d_mat - m_t), 0.0)
            sint_p[c, j] = jnp.exp(inter - m_t)
            emt_p[c, j] = jnp.exp(-m_t)
            g = b_c[ch - 1:ch, :]
            m_new = jnp.maximum(g + m_prev, g + cm_c[ch - 1:ch, :])
            wa_p[c, j] = jnp.exp(g - b_c + li_c - m_new)
            decay_p[c, j] = jnp.exp(g + m_prev - m_new)
            m_prev = m_new
        m_ref[j] = jnp.broadcast_to(m_prev, m_ref.shape[1:])

    yield
    upd = {p: jnp.where(bd2, _mm_tn(wa_p[p] * k_p[p], jnp.concatenate([v_p[p], ones_t], axis=1)), 0.0)
           for p in probs}
    yield
    st_p = {}
    for j in range(npair):
        st = cst_ref[j]
        for c in range(nch):
            st_p[c, j] = st
            dec = decay_p[c, j]
            st = jnp.concatenate([dec, dec], axis=1) * st + upd[c, j]
        cst_ref[j] = st
    yield
    inter_p = {p: _mm(q_p[p], st_p[p]) for p in probs}
    yield
    intra_p = {p: _mm(qk[p] * dw_p[p], jnp.concatenate([bdiag(v_p[p]), ones_bd], axis=1)) for p in probs}
    yield

    rows = []
    for c in range(nch):
        pairs = []
        for j in range(npair):
            p = (c, j)
            tot = intra_p[p] + jnp.concatenate([sint_p[p], sint_p[p]], axis=1) * inter_p[p]
            pairs.append(tot[:, :LANES] / jnp.maximum(jnp.abs(tot[:, LANES:]), emt_p[p]))
        rows.append(jnp.concatenate(pairs, axis=1))
    hh = jnp.concatenate(rows, axis=0)

    head_ones = _chunk_mask(M_WIDTH, hd, False).astype(BF16)
    mu = _mm(hh, head_ones) * (1.0 / hd)
    hc = hh - mu
    var = _mm(hc * hc, head_ones) * (1.0 / hd)
    o_pre = pm_ref[:, 3 * M_WIDTH:4 * M_WIDTH]
    out_ref[...] = hc * lax.rsqrt(var + M_NORM_EPS) * nw_ref[...] * jax.nn.sigmoid(o_pre)


def _swap_pairs(w, axis=0):
    split = w.shape[:axis] + (R_HEADS // 2, 2, HEAD_DIM) + w.shape[axis + 1:]
    return jnp.flip(w.reshape(split), axis=axis + 1).reshape(w.shape)


def _rwkv_stages(pr_ref, prev_ref, mu_ref, w0_ref, w2_ref, a0_ref, a2_ref, g2_ref,
                 kk_ref, ka_ref, rk_ref, lnx_ref, out_ref, st_ref):
    blk = pl.program_id(1)
    tb = pr_ref.shape[0]
    ch = R_CHUNK
    rw = R_WIDTH
    hd = HEAD_DIM
    nch = tb // ch
    assert ch == LANES and 2 * hd == LANES
    n_levels = ch.bit_length() - 1
    yield 6 + n_levels + nch

    @pl.when(blk == 0)
    def _():
        st_ref[...] = jnp.zeros_like(st_ref)

    x = pr_ref[...]
    last = jnp.where(blk == 0, 0.0, prev_ref[7:8, :])
    prev = jnp.where(_iota2((tb, 1), 0) == 0, last, pltpu.roll(x, 1, axis=0))
    t = x + (prev - x) * mu_ref[...]
    r = t[:, 0:rw]
    k = t[:, rw:2 * rw]
    v = t[:, 2 * rw:3 * rw]
    o = 3 * rw
    w_lat = t[:, o:o + R_DECAY_RANK]
    a_lat = t[:, o + R_DECAY_RANK:o + R_DECAY_RANK + R_A_RANK]
    g_lat = t[:, o + R_DECAY_RANK + R_A_RANK:]

    w_log = -_softplus(-(w0_ref[...] + _mm(jnp.tanh(w_lat), w2_ref[...]))) - 0.5
    lw = -jnp.exp(w_log)
    a = jax.nn.sigmoid(a0_ref[...] + _mm(a_lat, a2_ref[...]))
    g = _mm(jax.nn.sigmoid(g_lat), g2_ref[...])

    head_ones = _chunk_mask(rw, hd, False).astype(BF16)
    sh = hd.bit_length() - 1
    swap_ones = ((jnp.right_shift(_iota2((rw, rw), 0), sh) ^ 1)
                 == jnp.right_shift(_iota2((rw, rw), 1), sh)).astype(BF16)
    kk = k * kk_ref[...]
    kk = kk / jnp.maximum(jnp.sqrt(_mm(kk * kk, head_ones)), 1e-12)
    k2 = k * (1.0 + (a - 1.0) * ka_ref[...])
    avec = -kk
    bvec = kk * a
    cum = _cumsum_rows(_chunk_mask(tb, ch, True).astype(F32), lw)

    r_i = _iota2((ch, ch), 0)
    c_i = _iota2((ch, ch), 1)
    strict = r_i > c_i
    eye = r_i == c_i
    incl2 = jnp.concatenate([r_i >= c_i] * 2, axis=1)
    low_cols = c_i < hd
    low_rows = r_i < hd

    probs = [(c, h) for c in range(nch) for h in range(R_HEADS)]
    at_p, rt_p, ac_p, rc_p, vo_p, kb_t, kb_p, wc_p = ({} for _ in range(8))
    for c in range(nch):
        cs = slice(c * ch, (c + 1) * ch)
        cum_c = cum[cs]
        mid = cum_c[ch // 2 - 1:ch // 2, :]
        cl = cum_c[ch - 1:ch, :]
        e_mid = jnp.exp(-mid)
        a_true = avec[cs] * jnp.exp(cum_c - lw[cs])
        r_true = r[cs] * jnp.exp(cum_c)
        a_cen = a_true * e_mid
        r_cen = r_true * e_mid
        inv = jnp.exp(mid - cum_c)
        rel = jnp.exp(cl - cum_c)
        kt_c, bt_c = k2[cs] * inv, bvec[cs] * inv
        kp_c, bp_c = k2[cs] * rel, bvec[cs] * rel
        wc_c = jnp.exp(cl)
        v_c = v[cs]
        for h in range(R_HEADS):
            ps = slice((h // 2) * LANES, (h // 2 + 1) * LANES)
            mine = low_cols if h % 2 == 0 else jnp.logical_not(low_cols)
            key = (c, h)
            at_p[key] = jnp.where(mine, a_true[:, ps], 0.0)
            rt_p[key] = jnp.where(mine, r_true[:, ps], 0.0)
            ac_p[key] = _bf(jnp.where(mine, a_cen[:, ps], 0.0))
            rc_p[key] = _bf(jnp.where(mine, r_cen[:, ps], 0.0))
            vo_p[key] = _bf(jnp.where(mine, 0.0, v_c[:, ps]))
            kb_t[key] = _bf(jnp.concatenate([kt_c[:, ps], bt_c[:, ps]], axis=0))
            kb_p[key] = _bf(jnp.concatenate([kp_c[:, ps], bp_c[:, ps]], axis=0))
            wc_p[key] = wc_c[:, ps]

    yield
    gram = {p: _mm_nt(jnp.concatenate([ac_p[p], rc_p[p]], axis=0), kb_t[p]) for p in probs}
    m_ak = {p: _bf(jnp.where(strict, gram[p][:ch, :ch], 0.0)) for p in probs}
    pw = {p: _bf(jnp.where(strict, gram[p][:ch, ch:], 0.0)) for p in probs}
    n_cat = {p: _bf(jnp.where(incl2, gram[p][ch:], 0.0)) for p in probs}
    yield
    z = {p: at_p[p] + _mm(m_ak[p], vo_p[p]) for p in probs}
    yield
    for lvl in range(n_levels):
        if lvl + 1 < n_levels:
            res = {p: _mm(pw[p], jnp.concatenate([_bf(z[p]), pw[p]], axis=1)) for p in probs}
            z = {p: z[p] + res[p][:, :LANES] for p in probs}
            pw = {p: _bf(res[p][:, LANES:]) for p in probs}
        else:
            z = {p: z[p] + _mm(pw[p], z[p]) for p in probs}
        yield
    rhs = {p: jnp.concatenate([vo_p[p], _bf(z[p])], axis=0) for p in probs}
    ry = {p: _mm(n_cat[p], rhs[p]) for p in probs}
    yield
    tg = {p: _mm_tn(kb_p[p], rhs[p]) for p in probs}
    yield

    sts = [st_ref[h] for h in range(R_HEADS)]
    y_chunks = []
    for c in range(nch):
        nxt, g_full = [], []
        for h in range(R_HEADS):
            p = (c, h)
            mine = low_cols if h % 2 == 0 else jnp.logical_not(low_cols)
            my_rows = low_rows if h % 2 == 0 else jnp.logical_not(low_rows)
            r_eff = rt_p[p] + jnp.where(mine, ry[p], 0.0)
            t_full = (jnp.where(my_rows & mine, tg[p], 0.0)
                      + jnp.where(eye & mine, wc_p[p], 0.0))
            g_full.append(jnp.where(my_rows & jnp.logical_not(mine), tg[p], 0.0))
            nxt.append(_mm(jnp.concatenate([r_eff, t_full], axis=0), sts[h]))
        ys = [jnp.where(low_cols if h % 2 else jnp.logical_not(low_cols), nxt[h][:ch] + ry[(c, h)], 0.0)
              for h in range(R_HEADS)]
        y_chunks.append(jnp.concatenate([ys[2 * j] + ys[2 * j + 1] for j in range(R_HEADS // 2)], axis=1))
        sts = [nxt[h][ch:] + g_full[h] for h in range(R_HEADS)]
        yield
    for h in range(R_HEADS):
        st_ref[h] = sts[h]
    y = jnp.concatenate(y_chunks, axis=0)

    inv_d = 1.0 / hd
    mu = _mm(y, head_ones) * inv_d
    yc = y - mu
    var = _mm(yc * yc, head_ones) * inv_d
    yn = yc * lax.rsqrt(var + R_GN_EPS) * lnx_ref[0:1, :] + lnx_ref[1:2, :]
    bonus = _mm(r * k2 * rk_ref[...], swap_ones) * v
    out_ref[...] = (yn + bonus) * g


def _t5_bucket(dist):
    max_exact = N_BUCKETS // 2
    d = jnp.maximum(dist, 1).astype(F32)
    large = max_exact + (jnp.log(d / max_exact) / math.log(MAX_DISTANCE / max_exact)
                         * (N_BUCKETS - max_exact)).astype(jnp.int32)
    large = jnp.minimum(large, N_BUCKETS - 1)
    return jnp.where(dist < max_exact, dist, large)


def _attn_tables(rel_bias):
    blk = A_BLOCK
    i = jnp.arange(blk)[:, None]
    j = jnp.arange(blk)[None, :]
    top = 3 * blk - 1
    period = 5 * blk
    by_dist_rev = rel_bias.astype(F32)[:, _t5_bucket(jnp.maximum(top - jnp.arange(period), 0))]
    shifted = jnp.tile(by_dist_rev, (1, blk))[:, :blk * (period - 1)].reshape(A_HEADS, blk, period - 1)

    def bias(offset):
        return shifted[:, :, top - offset:top - offset + blk]

    neg = jnp.full((A_HEADS, blk, blk), NEG, F32)
    is_meta = j >= FRONT_PAD
    cur_d = i - j
    prev_d = blk + i - j
    cur = jnp.where(cur_d >= 0, bias(0), NEG)
    prev = jnp.where(prev_d < WINDOW, bias(blk), NEG)
    meta0 = jnp.where(is_meta & (cur_d >= 0), bias(0), NEG)
    meta1 = jnp.where(is_meta, bias(blk), NEG)
    meta2 = jnp.where(is_meta, bias(2 * blk), NEG)
    case0 = jnp.concatenate([meta0, neg, neg], axis=-1)
    case1 = jnp.concatenate([meta1, neg, cur], axis=-1)
    case2 = jnp.concatenate([meta2, prev, cur], axis=-1)
    tab = jnp.stack([case0, case1, case2], axis=0)
    tab = tab.reshape(3, A_KV_HEADS, A_GROUP, blk, 3 * blk)
    return tab.transpose(0, 1, 4, 2, 3).reshape(3, A_KV_HEADS, 3 * blk, A_GROUP * blk)


assert A_KV_HEADS * HEAD_DIM == LANES
_PAIRED_HEADS = tuple(kv * A_GROUP + g for g in range(A_GROUP) for kv in range(A_KV_HEADS))


def _pair_heads(w, axis=0):
    split = w.shape[:axis] + (A_KV_HEADS, A_GROUP, HEAD_DIM) + w.shape[axis + 1:]
    return jnp.swapaxes(w.reshape(split), axis, axis + 1).reshape(w.shape)


def _swa_stages(q_ref, kc_ref, kp_ref, km_ref, vc_ref, vp_ref, vm_ref, tab_ref, sink_ref, out_ref):
    scale = HEAD_DIM ** -0.5
    blk = A_BLOCK
    hd = HEAD_DIM
    nb = q_ref.shape[0] // blk
    step = pl.program_id(1)
    kvs = range(A_KV_HEADS)
    upper = _iota2((blk, LANES), 1) >= hd
    yield 4

    v_t = jnp.concatenate([vm_ref[...], vp_ref[...], vc_ref[...]], axis=0).T
    ones = jnp.ones((hd, 3 * blk), F32)

    probs = [(i, kv) for i in range(nb) for kv in kvs]
    s_t = {}
    for i in range(nb):
        rows = slice(i * blk, (i + 1) * blk)
        prev = kp_ref[...] if i == 0 else kc_ref[(i - 1) * blk:i * blk, :]
        keys = jnp.concatenate([km_ref[...], prev, kc_ref[rows, :]], axis=0)
        for kv in kvs:
            mine = upper if kv == 1 else jnp.logical_not(upper)
            q = jnp.concatenate([jnp.where(mine, q_ref[rows, g * LANES:(g + 1) * LANES], 0.0)
                                 for g in range(A_GROUP)], axis=0) * scale
            s_t[i, kv] = _mm_nt(keys, q)
    yield
    p_t, esink = {}, {}
    for i, kv in probs:
        case = 2 if i >= 2 else jnp.minimum(step * nb + i, 2)
        sk = s_t[i, kv] + tab_ref[case, kv]
        sink = sink_ref[kv][0:1, :]
        m = jnp.maximum(jnp.max(sk, axis=0, keepdims=True), sink)
        p_t[i, kv] = _bf(jnp.exp(sk - m))
        esink[i, kv] = jnp.exp(sink - m)
    yield
    o_t = {}
    for i, kv in probs:
        vr = slice(kv * hd, (kv + 1) * hd)
        prev = v_t[vr, blk:2 * blk] if i == 0 else v_t[vr, (i + 1) * blk:(i + 2) * blk]
        vals = jnp.concatenate([v_t[vr, 0:blk], prev, v_t[vr, (i + 2) * blk:(i + 3) * blk]], axis=1)
        r = _mm(jnp.concatenate([vals, ones], axis=0), p_t[i, kv])
        o_t[i, kv] = r[:hd] / (r[hd:] + esink[i, kv])
    yield
    for i in range(nb):
        tiles = [jnp.concatenate([o_t[i, kv][:, g * blk:(g + 1) * blk] for kv in kvs], axis=0).T
                 for g in range(A_GROUP)]
        out_ref[i * blk:(i + 1) * blk, :] = jnp.concatenate(tiles, axis=1)


def _interleave(gens):
    total = [next(g) for g in gens]
    done = [0] * len(gens)
    live = set(range(len(gens)))
    while live:
        i = min(live, key=lambda n: ((done[n] + 1) / total[n], n))
        try:
            next(gens[i])
            done[i] += 1
        except StopIteration:
            live.discard(i)


def _mixers_kernel(*refs, n_m, n_r, n_s):
    m_in, r_in, s_in = refs[:n_m], refs[n_m:n_m + n_r], refs[n_m + n_r:n_m + n_r + n_s]
    ym_ref, yr_ref, ya_ref, cst_ref, m_ref, st_ref = refs[n_m + n_r + n_s:]
    _interleave([_rwkv_stages(*r_in, yr_ref, st_ref),
                 _mlstm_stages(*m_in, ym_ref, cst_ref, m_ref),
                 _swa_stages(*s_in, ya_ref)])


def _mixers(pa, pm, pr, tables, sinks, gate_bias, norm_w, mu, w0, w2, a0, a2, g2, k_k, k_a, r_k, ln_x):
    b, lp, _ = pa.shape
    blk = A_BLOCK
    tb = _mix_rows(lp)
    nb = tb // blk
    row = lambda a: a.reshape(1, -1)
    full = lambda a: pl.BlockSpec(a.shape, lambda i, j: (0,) * a.ndim)
    tile = lambda width: pl.BlockSpec((None, tb, width), lambda i, j: (i, j, 0))

    bias = jnp.zeros((1, LANES), F32).at[0, :2 * M_HEADS].set(gate_bias.reshape(-1))
    m_in = [pm, bias, row(norm_w)]
    m_specs = [tile(PM_W), full(bias), full(m_in[2])]

    r_params = [row(mu), row(w0), w2, row(a0), a2, _swap_pairs(g2, 1), row(k_k), row(k_a), row(r_k),
                _swap_pairs(ln_x, 1)]
    r_in = [pr, pr] + r_params
    r_specs = [tile(PR_W),
               pl.BlockSpec((None, 8, PR_W), lambda i, j: (i, jnp.maximum(j * (tb // 8) - 1, 0), 0))]
    r_specs += [full(a) for a in r_params]

    kcol = A_WIDTH // A_KV_WIDTH
    sink = jnp.broadcast_to(jnp.repeat(sinks.astype(F32).reshape(A_KV_HEADS, 1, A_GROUP), blk, axis=2),
                            (A_KV_HEADS, 8, A_GROUP * blk))
    cur_spec = lambda col: pl.BlockSpec((None, tb, A_KV_WIDTH), lambda i, j: (i, j, col))
    prev_spec = lambda col: pl.BlockSpec((None, blk, A_KV_WIDTH),
                                         lambda i, j: (i, jnp.maximum(j * nb - 1, 0), col))
    meta_spec = lambda col: pl.BlockSpec((None, blk, A_KV_WIDTH), lambda i, j: (i, 0, col))
    s_in = [pa] * 7 + [tables, sink]
    s_specs = [tile(A_WIDTH), cur_spec(kcol), prev_spec(kcol), meta_spec(kcol),
               cur_spec(kcol + 1), prev_spec(kcol + 1), meta_spec(kcol + 1),
               pl.BlockSpec(tables.shape, lambda i, j: (0, 0, 0, 0), pipeline_mode=pl.Buffered(1)),
               full(sink)]

    return pl.pallas_call(
        functools.partial(_mixers_kernel, n_m=len(m_in), n_r=len(r_in), n_s=len(s_in)),
        grid=(b, lp // tb),
        in_specs=m_specs + r_specs + s_specs,
        out_specs=[tile(M_WIDTH), tile(R_WIDTH), tile(A_WIDTH)],
        out_shape=[jax.ShapeDtypeStruct((b, lp, M_WIDTH), F32),
                   jax.ShapeDtypeStruct((b, lp, R_WIDTH), F32),
                   jax.ShapeDtypeStruct((b, lp, A_WIDTH), F32)],
        scratch_shapes=[pltpu.VMEM((M_HEADS // 2, LANES, 2 * LANES), F32),
                        pltpu.VMEM((M_HEADS // 2, 8, LANES), F32),
                        pltpu.VMEM((R_HEADS, LANES, LANES), F32)],
        compiler_params=_cparams(("arbitrary", "arbitrary")),
        name="mixers",
    )(*m_in, *r_in, *s_in)


def _layer_norm_rows(z, ln_ref):
    mu = jnp.mean(z, axis=-1, keepdims=True)
    zc = z - mu
    var = jnp.mean(zc * zc, axis=-1, keepdims=True)
    return zc * lax.rsqrt(var + LN_EPS) * ln_ref[0:1, :] + ln_ref[1:2, :]


FF_CHUNK = 256


def _post_kernel(*refs, n_each, from_x, final, alpha, d_ff):
    groups, pos = [], 0
    for n in n_each:
        groups.append(refs[pos:pos + n])
        pos += n
    w_ref, lnm_ref, wi_ref, wo_ref, lnf_ref, out_ref = refs[pos:]
    ym, yr, ya = (_read_tile(g, False) for g in groups[:3])
    h = _read_tile(groups[3], from_x)
    tm = h.shape[0]
    mix = jnp.dot(ym.astype(BF16), w_ref[0:M_WIDTH, :], preferred_element_type=F32)
    mix += jnp.dot(yr.astype(BF16), w_ref[M_WIDTH:M_WIDTH + R_WIDTH, :], preferred_element_type=F32)
    mix += jnp.dot(ya.astype(BF16), w_ref[M_WIDTH + R_WIDTH:, :], preferred_element_type=F32)
    x = _layer_norm_rows(alpha * h + mix, lnm_ref)
    if not final:
        is_pad = (pl.program_id(1) * tm + _iota2((tm, 1), 0)) < FRONT_PAD
        x = jnp.where(is_pad, 0.0, x)
    xb = x.astype(BF16)
    acc = alpha * x
    for j in range(d_ff // FF_CHUNK):
        cs = slice(j * FF_CHUNK, (j + 1) * FF_CHUNK)
        us = slice(d_ff + j * FF_CHUNK, d_ff + (j + 1) * FF_CHUNK)
        gate = jnp.dot(xb, wi_ref[:, cs], preferred_element_type=F32)
        up = jnp.dot(xb, wi_ref[:, us], preferred_element_type=F32)
        act = (gate * jax.nn.sigmoid(gate) * up).astype(BF16)
        acc += jnp.dot(act, wo_ref[cs, :], preferred_element_type=F32)
    y = _layer_norm_rows(acc, lnf_ref)
    out_ref[...] = y if final else jnp.where(is_pad, 0.0, y)


def _post(ym, yr, ya, h, x, prefix, w, ln_mix, wi, wo, ln_ffn, alpha, final):
    b, lp, _ = ym.shape
    seq, d = x.shape[1:]
    d_ff = wo.shape[0]
    assert d_ff % FF_CHUNK == 0
    from_x = h is None
    if final:
        tm = _row_tile(seq)
        groups = [_token_tile_inputs(a, tm) for a in (ym, yr, ya)]
        groups.append(([x], [pl.BlockSpec((None, tm, d), lambda i, j: (i, j, 0))]) if from_x
                      else _token_tile_inputs(h, tm))
        rows_out = seq
    else:
        tm = _row_tile(lp)
        groups = [_padded_tile_inputs(a, None, None, tm) for a in (ym, yr, ya)]
        groups.append(_padded_tile_inputs(h, x, prefix, tm))
        rows_out = lp
    arrays = [a for g in groups for a in g[0]]
    specs = [s for g in groups for s in g[1]]
    const = lambda a: pl.BlockSpec(a.shape, lambda i, j: (0, 0), pipeline_mode=pl.Buffered(1))
    ln_spec = pl.BlockSpec((2, d), lambda i, j: (0, 0))
    return pl.pallas_call(
        functools.partial(_post_kernel, n_each=tuple(len(g[0]) for g in groups),
                          from_x=from_x and not final, final=final, alpha=alpha, d_ff=d_ff),
        grid=(b, rows_out // tm),
        in_specs=specs + [const(w), ln_spec, const(wi), const(wo), ln_spec],
        out_specs=pl.BlockSpec((None, tm, d), lambda i, j: (i, j, 0)),
        out_shape=jax.ShapeDtypeStruct((b, rows_out, d), F32),
        compiler_params=_cparams(("arbitrary", "arbitrary")),
        name="out_proj_ffn",
    )(*arrays, w, ln_mix, wi, wo, ln_ffn)


def kernel(x, meta_tokens, rel_bias, w_in, m_gate_bias, m_norm_w, r_mu_rkv, r_mu_w, r_mu_a, r_mu_g, r_w0, r_w2, r_a0, r_a2, r_g2, r_k_k, r_k_a, r_r_k, r_ln_x, a_sinks, w_out, ln_mix, w_ff_in, w_ff_out, ln_ffn):
    b, seq, d = x.shape
    depth = w_in.shape[0]
    alpha = (2 * depth) ** 0.25
    assert (TOKEN_START + seq) % A_BLOCK == 0

    lp = TOKEN_START + seq
    prefix = jnp.concatenate([jnp.zeros((FRONT_PAD, d), x.dtype), meta_tokens.astype(x.dtype)], axis=0)
    tables = _attn_tables(rel_bias)
    a_end = PA_W
    m_end = a_end + M_RAW

    h = None
    for l in range(depth):
        wl = w_in[l]
        w_q = _pair_heads(wl[:, :A_WIDTH], 1)
        rv0 = m_end + 2 * R_WIDTH
        w_rv = _swap_pairs(wl[:, rv0:rv0 + R_WIDTH], 1)
        w_pad = jnp.concatenate(
            [w_q, wl[:, A_WIDTH:m_end], jnp.zeros((d, PM_W - M_RAW), wl.dtype), wl[:, m_end:rv0], w_rv,
             wl[:, rv0 + R_WIDTH:]], axis=1).astype(BF16)
        wo = w_out[l]
        wo = jnp.concatenate([wo[:M_WIDTH], _swap_pairs(wo[M_WIDTH:M_WIDTH + R_WIDTH]),
                              _pair_heads(wo[M_WIDTH + R_WIDTH:])], axis=0)
        pa, pm, pr = _proj(h, x, prefix, w_pad, lp)
        mu = jnp.concatenate([r_mu_rkv[l, 0], r_mu_rkv[l, 1], _swap_pairs(r_mu_rkv[l, 2]),
                              r_mu_w[l], r_mu_a[l], r_mu_g[l]])
        y_m, y_r, y_a = _mixers(pa, pm, pr, tables, a_sinks[l], m_gate_bias[l], m_norm_w[l], mu,
                                r_w0[l], r_w2[l], r_a0[l], r_a2[l], r_g2[l], r_k_k[l], r_k_a[l],
                                r_r_k[l], r_ln_x[l])
        h = _post(y_m, y_r, y_a, h, x, prefix, wo.astype(BF16), ln_mix[l],
                  w_ff_in[l].astype(BF16), w_ff_out[l].astype(BF16), ln_ffn[l], alpha,
                  final=l == depth - 1)
    return h
```

```python
import functools
import math

import jax
import jax.numpy as jnp
from jax import lax
from jax.experimental import pallas as pl
from jax.experimental.pallas import tpu as pltpu

F32 = jnp.float32
BF16 = jnp.bfloat16

HEAD_DIM = 64
N_META = 16
M_HEADS = 4
M_WIDTH = M_HEADS * HEAD_DIM
M_CHUNK = 64
M_NORM_EPS = 1e-6
R_HEADS = 4
R_WIDTH = R_HEADS * HEAD_DIM
R_CHUNK = 128
R_DECAY_RANK = 32
R_A_RANK = 32
R_GATE_RANK = 64
R_GN_EPS = 64e-5
A_HEADS = 8
A_KV_HEADS = 2
A_GROUP = A_HEADS // A_KV_HEADS
A_WIDTH = A_HEADS * HEAD_DIM
A_KV_WIDTH = A_KV_HEADS * HEAD_DIM
WINDOW = 128
A_BLOCK = 128
N_BUCKETS = 32
MAX_DISTANCE = 128
LN_EPS = 1e-5
NEG = -1e30

LANES = 128
FRONT_PAD = A_BLOCK - N_META
TOKEN_START = FRONT_PAD + N_META

PA_W = A_WIDTH + 2 * A_KV_WIDTH
PM_W = 4 * M_WIDTH + LANES
PR_W = 3 * R_WIDTH + R_DECAY_RANK + R_A_RANK + R_GATE_RANK
M_RAW = 4 * M_WIDTH + 2 * M_HEADS

VMEM_LIMIT = 56 * 1024 * 1024
MIX_ROWS = 640


def _row_tile(n):
    for t in (640, 512, 256, 128, 64):
        if n % t == 0:
            return t
    raise ValueError(f"row count {n} has no supported tile")


def _mix_rows(lp):
    return MIX_ROWS if lp % MIX_ROWS == 0 else A_BLOCK


def _cparams(sem):
    return pltpu.CompilerParams(dimension_semantics=sem, vmem_limit_bytes=VMEM_LIMIT)


def _bf(x):
    return x.astype(BF16)


def _mm(a, b):
    return lax.dot_general(_bf(a), _bf(b), (((1,), (0,)), ((), ())), preferred_element_type=F32)


def _mm_nt(a, b):
    return lax.dot_general(_bf(a), _bf(b), (((1,), (1,)), ((), ())), preferred_element_type=F32)


def _mm_tn(a, b):
    return lax.dot_general(_bf(a), _bf(b), (((0,), (0,)), ((), ())), preferred_element_type=F32)


def _iota2(shape, dim):
    return lax.broadcasted_iota(jnp.int32, shape, dim)


def _split3(x):
    hi = _bf(x).astype(F32)
    r1 = x - hi
    mid = _bf(r1).astype(F32)
    return hi, mid, r1 - mid


def _chunk_mask(n, chunk, lower):
    r_i = _iota2((n, n), 0)
    c_i = _iota2((n, n), 1)
    sh = chunk.bit_length() - 1
    same = jnp.right_shift(r_i, sh) == jnp.right_shift(c_i, sh)
    return same & (r_i >= c_i) if lower else same


def _cumsum_rows(tri, x):
    w = x.shape[1]
    res = _mm(tri, jnp.concatenate(_split3(x), axis=1))
    return res[:, :w] + res[:, w:2 * w] + res[:, 2 * w:]


def _cumsum_lanes(tri, x):
    r = x.shape[0]
    res = _mm_nt(jnp.concatenate(_split3(x), axis=0), tri)
    return res[:r] + res[r:2 * r] + res[2 * r:]


def _softplus(x):
    return jnp.maximum(x, 0.0) + jnp.log1p(jnp.exp(-jnp.abs(x)))


def _padded_tile_inputs(h, x, prefix, tm):
    if h is not None:
        return [h], [pl.BlockSpec((None, tm, h.shape[-1]), lambda i, j: (i, j, 0))]
    d = x.shape[-1]
    nb = tm // A_BLOCK
    specs = [pl.BlockSpec((A_BLOCK, d), lambda i, j: (0, 0))]
    specs += [pl.BlockSpec((None, A_BLOCK, d), lambda i, j, r=r: (i, jnp.maximum(j * nb - 1 + r, 0), 0))
              for r in range(nb)]
    return [prefix] + [x] * nb, specs


def _token_tile_inputs(a, tm):
    nb = tm // A_BLOCK
    first = TOKEN_START // A_BLOCK
    specs = [pl.BlockSpec((None, A_BLOCK, a.shape[-1]), lambda i, j, r=r: (i, first + j * nb + r, 0))
             for r in range(nb)]
    return [a] * nb, specs


def _read_tile(refs, from_x):
    if from_x:
        first = jnp.where(pl.program_id(1) == 0, refs[0][...], refs[1][...])
        return jnp.concatenate([first] + [r[...] for r in refs[2:]], axis=0)
    if len(refs) == 1:
        return refs[0][...]
    return jnp.concatenate([r[...] for r in refs], axis=0)


def _proj_kernel(*refs, n_in, from_x):
    w_ref, oa_ref, om_ref, or_ref = refs[n_in:]
    x = _read_tile(refs[:n_in], from_x)
    acc = jnp.dot(x.astype(BF16), w_ref[...], preferred_element_type=F32)
    oa_ref[...] = acc[:, :PA_W]
    om_ref[...] = acc[:, PA_W:PA_W + PM_W]
    or_ref[...] = acc[:, PA_W + PM_W:]


def _layer_spec(a, l, **kw):
    return pl.BlockSpec((None,) + a.shape[1:], lambda i, j: (l,) + (0,) * (a.ndim - 1), **kw)


def _proj(h, x, prefix, w, l, lp):
    b = x.shape[0]
    d, n = w.shape[1:]
    tm = _row_tile(lp)
    arrays, specs = _padded_tile_inputs(h, x, prefix, tm)
    spec = lambda width: pl.BlockSpec((None, tm, width), lambda i, j: (i, j, 0))
    return pl.pallas_call(
        functools.partial(_proj_kernel, n_in=len(arrays), from_x=h is None),
        grid=(b, lp // tm),
        in_specs=specs + [_layer_spec(w, l, pipeline_mode=pl.Buffered(1))],
        out_specs=[spec(PA_W), spec(PM_W), spec(PR_W)],
        out_shape=[jax.ShapeDtypeStruct((b, lp, PA_W), F32),
                   jax.ShapeDtypeStruct((b, lp, PM_W), F32),
                   jax.ShapeDtypeStruct((b, lp, PR_W), F32)],
        compiler_params=_cparams(("arbitrary", "arbitrary")),
        name="in_proj",
    )(*arrays, w)


def _mlstm_stages(pm_ref, bias_ref, nw_ref, out_ref, cst_ref, m_ref):
    blk = pl.program_id(1)
    tb = pm_ref.shape[0]
    ch = M_CHUNK
    hd = HEAD_DIM
    nch = tb // ch
    npair = M_HEADS // 2
    assert 2 * hd == LANES and 2 * ch == LANES
    yield 8

    @pl.when(blk == 0)
    def _():
        cst_ref[...] = jnp.zeros_like(cst_ref)
        m_ref[...] = jnp.full_like(m_ref, NEG)

    row = blk * tb + _iota2((tb, 1), 0)
    is_pad = row < FRONT_PAD
    gb = pm_ref[:, 4 * M_WIDTH:] + bias_ref[...]
    lane = _iota2((tb, LANES), 1)
    li = jnp.where(is_pad, NEG, gb)
    lf = jnp.where(is_pad, 0.0, -_softplus(-gb))
    gcol = jnp.where(lane < M_HEADS, li, jnp.where(lane < 2 * M_HEADS, lf, 0.0))
    sel = (_iota2((8, LANES), 0) == _iota2((8, LANES), 1)).astype(F32)
    grow = _mm_nt(jnp.concatenate([sel] * 3, axis=1), jnp.concatenate(_split3(gcol), axis=1))
    brow = _cumsum_lanes(_chunk_mask(tb, ch, True).astype(F32), grow)
    r8 = _iota2((8, tb), 0)
    t8 = jnp.bitwise_and(_iota2((8, tb), 1), ch - 1)
    cm = grow - pltpu.roll(brow, M_HEADS, axis=0)
    sh = 1
    while sh < ch:
        cm = jnp.where(t8 >= sh, jnp.maximum(cm, pltpu.roll(cm, sh, axis=1)), cm)
        sh *= 2
    rows_lb = jnp.where(r8 < M_HEADS, grow, brow)
    x16 = jnp.concatenate([rows_lb, cm], axis=0)
    nrep = 3 * M_WIDTH
    e_sel = (_iota2((16, nrep), 0) == jnp.right_shift(_iota2((16, nrep), 1), hd.bit_length() - 1)
             ).astype(F32)
    rep = _mm_tn(jnp.concatenate(_split3(x16), axis=0), jnp.concatenate([e_sel] * 3, axis=0))
    rep_of = lambda qn, j, cs: rep[cs, qn * M_WIDTH + j * LANES:qn * M_WIDTH + (j + 1) * LANES]
    rows_up = pltpu.roll(rows_lb, ch, axis=1)
    rows_dn = pltpu.roll(rows_lb, tb - ch, axis=1)
    low1 = _iota2((1, LANES), 1) < hd

    def row_pair(base, c, j):
        ts = slice((c // 2) * LANES, (c // 2 + 1) * LANES)
        h0, h1 = base + 2 * j, base + 2 * j + 1
        if c % 2 == 0:
            return jnp.where(low1, rows_lb[h0:h0 + 1, ts], rows_up[h1:h1 + 1, ts])
        return jnp.where(low1, rows_dn[h0:h0 + 1, ts], rows_lb[h1:h1 + 1, ts])

    low = _iota2((ch, LANES), 1) < hd
    t_i = _iota2((ch, LANES), 0)
    causal2 = t_i >= jnp.bitwise_and(_iota2((ch, LANES), 1), ch - 1)
    r128 = _iota2((LANES, LANES), 0)
    c128 = _iota2((LANES, LANES), 1)
    ones_bd = ((r128 < hd) == (c128 < hd)).astype(F32)
    bd2 = jnp.concatenate([ones_bd, ones_bd], axis=1) > 0.5
    ones_t = jnp.ones((ch, LANES), F32)

    probs = [(c, j) for c in range(nch) for j in range(npair)]
    q_p, k_p, v_p = {}, {}, {}
    for c, j in probs:
        cs = slice(c * ch, (c + 1) * ch)
        ps = slice(j * LANES, (j + 1) * LANES)
        q_p[c, j] = pm_ref[cs, ps]
        k_p[c, j] = pm_ref[cs, M_WIDTH + ps.start:M_WIDTH + ps.stop] * (hd ** -0.5)
        v_p[c, j] = pm_ref[cs, 2 * M_WIDTH + ps.start:2 * M_WIDTH + ps.stop]
    bdiag = lambda a: jnp.concatenate([jnp.where(low, a, 0.0), jnp.where(low, 0.0, a)], axis=0)
    yield
    qk = {p: _mm_nt(q_p[p], bdiag(k_p[p])) for p in probs}
    yield

    dw_p, sint_p, emt_p, wa_p, decay_p = {}, {}, {}, {}, {}
    for j in range(npair):
        m_prev = m_ref[j][0:1, :]
        for c in range(nch):
            cs = slice(c * ch, (c + 1) * ch)
            li_c, b_c, cm_c = rep_of(0, j, cs), rep_of(1, j, cs), rep_of(2, j, cs)
            d_mat = jnp.where(causal2, b_c - row_pair(M_HEADS, c, j) + row_pair(0, c, j), NEG)
            inter = b_c + m_prev
            m_t = jnp.maximum(b_c + cm_c, inter)
            dw_p[c, j] = jnp.where(causal2, jnp.exp(d_mat - m_t), 0.0)
            sint_p[c, j] = jnp.exp(inter - m_t)
            emt_p[c, j] = jnp.exp(-m_t)
            g = b_c[ch - 1:ch, :]
            m_new = jnp.maximum(g + m_prev, g + cm_c[ch - 1:ch, :])
            wa_p[c, j] = jnp.exp(g - b_c + li_c - m_new)
            decay_p[c, j] = jnp.exp(g + m_prev - m_new)
            m_prev = m_new
        m_ref[j] = jnp.broadcast_to(m_prev, m_ref.shape[1:])

    yield
    upd = {p: jnp.where(bd2, _mm_tn(wa_p[p] * k_p[p], jnp.concatenate([v_p[p], ones_t], axis=1)), 0.0)
           for p in probs}
    yield
    st_p = {}
    for j in range(npair):
        st = cst_ref[j]
        for c in range(nch):
            st_p[c, j] = st
            dec = decay_p[c, j]
            st = jnp.concatenate([dec, dec], axis=1) * st + upd[c, j]
        cst_ref[j] = st
    yield
    inter_p = {p: _mm(q_p[p], st_p[p]) for p in probs}
    yield
    intra_p = {p: _mm(qk[p] * dw_p[p], jnp.concatenate([bdiag(v_p[p]), ones_bd], axis=1)) for p in probs}
    yield

    rows = []
    for c in range(nch):
        pairs = []
        for j in range(npair):
            p = (c, j)
            tot = intra_p[p] + jnp.concatenate([sint_p[p], sint_p[p]], axis=1) * inter_p[p]
            pairs.append(tot[:, :LANES] / jnp.maximum(jnp.abs(tot[:, LANES:]), emt_p[p]))
        rows.append(jnp.concatenate(pairs, axis=1))
    hh = jnp.concatenate(rows, axis=0)

    head_ones = _chunk_mask(M_WIDTH, hd, False).astype(BF16)
    mu = _mm(hh, head_ones) * (1.0 / hd)
    hc = hh - mu
    var = _mm(hc * hc, head_ones) * (1.0 / hd)
    o_pre = pm_ref[:, 3 * M_WIDTH:4 * M_WIDTH]
    out_ref[...] = hc * lax.rsqrt(var + M_NORM_EPS) * nw_ref[...] * jax.nn.sigmoid(o_pre)


def _swap_pairs(w, axis=0):
    split = w.shape[:axis] + (R_HEADS // 2, 2, HEAD_DIM) + w.shape[axis + 1:]
    return jnp.flip(w.reshape(split), axis=axis + 1).reshape(w.shape)


def _rwkv_stages(pr_ref, prev_ref, mu_ref, w0_ref, w2_ref, a0_ref, a2_ref, g2_ref,
                 kk_ref, ka_ref, rk_ref, lnx_ref, out_ref, st_ref):
    blk = pl.program_id(1)
    tb = pr_ref.shape[0]
    ch = R_CHUNK
    rw = R_WIDTH
    hd = HEAD_DIM
    nch = tb // ch
    assert ch == LANES and 2 * hd == LANES
    n_levels = ch.bit_length() - 1
    yield 6 + n_levels + nch

    @pl.when(blk == 0)
    def _():
        st_ref[...] = jnp.zeros_like(st_ref)

    x = pr_ref[...]
    last = jnp.where(blk == 0, 0.0, prev_ref[7:8, :])
    prev = jnp.where(_iota2((tb, 1), 0) == 0, last, pltpu.roll(x, 1, axis=0))
    t = x + (prev - x) * mu_ref[...]
    r = t[:, 0:rw]
    k = t[:, rw:2 * rw]
    v = t[:, 2 * rw:3 * rw]
    o = 3 * rw
    w_lat = t[:, o:o + R_DECAY_RANK]
    a_lat = t[:, o + R_DECAY_RANK:o + R_DECAY_RANK + R_A_RANK]
    g_lat = t[:, o + R_DECAY_RANK + R_A_RANK:]

    w_log = -_softplus(-(w0_ref[...] + _mm(jnp.tanh(w_lat), w2_ref[...]))) - 0.5
    lw = -jnp.exp(w_log)
    a = jax.nn.sigmoid(a0_ref[...] + _mm(a_lat, a2_ref[...]))
    g = _mm(jax.nn.sigmoid(g_lat), g2_ref[...])

    head_ones = _chunk_mask(rw, hd, False).astype(BF16)
    sh = hd.bit_length() - 1
    swap_ones = ((jnp.right_shift(_iota2((rw, rw), 0), sh) ^ 1)
                 == jnp.right_shift(_iota2((rw, rw), 1), sh)).astype(BF16)
    kk = k * kk_ref[...]
    kk = kk / jnp.maximum(jnp.sqrt(_mm(kk * kk, head_ones)), 1e-12)
    k2 = k * (1.0 + (a - 1.0) * ka_ref[...])
    avec = -kk
    bvec = kk * a
    cum = _cumsum_rows(_chunk_mask(tb, ch, True).astype(F32), lw)

    r_i = _iota2((ch, ch), 0)
    c_i = _iota2((ch, ch), 1)
    strict = r_i > c_i
    eye = r_i == c_i
    incl2 = jnp.concatenate([r_i >= c_i] * 2, axis=1)
    low_cols = c_i < hd
    low_rows = r_i < hd

    probs = [(c, h) for c in range(nch) for h in range(R_HEADS)]
    at_p, rt_p, ac_p, rc_p, vo_p, kb_t, kb_p, wc_p = ({} for _ in range(8))
    for c in range(nch):
        cs = slice(c * ch, (c + 1) * ch)
        cum_c = cum[cs]
        mid = cum_c[ch // 2 - 1:ch // 2, :]
        cl = cum_c[ch - 1:ch, :]
        e_mid = jnp.exp(-mid)
        a_true = avec[cs] * jnp.exp(cum_c - lw[cs])
        r_true = r[cs] * jnp.exp(cum_c)
        a_cen = a_true * e_mid
        r_cen = r_true * e_mid
        inv = jnp.exp(mid - cum_c)
        rel = jnp.exp(cl - cum_c)
        kt_c, bt_c = k2[cs] * inv, bvec[cs] * inv
        kp_c, bp_c = k2[cs] * rel, bvec[cs] * rel
        wc_c = jnp.exp(cl)
        v_c = v[cs]
        for h in range(R_HEADS):
            ps = slice((h // 2) * LANES, (h // 2 + 1) * LANES)
            mine = low_cols if h % 2 == 0 else jnp.logical_not(low_cols)
            key = (c, h)
            at_p[key] = jnp.where(mine, a_true[:, ps], 0.0)
            rt_p[key] = jnp.where(mine, r_true[:, ps], 0.0)
            ac_p[key] = _bf(jnp.where(mine, a_cen[:, ps], 0.0))
            rc_p[key] = _bf(jnp.where(mine, r_cen[:, ps], 0.0))
            vo_p[key] = _bf(jnp.where(mine, 0.0, v_c[:, ps]))
            kb_t[key] = _bf(jnp.concatenate([kt_c[:, ps], bt_c[:, ps]], axis=0))
            kb_p[key] = _bf(jnp.concatenate([kp_c[:, ps], bp_c[:, ps]], axis=0))
            wc_p[key] = wc_c[:, ps]

    yield
    gram = {p: _mm_nt(jnp.concatenate([ac_p[p], rc_p[p]], axis=0), kb_t[p]) for p in probs}
    m_ak = {p: _bf(jnp.where(strict, gram[p][:ch, :ch], 0.0)) for p in probs}
    pw = {p: _bf(jnp.where(strict, gram[p][:ch, ch:], 0.0)) for p in probs}
    n_cat = {p: _bf(jnp.where(incl2, gram[p][ch:], 0.0)) for p in probs}
    yield
    z = {p: at_p[p] + _mm(m_ak[p], vo_p[p]) for p in probs}
    yield
    for lvl in range(n_levels):
        if lvl + 1 < n_levels:
            res = {p: _mm(pw[p], jnp.concatenate([_bf(z[p]), pw[p]], axis=1)) for p in probs}
            z = {p: z[p] + res[p][:, :LANES] for p in probs}
            pw = {p: _bf(res[p][:, LANES:]) for p in probs}
        else:
            z = {p: z[p] + _mm(pw[p], z[p]) for p in probs}
        yield
    rhs = {p: jnp.concatenate([vo_p[p], _bf(z[p])], axis=0) for p in probs}
    ry = {p: _mm(n_cat[p], rhs[p]) for p in probs}
    yield
    tg = {p: _mm_tn(kb_p[p], rhs[p]) for p in probs}
    yield

    sts = [st_ref[h] for h in range(R_HEADS)]
    y_chunks = []
    for c in range(nch):
        nxt, g_full = [], []
        for h in range(R_HEADS):
            p = (c, h)
            mine = low_cols if h % 2 == 0 else jnp.logical_not(low_cols)
            my_rows = low_rows if h % 2 == 0 else jnp.logical_not(low_rows)
            r_eff = rt_p[p] + jnp.where(mine, ry[p], 0.0)
            t_full = (jnp.where(my_rows & mine, tg[p], 0.0)
                      + jnp.where(eye & mine, wc_p[p], 0.0))
            g_full.append(jnp.where(my_rows & jnp.logical_not(mine), tg[p], 0.0))
            nxt.append(_mm(jnp.concatenate([r_eff, t_full], axis=0), sts[h]))
        ys = [jnp.where(low_cols if h % 2 else jnp.logical_not(low_cols), nxt[h][:ch] + ry[(c, h)], 0.0)
              for h in range(R_HEADS)]
        y_chunks.append(jnp.concatenate([ys[2 * j] + ys[2 * j + 1] for j in range(R_HEADS // 2)], axis=1))
        sts = [nxt[h][ch:] + g_full[h] for h in range(R_HEADS)]
        yield
    for h in range(R_HEADS):
        st_ref[h] = sts[h]
    y = jnp.concatenate(y_chunks, axis=0)

    inv_d = 1.0 / hd
    mu = _mm(y, head_ones) * inv_d
    yc = y - mu
    var = _mm(yc * yc, head_ones) * inv_d
    yn = yc * lax.rsqrt(var + R_GN_EPS) * lnx_ref[0:1, :] + lnx_ref[1:2, :]
    bonus = _mm(r * k2 * rk_ref[...], swap_ones) * v
    out_ref[...] = (yn + bonus) * g


def _t5_bucket(dist):
    max_exact = N_BUCKETS // 2
    d = jnp.maximum(dist, 1).astype(F32)
    large = max_exact + (jnp.log(d / max_exact) / math.log(MAX_DISTANCE / max_exact)
                         * (N_BUCKETS - max_exact)).astype(jnp.int32)
    large = jnp.minimum(large, N_BUCKETS - 1)
    return jnp.where(dist < max_exact, dist, large)


def _attn_tables(rel_bias):
    blk = A_BLOCK
    i = jnp.arange(blk)[:, None]
    j = jnp.arange(blk)[None, :]
    top = 3 * blk - 1
    period = 5 * blk
    by_dist_rev = rel_bias.astype(F32)[:, _t5_bucket(jnp.maximum(top - jnp.arange(period), 0))]
    shifted = jnp.tile(by_dist_rev, (1, blk))[:, :blk * (period - 1)].reshape(A_HEADS, blk, period - 1)

    def bias(offset):
        return shifted[:, :, top - offset:top - offset + blk]

    neg = jnp.full((A_HEADS, blk, blk), NEG, F32)
    is_meta = j >= FRONT_PAD
    cur_d = i - j
    prev_d = blk + i - j
    cur = jnp.where(cur_d >= 0, bias(0), NEG)
    prev = jnp.where(prev_d < WINDOW, bias(blk), NEG)
    meta0 = jnp.where(is_meta & (cur_d >= 0), bias(0), NEG)
    meta1 = jnp.where(is_meta, bias(blk), NEG)
    meta2 = jnp.where(is_meta, bias(2 * blk), NEG)
    case0 = jnp.concatenate([meta0, neg, neg], axis=-1)
    case1 = jnp.concatenate([meta1, neg, cur], axis=-1)
    case2 = jnp.concatenate([meta2, prev, cur], axis=-1)
    tab = jnp.stack([case0, case1, case2], axis=0)
    tab = tab.reshape(3, A_KV_HEADS, A_GROUP, blk, 3 * blk)
    return tab.transpose(0, 1, 4, 2, 3).reshape(3, A_KV_HEADS, 3 * blk, A_GROUP * blk)


assert A_KV_HEADS * HEAD_DIM == LANES
_PAIRED_HEADS = tuple(kv * A_GROUP + g for g in range(A_GROUP) for kv in range(A_KV_HEADS))


def _pair_heads(w, axis=0):
    split = w.shape[:axis] + (A_KV_HEADS, A_GROUP, HEAD_DIM) + w.shape[axis + 1:]
    return jnp.swapaxes(w.reshape(split), axis, axis + 1).reshape(w.shape)


def _swa_stages(q_ref, kc_ref, kp_ref, km_ref, vc_ref, vp_ref, vm_ref, tab_ref, sink_ref, out_ref):
    scale = HEAD_DIM ** -0.5
    blk = A_BLOCK
    hd = HEAD_DIM
    nb = q_ref.shape[0] // blk
    step = pl.program_id(1)
    kvs = range(A_KV_HEADS)
    upper = _iota2((blk, LANES), 1) >= hd
    yield 4

    v_t = jnp.concatenate([vm_ref[...], vp_ref[...], vc_ref[...]], axis=0).T
    ones = jnp.ones((hd, 3 * blk), F32)

    probs = [(i, kv) for i in range(nb) for kv in kvs]
    s_t = {}
    for i in range(nb):
        rows = slice(i * blk, (i + 1) * blk)
        prev = kp_ref[...] if i == 0 else kc_ref[(i - 1) * blk:i * blk, :]
        keys = jnp.concatenate([km_ref[...], prev, kc_ref[rows, :]], axis=0)
        for kv in kvs:
            mine = upper if kv == 1 else jnp.logical_not(upper)
            q = jnp.concatenate([jnp.where(mine, q_ref[rows, g * LANES:(g + 1) * LANES], 0.0)
                                 for g in range(A_GROUP)], axis=0) * scale
            s_t[i, kv] = _mm_nt(keys, q)
    yield
    p_t, esink = {}, {}
    for i, kv in probs:
        case = 2 if i >= 2 else jnp.minimum(step * nb + i, 2)
        sk = s_t[i, kv] + tab_ref[case, kv]
        sink = sink_ref[kv][0:1, :]
        m = jnp.maximum(jnp.max(sk, axis=0, keepdims=True), sink)
        p_t[i, kv] = _bf(jnp.exp(sk - m))
        esink[i, kv] = jnp.exp(sink - m)
    yield
    o_t = {}
    for i, kv in probs:
        vr = slice(kv * hd, (kv + 1) * hd)
        prev = v_t[vr, blk:2 * blk] if i == 0 else v_t[vr, (i + 1) * blk:(i + 2) * blk]
        vals = jnp.concatenate([v_t[vr, 0:blk], prev, v_t[vr, (i + 2) * blk:(i + 3) * blk]], axis=1)
        r = _mm(jnp.concatenate([vals, ones], axis=0), p_t[i, kv])
        o_t[i, kv] = r[:hd] / (r[hd:] + esink[i, kv])
    yield
    for i in range(nb):
        tiles = [jnp.concatenate([o_t[i, kv][:, g * blk:(g + 1) * blk] for kv in kvs], axis=0).T
                 for g in range(A_GROUP)]
        out_ref[i * blk:(i + 1) * blk, :] = jnp.concatenate(tiles, axis=1)


def _interleave(gens):
    total = [next(g) for g in gens]
    done = [0] * len(gens)
    live = set(range(len(gens)))
    while live:
        i = min(live, key=lambda n: ((done[n] + 1) / total[n], n))
        try:
            next(gens[i])
            done[i] += 1
        except StopIteration:
            live.discard(i)


def _mixers_kernel(*refs, n_m, n_r, n_s):
    m_in, r_in, s_in = refs[:n_m], refs[n_m:n_m + n_r], refs[n_m + n_r:n_m + n_r + n_s]
    ym_ref, yr_ref, ya_ref, cst_ref, m_ref, st_ref = refs[n_m + n_r + n_s:]
    _interleave([_rwkv_stages(*r_in, yr_ref, st_ref),
                 _mlstm_stages(*m_in, ym_ref, cst_ref, m_ref),
                 _swa_stages(*s_in, ya_ref)])


def _mixers(pa, pm, pr, tables, l, sink, bias, norm_w, mu, w0, w2, a0, a2, g2, k_k, k_a, r_k, ln_x):
    b, lp, _ = pa.shape
    blk = A_BLOCK
    tb = _mix_rows(lp)
    nb = tb // blk
    tile = lambda width: pl.BlockSpec((None, tb, width), lambda i, j: (i, j, 0))

    m_in = [pm, bias, norm_w]
    m_specs = [tile(PM_W), _layer_spec(bias, l), _layer_spec(norm_w, l)]

    r_params = [mu, w0, w2, a0, a2, g2, k_k, k_a, r_k, ln_x]
    r_in = [pr, pr] + r_params
    r_specs = [tile(PR_W),
               pl.BlockSpec((None, 8, PR_W), lambda i, j: (i, jnp.maximum(j * (tb // 8) - 1, 0), 0))]
    r_specs += [_layer_spec(a, l) for a in r_params]

    kcol = A_WIDTH // A_KV_WIDTH
    cur_spec = lambda col: pl.BlockSpec((None, tb, A_KV_WIDTH), lambda i, j: (i, j, col))
    prev_spec = lambda col: pl.BlockSpec((None, blk, A_KV_WIDTH),
                                         lambda i, j: (i, jnp.maximum(j * nb - 1, 0), col))
    meta_spec = lambda col: pl.BlockSpec((None, blk, A_KV_WIDTH), lambda i, j: (i, 0, col))
    s_in = [pa] * 7 + [tables, sink]
    s_specs = [tile(A_WIDTH), cur_spec(kcol), prev_spec(kcol), meta_spec(kcol),
               cur_spec(kcol + 1), prev_spec(kcol + 1), meta_spec(kcol + 1),
               pl.BlockSpec(tables.shape, lambda i, j: (0, 0, 0, 0), pipeline_mode=pl.Buffered(1)),
               _layer_spec(sink, l)]

    return pl.pallas_call(
        functools.partial(_mixers_kernel, n_m=len(m_in), n_r=len(r_in), n_s=len(s_in)),
        grid=(b, lp // tb),
        in_specs=m_specs + r_specs + s_specs,
        out_specs=[tile(M_WIDTH), tile(R_WIDTH), tile(A_WIDTH)],
        out_shape=[jax.ShapeDtypeStruct((b, lp, M_WIDTH), F32),
                   jax.ShapeDtypeStruct((b, lp, R_WIDTH), F32),
                   jax.ShapeDtypeStruct((b, lp, A_WIDTH), F32)],
        scratch_shapes=[pltpu.VMEM((M_HEADS // 2, LANES, 2 * LANES), F32),
                        pltpu.VMEM((M_HEADS // 2, 8, LANES), F32),
                        pltpu.VMEM((R_HEADS, LANES, LANES), F32)],
        compiler_params=_cparams(("arbitrary", "arbitrary")),
        name="mixers",
    )(*m_in, *r_in, *s_in)


def _layer_norm_rows(z, ln_ref):
    mu = jnp.mean(z, axis=-1, keepdims=True)
    zc = z - mu
    var = jnp.mean(zc * zc, axis=-1, keepdims=True)
    return zc * lax.rsqrt(var + LN_EPS) * ln_ref[0:1, :] + ln_ref[1:2, :]


FF_CHUNK = 256


def _post_kernel(*refs, n_each, from_x, final, alpha, d_ff):
    groups, pos = [], 0
    for n in n_each:
        groups.append(refs[pos:pos + n])
        pos += n
    w_ref, lnm_ref, wi_ref, wo_ref, lnf_ref, out_ref = refs[pos:]
    ym, yr, ya = (_read_tile(g, False) for g in groups[:3])
    h = _read_tile(groups[3], from_x)
    tm = h.shape[0]
    mix = jnp.dot(ym.astype(BF16), w_ref[0:M_WIDTH, :], preferred_element_type=F32)
    mix += jnp.dot(yr.astype(BF16), w_ref[M_WIDTH:M_WIDTH + R_WIDTH, :], preferred_element_type=F32)
    mix += jnp.dot(ya.astype(BF16), w_ref[M_WIDTH + R_WIDTH:, :], preferred_element_type=F32)
    x = _layer_norm_rows(alpha * h + mix, lnm_ref)
    if not final:
        is_pad = (pl.program_id(1) * tm + _iota2((tm, 1), 0)) < FRONT_PAD
        x = jnp.where(is_pad, 0.0, x)
    xb = x.astype(BF16)
    acc = alpha * x
    for j in range(d_ff // FF_CHUNK):
        cs = slice(j * FF_CHUNK, (j + 1) * FF_CHUNK)
        us = slice(d_ff + j * FF_CHUNK, d_ff + (j + 1) * FF_CHUNK)
        gate = jnp.dot(xb, wi_ref[:, cs], preferred_element_type=F32)
        up = jnp.dot(xb, wi_ref[:, us], preferred_element_type=F32)
        act = (gate * jax.nn.sigmoid(gate) * up).astype(BF16)
        acc += jnp.dot(act, wo_ref[cs, :], preferred_element_type=F32)
    y = _layer_norm_rows(acc, lnf_ref)
    out_ref[...] = y if final else jnp.where(is_pad, 0.0, y)


def _post(ym, yr, ya, h, x, prefix, w, ln_mix, wi, wo, ln_ffn, l, alpha, final):
    b, lp, _ = ym.shape
    seq, d = x.shape[1:]
    d_ff = wo.shape[1]
    assert d_ff % FF_CHUNK == 0
    from_x = h is None
    if final:
        tm = _row_tile(seq)
        groups = [_token_tile_inputs(a, tm) for a in (ym, yr, ya)]
        groups.append(([x], [pl.BlockSpec((None, tm, d), lambda i, j: (i, j, 0))]) if from_x
                      else _token_tile_inputs(h, tm))
        rows_out = seq
    else:
        tm = _row_tile(lp)
        groups = [_padded_tile_inputs(a, None, None, tm) for a in (ym, yr, ya)]
        groups.append(_padded_tile_inputs(h, x, prefix, tm))
        rows_out = lp
    arrays = [a for g in groups for a in g[0]]
    specs = [s for g in groups for s in g[1]]
    const = lambda a: _layer_spec(a, l, pipeline_mode=pl.Buffered(1))
    return pl.pallas_call(
        functools.partial(_post_kernel, n_each=tuple(len(g[0]) for g in groups),
                          from_x=from_x and not final, final=final, alpha=alpha, d_ff=d_ff),
        grid=(b, rows_out // tm),
        in_specs=specs + [const(w), _layer_spec(ln_mix, l), const(wi), const(wo), _layer_spec(ln_ffn, l)],
        out_specs=pl.BlockSpec((None, tm, d), lambda i, j: (i, j, 0)),
        out_shape=jax.ShapeDtypeStruct((b, rows_out, d), F32),
        compiler_params=_cparams(("arbitrary", "arbitrary")),
        name="out_proj_ffn",
    )(*arrays, w, ln_mix, wi, wo, ln_ffn)


def kernel(x, meta_tokens, rel_bias, w_in, m_gate_bias, m_norm_w, r_mu_rkv, r_mu_w, r_mu_a, r_mu_g, r_w0, r_w2, r_a0, r_a2, r_g2, r_k_k, r_k_a, r_r_k, r_ln_x, a_sinks, w_out, ln_mix, w_ff_in, w_ff_out, ln_ffn):
    b, seq, d = x.shape
    depth = w_in.shape[0]
    alpha = (2 * depth) ** 0.25
    assert (TOKEN_START + seq) % A_BLOCK == 0

    lp = TOKEN_START + seq
    prefix = jnp.concatenate([jnp.zeros((FRONT_PAD, d), x.dtype), meta_tokens.astype(x.dtype)], axis=0)
    tables = _attn_tables(rel_bias)

    m_end = PA_W + M_RAW
    rv0 = m_end + 2 * R_WIDTH
    w_pad = jnp.concatenate(
        [_pair_heads(w_in[:, :, :A_WIDTH], 2), w_in[:, :, A_WIDTH:m_end],
         jnp.zeros((depth, d, PM_W - M_RAW), w_in.dtype), w_in[:, :, m_end:rv0],
         _swap_pairs(w_in[:, :, rv0:rv0 + R_WIDTH], 2), w_in[:, :, rv0 + R_WIDTH:]], axis=2).astype(BF16)
    w_o = jnp.concatenate([w_out[:, :M_WIDTH], _swap_pairs(w_out[:, M_WIDTH:M_WIDTH + R_WIDTH], 1),
                           _pair_heads(w_out[:, M_WIDTH + R_WIDTH:], 1)], axis=1).astype(BF16)
    w_fi, w_fo = w_ff_in.astype(BF16), w_ff_out.astype(BF16)
    rows = lambda a: a.reshape(depth, 1, -1)
    mu = jnp.concatenate([r_mu_rkv[:, 0], r_mu_rkv[:, 1], _swap_pairs(r_mu_rkv[:, 2], 1),
                          r_mu_w, r_mu_a, r_mu_g], axis=1)
    bias = jnp.pad(rows(m_gate_bias.astype(F32)), ((0, 0), (0, 0), (0, LANES - 2 * M_HEADS)))
    sink = jnp.broadcast_to(
        jnp.repeat(a_sinks.astype(F32).reshape(depth, A_KV_HEADS, 1, A_GROUP), A_BLOCK, axis=3),
        (depth, A_KV_HEADS, 8, A_GROUP * A_BLOCK))
    mixer_params = (sink, bias, rows(m_norm_w), rows(mu), rows(r_w0), r_w2, rows(r_a0), r_a2,
                    _swap_pairs(r_g2, 2), rows(r_k_k), rows(r_k_a), rows(r_r_k), _swap_pairs(r_ln_x, 2))

    h = None
    for l in range(depth):
        pa, pm, pr = _proj(h, x, prefix, w_pad, l, lp)
        y_m, y_r, y_a = _mixers(pa, pm, pr, tables, l, *mixer_params)
        h = _post(y_m, y_r, y_a, h, x, prefix, w_o, ln_mix, w_fi, w_fo, ln_ffn, l, alpha,
                  final=l == depth - 1)
    return h
```

```python
import functools
import math

import jax
import jax.numpy as jnp
from jax import lax
from jax.experimental import pallas as pl
from jax.experimental.pallas import tpu as pltpu

F32 = jnp.float32
BF16 = jnp.bfloat16

HEAD_DIM = 64
N_META = 16
M_HEADS = 4
M_WIDTH = M_HEADS * HEAD_DIM
M_CHUNK = 64
M_NORM_EPS = 1e-6
R_HEADS = 4
R_WIDTH = R_HEADS * HEAD_DIM
R_CHUNK = 128
R_DECAY_RANK = 32
R_A_RANK = 32
R_GATE_RANK = 64
R_GN_EPS = 64e-5
A_HEADS = 8
A_KV_HEADS = 2
A_GROUP = A_HEADS // A_KV_HEADS
A_WIDTH = A_HEADS * HEAD_DIM
A_KV_WIDTH = A_KV_HEADS * HEAD_DIM
WINDOW = 128
A_BLOCK = 128
N_BUCKETS = 32
MAX_DISTANCE = 128
LN_EPS = 1e-5
NEG = -1e30

LANES = 128
FRONT_PAD = A_BLOCK - N_META
TOKEN_START = FRONT_PAD + N_META

PA_W = A_WIDTH + 2 * A_KV_WIDTH
PM_W = 4 * M_WIDTH + LANES
PR_W = 3 * R_WIDTH + R_DECAY_RANK + R_A_RANK + R_GATE_RANK
M_RAW = 4 * M_WIDTH + 2 * M_HEADS

VMEM_LIMIT = 56 * 1024 * 1024
MIX_ROWS = 640


def _row_tile(n):
    for t in (640, 512, 256, 128, 64):
        if n % t == 0:
            return t
    raise ValueError(f"row count {n} has no supported tile")


def _mix_rows(lp):
    return MIX_ROWS if lp % MIX_ROWS == 0 else A_BLOCK


def _cparams(sem):
    return pltpu.CompilerParams(dimension_semantics=sem, vmem_limit_bytes=VMEM_LIMIT)


def _bf(x):
    return x.astype(BF16)


def _mm(a, b):
    return lax.dot_general(_bf(a), _bf(b), (((1,), (0,)), ((), ())), preferred_element_type=F32)


def _mm_nt(a, b):
    return lax.dot_general(_bf(a), _bf(b), (((1,), (1,)), ((), ())), preferred_element_type=F32)


def _mm_tn(a, b):
    return lax.dot_general(_bf(a), _bf(b), (((0,), (0,)), ((), ())), preferred_element_type=F32)


def _iota2(shape, dim):
    return lax.broadcasted_iota(jnp.int32, shape, dim)


def _split3(x):
    hi = _bf(x).astype(F32)
    r1 = x - hi
    mid = _bf(r1).astype(F32)
    return hi, mid, r1 - mid


def _chunk_mask(n, chunk, lower):
    r_i = _iota2((n, n), 0)
    c_i = _iota2((n, n), 1)
    sh = chunk.bit_length() - 1
    same = jnp.right_shift(r_i, sh) == jnp.right_shift(c_i, sh)
    return same & (r_i >= c_i) if lower else same


def _cumsum_rows(tri, x):
    w = x.shape[1]
    res = _mm(tri, jnp.concatenate(_split3(x), axis=1))
    return res[:, :w] + res[:, w:2 * w] + res[:, 2 * w:]


def _cumsum_lanes(tri, x):
    r = x.shape[0]
    res = _mm_nt(jnp.concatenate(_split3(x), axis=0), tri)
    return res[:r] + res[r:2 * r] + res[2 * r:]


def _softplus(x):
    return jnp.maximum(x, 0.0) + jnp.log1p(jnp.exp(-jnp.abs(x)))


def _padded_tile_inputs(h, x, prefix, tm):
    if h is not None:
        return [h], [pl.BlockSpec((None, tm, h.shape[-1]), lambda i, j: (i, j, 0))]
    d = x.shape[-1]
    nb = tm // A_BLOCK
    specs = [pl.BlockSpec((A_BLOCK, d), lambda i, j: (0, 0))]
    specs += [pl.BlockSpec((None, A_BLOCK, d), lambda i, j, r=r: (i, jnp.maximum(j * nb - 1 + r, 0), 0))
              for r in range(nb)]
    return [prefix] + [x] * nb, specs


def _token_tile_inputs(a, tm):
    nb = tm // A_BLOCK
    first = TOKEN_START // A_BLOCK
    specs = [pl.BlockSpec((None, A_BLOCK, a.shape[-1]), lambda i, j, r=r: (i, first + j * nb + r, 0))
             for r in range(nb)]
    return [a] * nb, specs


def _read_tile(refs, from_x):
    if from_x:
        first = jnp.where(pl.program_id(1) == 0, refs[0][...], refs[1][...])
        return jnp.concatenate([first] + [r[...] for r in refs[2:]], axis=0)
    if len(refs) == 1:
        return refs[0][...]
    return jnp.concatenate([r[...] for r in refs], axis=0)


def _proj_kernel(*refs, n_in, from_x):
    w_ref, oa_ref, om_ref, or_ref = refs[n_in:]
    x = _read_tile(refs[:n_in], from_x)
    acc = jnp.dot(x.astype(BF16), w_ref[...], preferred_element_type=F32)
    oa_ref[...] = acc[:, :PA_W]
    om_ref[...] = acc[:, PA_W:PA_W + PM_W]
    or_ref[...] = acc[:, PA_W + PM_W:]


def _layer_spec(a, l, **kw):
    return pl.BlockSpec((None,) + a.shape[1:], lambda i, j: (l,) + (0,) * (a.ndim - 1), **kw)


def _proj(h, x, prefix, w, l, lp):
    b = x.shape[0]
    d, n = w.shape[1:]
    tm = _row_tile(lp)
    arrays, specs = _padded_tile_inputs(h, x, prefix, tm)
    spec = lambda width: pl.BlockSpec((None, tm, width), lambda i, j: (i, j, 0))
    return pl.pallas_call(
        functools.partial(_proj_kernel, n_in=len(arrays), from_x=h is None),
        grid=(b, lp // tm),
        in_specs=specs + [_layer_spec(w, l, pipeline_mode=pl.Buffered(1))],
        out_specs=[spec(PA_W), spec(PM_W), spec(PR_W)],
        out_shape=[jax.ShapeDtypeStruct((b, lp, PA_W), F32),
                   jax.ShapeDtypeStruct((b, lp, PM_W), F32),
                   jax.ShapeDtypeStruct((b, lp, PR_W), F32)],
        compiler_params=_cparams(("arbitrary", "arbitrary")),
        name="in_proj",
    )(*arrays, w)


def _mlstm_stages(pm_ref, bias_ref, nw_ref, out_ref, cst_ref, m_ref):
    blk = pl.program_id(1)
    tb = pm_ref.shape[0]
    ch = M_CHUNK
    hd = HEAD_DIM
    nch = tb // ch
    npair = M_HEADS // 2
    assert 2 * hd == LANES and 2 * ch == LANES
    yield 8

    @pl.when(blk == 0)
    def _():
        cst_ref[...] = jnp.zeros_like(cst_ref)
        m_ref[...] = jnp.full_like(m_ref, NEG)

    row = blk * tb + _iota2((tb, 1), 0)
    is_pad = row < FRONT_PAD
    gb = pm_ref[:, 4 * M_WIDTH:] + bias_ref[...]
    lane = _iota2((tb, LANES), 1)
    li = jnp.where(is_pad, NEG, gb)
    lf = jnp.where(is_pad, 0.0, -_softplus(-gb))
    gcol = jnp.where(lane < M_HEADS, li, jnp.where(lane < 2 * M_HEADS, lf, 0.0))
    sel = (_iota2((8, LANES), 0) == _iota2((8, LANES), 1)).astype(F32)
    grow = _mm_nt(jnp.concatenate([sel] * 3, axis=1), jnp.concatenate(_split3(gcol), axis=1))
    brow = _cumsum_lanes(_chunk_mask(tb, ch, True).astype(F32), grow)
    r8 = _iota2((8, tb), 0)
    t8 = jnp.bitwise_and(_iota2((8, tb), 1), ch - 1)
    cm = grow - pltpu.roll(brow, M_HEADS, axis=0)
    sh = 1
    while sh < ch:
        cm = jnp.where(t8 >= sh, jnp.maximum(cm, pltpu.roll(cm, sh, axis=1)), cm)
        sh *= 2
    rows_lb = jnp.where(r8 < M_HEADS, grow, brow)
    x16 = jnp.concatenate([rows_lb, cm], axis=0)
    nrep = 3 * M_WIDTH
    e_sel = (_iota2((16, nrep), 0) == jnp.right_shift(_iota2((16, nrep), 1), hd.bit_length() - 1)
             ).astype(F32)
    rep = _mm_tn(jnp.concatenate(_split3(x16), axis=0), jnp.concatenate([e_sel] * 3, axis=0))
    rep_of = lambda qn, j, cs: rep[cs, qn * M_WIDTH + j * LANES:qn * M_WIDTH + (j + 1) * LANES]
    rows_up = pltpu.roll(rows_lb, ch, axis=1)
    rows_dn = pltpu.roll(rows_lb, tb - ch, axis=1)
    low1 = _iota2((1, LANES), 1) < hd

    def row_pair(base, c, j):
        ts = slice((c // 2) * LANES, (c // 2 + 1) * LANES)
        h0, h1 = base + 2 * j, base + 2 * j + 1
        if c % 2 == 0:
            return jnp.where(low1, rows_lb[h0:h0 + 1, ts], rows_up[h1:h1 + 1, ts])
        return jnp.where(low1, rows_dn[h0:h0 + 1, ts], rows_lb[h1:h1 + 1, ts])

    low = _iota2((ch, LANES), 1) < hd
    t_i = _iota2((ch, LANES), 0)
    causal2 = t_i >= jnp.bitwise_and(_iota2((ch, LANES), 1), ch - 1)
    r128 = _iota2((LANES, LANES), 0)
    c128 = _iota2((LANES, LANES), 1)
    ones_bd = ((r128 < hd) == (c128 < hd)).astype(F32)
    bd2 = jnp.concatenate([ones_bd, ones_bd], axis=1) > 0.5
    ones_t = jnp.ones((ch, LANES), F32)

    probs = [(c, j) for c in range(nch) for j in range(npair)]
    q_p, k_p, v_p = {}, {}, {}
    for c, j in probs:
        cs = slice(c * ch, (c + 1) * ch)
        ps = slice(j * LANES, (j + 1) * LANES)
        q_p[c, j] = pm_ref[cs, ps]
        k_p[c, j] = pm_ref[cs, M_WIDTH + ps.start:M_WIDTH + ps.stop] * (hd ** -0.5)
        v_p[c, j] = pm_ref[cs, 2 * M_WIDTH + ps.start:2 * M_WIDTH + ps.stop]
    bdiag = lambda a: jnp.concatenate([jnp.where(low, a, 0.0), jnp.where(low, 0.0, a)], axis=0)
    yield
    qk = {p: _mm_nt(q_p[p], bdiag(k_p[p])) for p in probs}
    yield

    dw_p, sint_p, emt_p, wa_p, decay_p = {}, {}, {}, {}, {}
    for j in range(npair):
        m_prev = m_ref[j][0:1, :]
        for c in range(nch):
            cs = slice(c * ch, (c + 1) * ch)
            li_c, b_c, cm_c = rep_of(0, j, cs), rep_of(1, j, cs), rep_of(2, j, cs)
            d_mat = jnp.where(causal2, b_c - row_pair(M_HEADS, c, j) + row_pair(0, c, j), NEG)
            inter = b_c + m_prev
            m_t = jnp.maximum(b_c + cm_c, inter)
            dw_p[c, j] = jnp.where(causal2, jnp.exp(d_mat - m_t), 0.0)
            sint_p[c, j] = jnp.exp(inter - m_t)
            emt_p[c, j] = jnp.exp(-m_t)
            g = b_c[ch - 1:ch, :]
            m_new = jnp.maximum(g + m_prev, g + cm_c[ch - 1:ch, :])
            wa_p[c, j] = jnp.exp(g - b_c + li_c - m_new)
            decay_p[c, j] = jnp.exp(g + m_prev - m_new)
            m_prev = m_new
        m_ref[j] = jnp.broadcast_to(m_prev, m_ref.shape[1:])

    yield
    upd = {p: jnp.where(bd2, _mm_tn(wa_p[p] * k_p[p], jnp.concatenate([v_p[p], ones_t], axis=1)), 0.0)
           for p in probs}
    yield
    st_p = {}
    for j in range(npair):
        st = cst_ref[j]
        for c in range(nch):
            st_p[c, j] = st
            dec = decay_p[c, j]
            st = jnp.concatenate([dec, dec], axis=1) * st + upd[c, j]
        cst_ref[j] = st
    yield
    inter_p = {p: _mm(q_p[p], st_p[p]) for p in probs}
    yield
    intra_p = {p: _mm(qk[p] * dw_p[p], jnp.concatenate([bdiag(v_p[p]), ones_bd], axis=1)) for p in probs}
    yield

    rows = []
    for c in range(nch):
        pairs = []
        for j in range(npair):
            p = (c, j)
            tot = intra_p[p] + jnp.concatenate([sint_p[p], sint_p[p]], axis=1) * inter_p[p]
            pairs.append(tot[:, :LANES] / jnp.maximum(jnp.abs(tot[:, LANES:]), emt_p[p]))
        rows.append(jnp.concatenate(pairs, axis=1))
    hh = jnp.concatenate(rows, axis=0)

    head_ones = _chunk_mask(M_WIDTH, hd, False).astype(BF16)
    mu = _mm(hh, head_ones) * (1.0 / hd)
    hc = hh - mu
    var = _mm(hc * hc, head_ones) * (1.0 / hd)
    o_pre = pm_ref[:, 3 * M_WIDTH:4 * M_WIDTH]
    out_ref[...] = hc * lax.rsqrt(var + M_NORM_EPS) * nw_ref[...] * jax.nn.sigmoid(o_pre)


def _swap_pairs(w, axis=0):
    split = w.shape[:axis] + (R_HEADS // 2, 2, HEAD_DIM) + w.shape[axis + 1:]
    return jnp.flip(w.reshape(split), axis=axis + 1).reshape(w.shape)


def _rwkv_stages(pr_ref, prev_ref, mu_ref, w0_ref, w2_ref, a0_ref, a2_ref, g2_ref,
                 kk_ref, ka_ref, rk_ref, lnx_ref, out_ref, st_ref):
    blk = pl.program_id(1)
    tb = pr_ref.shape[0]
    ch = R_CHUNK
    rw = R_WIDTH
    hd = HEAD_DIM
    nch = tb // ch
    assert ch == LANES and 2 * hd == LANES
    n_levels = ch.bit_length() - 1
    yield 6 + n_levels + nch

    @pl.when(blk == 0)
    def _():
        st_ref[...] = jnp.zeros_like(st_ref)

    x = pr_ref[...]
    last = jnp.where(blk == 0, 0.0, prev_ref[7:8, :])
    prev = jnp.where(_iota2((tb, 1), 0) == 0, last, pltpu.roll(x, 1, axis=0))
    t = x + (prev - x) * mu_ref[...]
    r = t[:, 0:rw]
    k = t[:, rw:2 * rw]
    v = t[:, 2 * rw:3 * rw]
    o = 3 * rw
    w_lat = t[:, o:o + R_DECAY_RANK]
    a_lat = t[:, o + R_DECAY_RANK:o + R_DECAY_RANK + R_A_RANK]
    g_lat = t[:, o + R_DECAY_RANK + R_A_RANK:]

    w_log = -_softplus(-(w0_ref[...] + _mm(jnp.tanh(w_lat), w2_ref[...]))) - 0.5
    lw = -jnp.exp(w_log)
    a = jax.nn.sigmoid(a0_ref[...] + _mm(a_lat, a2_ref[...]))
    g = _mm(jax.nn.sigmoid(g_lat), g2_ref[...])

    head_ones = _chunk_mask(rw, hd, False).astype(BF16)
    sh = hd.bit_length() - 1
    swap_ones = ((jnp.right_shift(_iota2((rw, rw), 0), sh) ^ 1)
                 == jnp.right_shift(_iota2((rw, rw), 1), sh)).astype(BF16)
    kk = k * kk_ref[...]
    kk = kk / jnp.maximum(jnp.sqrt(_mm(kk * kk, head_ones)), 1e-12)
    k2 = k * (1.0 + (a - 1.0) * ka_ref[...])
    avec = -kk
    bvec = kk * a
    cum = _cumsum_rows(_chunk_mask(tb, ch, True).astype(F32), lw)

    r_i = _iota2((ch, ch), 0)
    c_i = _iota2((ch, ch), 1)
    strict = r_i > c_i
    eye = r_i == c_i
    incl2 = jnp.concatenate([r_i >= c_i] * 2, axis=1)
    low_cols = c_i < hd
    low_rows = r_i < hd

    probs = [(c, h) for c in range(nch) for h in range(R_HEADS)]
    at_p, rt_p, ac_p, rc_p, vo_p, kb_t, kb_p, wc_p = ({} for _ in range(8))
    for c in range(nch):
        cs = slice(c * ch, (c + 1) * ch)
        cum_c = cum[cs]
        mid = cum_c[ch // 2 - 1:ch // 2, :]
        cl = cum_c[ch - 1:ch, :]
        e_mid = jnp.exp(-mid)
        a_true = avec[cs] * jnp.exp(cum_c - lw[cs])
        r_true = r[cs] * jnp.exp(cum_c)
        a_cen = a_true * e_mid
        r_cen = r_true * e_mid
        inv = jnp.exp(mid - cum_c)
        rel = jnp.exp(cl - cum_c)
        kt_c, bt_c = k2[cs] * inv, bvec[cs] * inv
        kp_c, bp_c = k2[cs] * rel, bvec[cs] * rel
        wc_c = jnp.exp(cl)
        v_c = v[cs]
        for h in range(R_HEADS):
            ps = slice((h // 2) * LANES, (h // 2 + 1) * LANES)
            mine = low_cols if h % 2 == 0 else jnp.logical_not(low_cols)
            key = (c, h)
            at_p[key] = jnp.where(mine, a_true[:, ps], 0.0)
            rt_p[key] = jnp.where(mine, r_true[:, ps], 0.0)
            ac_p[key] = _bf(jnp.where(mine, a_cen[:, ps], 0.0))
            rc_p[key] = _bf(jnp.where(mine, r_cen[:, ps], 0.0))
            vo_p[key] = _bf(jnp.where(mine, 0.0, v_c[:, ps]))
            kb_t[key] = _bf(jnp.concatenate([kt_c[:, ps], bt_c[:, ps]], axis=0))
            kb_p[key] = _bf(jnp.concatenate([kp_c[:, ps], bp_c[:, ps]], axis=0))
            wc_p[key] = wc_c[:, ps]

    yield
    gram = {p: _mm_nt(jnp.concatenate([ac_p[p], rc_p[p]], axis=0), kb_t[p]) for p in probs}
    m_ak = {p: _bf(jnp.where(strict, gram[p][:ch, :ch], 0.0)) for p in probs}
    pw = {p: _bf(jnp.where(strict, gram[p][:ch, ch:], 0.0)) for p in probs}
    n_cat = {p: _bf(jnp.where(incl2, gram[p][ch:], 0.0)) for p in probs}
    yield
    z = {p: at_p[p] + _mm(m_ak[p], vo_p[p]) for p in probs}
    yield
    for lvl in range(n_levels):
        if lvl + 1 < n_levels:
            res = {p: _mm(pw[p], jnp.concatenate([_bf(z[p]), pw[p]], axis=1)) for p in probs}
            z = {p: z[p] + res[p][:, :LANES] for p in probs}
            pw = {p: _bf(res[p][:, LANES:]) for p in probs}
        else:
            z = {p: z[p] + _mm(pw[p], z[p]) for p in probs}
        yield
    rhs = {p: jnp.concatenate([vo_p[p], _bf(z[p])], axis=0) for p in probs}
    ry = {p: _mm(n_cat[p], rhs[p]) for p in probs}
    yield
    tg = {p: _mm_tn(kb_p[p], rhs[p]) for p in probs}
    yield

    sts = [st_ref[h] for h in range(R_HEADS)]
    y_chunks = []
    for c in range(nch):
        nxt, g_full = [], []
        for h in range(R_HEADS):
            p = (c, h)
            mine = low_cols if h % 2 == 0 else jnp.logical_not(low_cols)
            my_rows = low_rows if h % 2 == 0 else jnp.logical_not(low_rows)
            r_eff = rt_p[p] + jnp.where(mine, ry[p], 0.0)
            t_full = (jnp.where(my_rows & mine, tg[p], 0.0)
                      + jnp.where(eye & mine, wc_p[p], 0.0))
            g_full.append(jnp.where(my_rows & jnp.logical_not(mine), tg[p], 0.0))
            nxt.append(_mm(jnp.concatenate([r_eff, t_full], axis=0), sts[h]))
        ys = [jnp.where(low_cols if h % 2 else jnp.logical_not(low_cols), nxt[h][:ch] + ry[(c, h)], 0.0)
              for h in range(R_HEADS)]
        y_chunks.append(jnp.concatenate([ys[2 * j] + ys[2 * j + 1] for j in range(R_HEADS // 2)], axis=1))
        sts = [nxt[h][ch:] + g_full[h] for h in range(R_HEADS)]
        yield
    for h in range(R_HEADS):
        st_ref[h] = sts[h]
    y = jnp.concatenate(y_chunks, axis=0)

    inv_d = 1.0 / hd
    mu = _mm(y, head_ones) * inv_d
    yc = y - mu
    var = _mm(yc * yc, head_ones) * inv_d
    yn = yc * lax.rsqrt(var + R_GN_EPS) * lnx_ref[0:1, :] + lnx_ref[1:2, :]
    bonus = _mm(r * k2 * rk_ref[...], swap_ones) * v
    out_ref[...] = (yn + bonus) * g


def _t5_bucket(dist):
    max_exact = N_BUCKETS // 2
    d = jnp.maximum(dist, 1).astype(F32)
    large = max_exact + (jnp.log(d / max_exact) / math.log(MAX_DISTANCE / max_exact)
                         * (N_BUCKETS - max_exact)).astype(jnp.int32)
    large = jnp.minimum(large, N_BUCKETS - 1)
    return jnp.where(dist < max_exact, dist, large)


def _attn_tables(rel_bias):
    blk = A_BLOCK
    i = jnp.arange(blk)[:, None]
    j = jnp.arange(blk)[None, :]
    top = 3 * blk - 1
    period = 5 * blk
    by_dist_rev = rel_bias.astype(F32)[:, _t5_bucket(jnp.maximum(top - jnp.arange(period), 0))]
    shifted = jnp.tile(by_dist_rev, (1, blk))[:, :blk * (period - 1)].reshape(A_HEADS, blk, period - 1)

    def bias(offset):
        return shifted[:, :, top - offset:top - offset + blk]

    neg = jnp.full((A_HEADS, blk, blk), NEG, F32)
    is_meta = j >= FRONT_PAD
    cur_d = i - j
    prev_d = blk + i - j
    cur = jnp.where(cur_d >= 0, bias(0), NEG)
    prev = jnp.where(prev_d < WINDOW, bias(blk), NEG)
    meta0 = jnp.where(is_meta & (cur_d >= 0), bias(0), NEG)
    meta1 = jnp.where(is_meta, bias(blk), NEG)
    meta2 = jnp.where(is_meta, bias(2 * blk), NEG)
    case0 = jnp.concatenate([meta0, neg, neg], axis=-1)
    case1 = jnp.concatenate([meta1, neg, cur], axis=-1)
    case2 = jnp.concatenate([meta2, prev, cur], axis=-1)
    tab = jnp.stack([case0, case1, case2], axis=0)
    tab = tab.reshape(3, A_KV_HEADS, A_GROUP, blk, 3 * blk)
    return tab.transpose(0, 1, 4, 2, 3).reshape(3, A_KV_HEADS, 3 * blk, A_GROUP * blk)


assert A_KV_HEADS * HEAD_DIM == LANES
_PAIRED_HEADS = tuple(kv * A_GROUP + g for g in range(A_GROUP) for kv in range(A_KV_HEADS))


def _pair_heads(w, axis=0):
    split = w.shape[:axis] + (A_KV_HEADS, A_GROUP, HEAD_DIM) + w.shape[axis + 1:]
    return jnp.swapaxes(w.reshape(split), axis, axis + 1).reshape(w.shape)


def _swa_stages(q_ref, kc_ref, kp_ref, km_ref, vc_ref, vp_ref, vm_ref, tab_ref, sink_ref, out_ref):
    scale = HEAD_DIM ** -0.5
    blk = A_BLOCK
    hd = HEAD_DIM
    nb = q_ref.shape[0] // blk
    step = pl.program_id(1)
    kvs = range(A_KV_HEADS)
    upper = _iota2((blk, LANES), 1) >= hd
    yield 4

    v_t = jnp.concatenate([vm_ref[...], vp_ref[...], vc_ref[...]], axis=0).T
    ones = jnp.ones((hd, 3 * blk), F32)

    probs = [(i, kv) for i in range(nb) for kv in kvs]
    s_t = {}
    for i in range(nb):
        rows = slice(i * blk, (i + 1) * blk)
        prev = kp_ref[...] if i == 0 else kc_ref[(i - 1) * blk:i * blk, :]
        keys = jnp.concatenate([km_ref[...], prev, kc_ref[rows, :]], axis=0)
        for kv in kvs:
            mine = upper if kv == 1 else jnp.logical_not(upper)
            q = jnp.concatenate([jnp.where(mine, q_ref[rows, g * LANES:(g + 1) * LANES], 0.0)
                                 for g in range(A_GROUP)], axis=0) * scale
            s_t[i, kv] = _mm_nt(keys, q)
    yield
    p_t, esink = {}, {}
    for i, kv in probs:
        case = 2 if i >= 2 else jnp.minimum(step * nb + i, 2)
        sk = s_t[i, kv] + tab_ref[case, kv]
        sink = sink_ref[kv][0:1, :]
        m = jnp.maximum(jnp.max(sk, axis=0, keepdims=True), sink)
        p_t[i, kv] = _bf(jnp.exp(sk - m))
        esink[i, kv] = jnp.exp(sink - m)
    yield
    o_t = {}
    for i, kv in probs:
        vr = slice(kv * hd, (kv + 1) * hd)
        prev = v_t[vr, blk:2 * blk] if i == 0 else v_t[vr, (i + 1) * blk:(i + 2) * blk]
        vals = jnp.concatenate([v_t[vr, 0:blk], prev, v_t[vr, (i + 2) * blk:(i + 3) * blk]], axis=1)
        r = _mm(jnp.concatenate([vals, ones], axis=0), p_t[i, kv])
        o_t[i, kv] = r[:hd] / (r[hd:] + esink[i, kv])
    yield
    for i in range(nb):
        tiles = [jnp.concatenate([o_t[i, kv][:, g * blk:(g + 1) * blk] for kv in kvs], axis=0).T
                 for g in range(A_GROUP)]
        out_ref[i * blk:(i + 1) * blk, :] = jnp.concatenate(tiles, axis=1)


def _interleave(gens):
    total = [next(g) for g in gens]
    done = [0] * len(gens)
    live = set(range(len(gens)))
    while live:
        i = min(live, key=lambda n: ((done[n] + 1) / total[n], n))
        try:
            next(gens[i])
            done[i] += 1
        except StopIteration:
            live.discard(i)


def _mixers_kernel(*refs, n_m, n_r, n_s):
    m_in, r_in, s_in = refs[:n_m], refs[n_m:n_m + n_r], refs[n_m + n_r:n_m + n_r + n_s]
    ym_ref, yr_ref, ya_ref, cst_ref, m_ref, st_ref = refs[n_m + n_r + n_s:]
    _interleave([_rwkv_stages(*r_in, yr_ref, st_ref),
                 _mlstm_stages(*m_in, ym_ref, cst_ref, m_ref),
                 _swa_stages(*s_in, ya_ref)])


def _mixers(pa, pm, pr, tables, l, sink, bias, norm_w, mu, w0, w2, a0, a2, g2, k_k, k_a, r_k, ln_x):
    b, lp, _ = pa.shape
    blk = A_BLOCK
    tb = _mix_rows(lp)
    nb = tb // blk
    tile = lambda width: pl.BlockSpec((None, tb, width), lambda i, j: (i, j, 0))

    m_in = [pm, bias, norm_w]
    m_specs = [tile(PM_W), _layer_spec(bias, l), _layer_spec(norm_w, l)]

    r_params = [mu, w0, w2, a0, a2, g2, k_k, k_a, r_k, ln_x]
    r_in = [pr, pr] + r_params
    r_specs = [tile(PR_W),
               pl.BlockSpec((None, 8, PR_W), lambda i, j: (i, jnp.maximum(j * (tb // 8) - 1, 0), 0))]
    r_specs += [_layer_spec(a, l) for a in r_params]

    kcol = A_WIDTH // A_KV_WIDTH
    cur_spec = lambda col: pl.BlockSpec((None, tb, A_KV_WIDTH), lambda i, j: (i, j, col))
    prev_spec = lambda col: pl.BlockSpec((None, blk, A_KV_WIDTH),
                                         lambda i, j: (i, jnp.maximum(j * nb - 1, 0), col))
    meta_spec = lambda col: pl.BlockSpec((None, blk, A_KV_WIDTH), lambda i, j: (i, 0, col))
    s_in = [pa] * 7 + [tables, sink]
    s_specs = [tile(A_WIDTH), cur_spec(kcol), prev_spec(kcol), meta_spec(kcol),
               cur_spec(kcol + 1), prev_spec(kcol + 1), meta_spec(kcol + 1),
               pl.BlockSpec(tables.shape, lambda i, j: (0, 0, 0, 0), pipeline_mode=pl.Buffered(1)),
               _layer_spec(sink, l)]

    return pl.pallas_call(
        functools.partial(_mixers_kernel, n_m=len(m_in), n_r=len(r_in), n_s=len(s_in)),
        grid=(b, lp // tb),
        in_specs=m_specs + r_specs + s_specs,
        out_specs=[tile(M_WIDTH), tile(R_WIDTH), tile(A_WIDTH)],
        out_shape=[jax.ShapeDtypeStruct((b, lp, M_WIDTH), F32),
                   jax.ShapeDtypeStruct((b, lp, R_WIDTH), F32),
                   jax.ShapeDtypeStruct((b, lp, A_WIDTH), F32)],
        scratch_shapes=[pltpu.VMEM((M_HEADS // 2, LANES, 2 * LANES), F32),
                        pltpu.VMEM((M_HEADS // 2, 8, LANES), F32),
                        pltpu.VMEM((R_HEADS, LANES, LANES), F32)],
        compiler_params=_cparams(("arbitrary", "arbitrary")),
        name="mixers",
    )(*m_in, *r_in, *s_in)


def _layer_norm_rows(z, ln_ref):
    mu = jnp.mean(z, axis=-1, keepdims=True)
    zc = z - mu
    var = jnp.mean(zc * zc, axis=-1, keepdims=True)
    return zc * lax.rsqrt(var + LN_EPS) * ln_ref[0:1, :] + ln_ref[1:2, :]


FF_CHUNK = 256


def _post_kernel(*refs, n_each, from_x, final, alpha, d_ff):
    groups, pos = [], 0
    for n in n_each:
        groups.append(refs[pos:pos + n])
        pos += n
    w_ref, lnm_ref, wi_ref, wo_ref, lnf_ref, out_ref = refs[pos:]
    ym, yr, ya = (_read_tile(g, False) for g in groups[:3])
    h = _read_tile(groups[3], from_x)
    tm = h.shape[0]
    y_cat = jnp.concatenate([ym.astype(BF16), yr.astype(BF16), ya.astype(BF16)], axis=1)
    mix = jnp.dot(y_cat, w_ref[...], preferred_element_type=F32)
    x = _layer_norm_rows(alpha * h + mix, lnm_ref)
    if not final:
        is_pad = (pl.program_id(1) * tm + _iota2((tm, 1), 0)) < FRONT_PAD
        x = jnp.where(is_pad, 0.0, x)
    xb = x.astype(BF16)
    acc = alpha * x
    for j in range(d_ff // FF_CHUNK):
        cs = slice(j * FF_CHUNK, (j + 1) * FF_CHUNK)
        us = slice(d_ff + j * FF_CHUNK, d_ff + (j + 1) * FF_CHUNK)
        gate = jnp.dot(xb, wi_ref[:, cs], preferred_element_type=F32)
        up = jnp.dot(xb, wi_ref[:, us], preferred_element_type=F32)
        act = (gate * jax.nn.sigmoid(gate) * up).astype(BF16)
        acc += jnp.dot(act, wo_ref[cs, :], preferred_element_type=F32)
    y = _layer_norm_rows(acc, lnf_ref)
    out_ref[...] = y if final else jnp.where(is_pad, 0.0, y)


def _post(ym, yr, ya, h, x, prefix, w, ln_mix, wi, wo, ln_ffn, l, alpha, final):
    b, lp, _ = ym.shape
    seq, d = x.shape[1:]
    d_ff = wo.shape[1]
    assert d_ff % FF_CHUNK == 0
    from_x = h is None
    if final:
        tm = _row_tile(seq)
        groups = [_token_tile_inputs(a, tm) for a in (ym, yr, ya)]
        groups.append(([x], [pl.BlockSpec((None, tm, d), lambda i, j: (i, j, 0))]) if from_x
                      else _token_tile_inputs(h, tm))
        rows_out = seq
    else:
        tm = _row_tile(lp)
        groups = [_padded_tile_inputs(a, None, None, tm) for a in (ym, yr, ya)]
        groups.append(_padded_tile_inputs(h, x, prefix, tm))
        rows_out = lp
    arrays = [a for g in groups for a in g[0]]
    specs = [s for g in groups for s in g[1]]
    const = lambda a: _layer_spec(a, l, pipeline_mode=pl.Buffered(1))
    return pl.pallas_call(
        functools.partial(_post_kernel, n_each=tuple(len(g[0]) for g in groups),
                          from_x=from_x and not final, final=final, alpha=alpha, d_ff=d_ff),
        grid=(b, rows_out // tm),
        in_specs=specs + [const(w), _layer_spec(ln_mix, l), const(wi), const(wo), _layer_spec(ln_ffn, l)],
        out_specs=pl.BlockSpec((None, tm, d), lambda i, j: (i, j, 0)),
        out_shape=jax.ShapeDtypeStruct((b, rows_out, d), F32),
        compiler_params=_cparams(("arbitrary", "arbitrary")),
        name="out_proj_ffn",
    )(*arrays, w, ln_mix, wi, wo, ln_ffn)


def kernel(x, meta_tokens, rel_bias, w_in, m_gate_bias, m_norm_w, r_mu_rkv, r_mu_w, r_mu_a, r_mu_g, r_w0, r_w2, r_a0, r_a2, r_g2, r_k_k, r_k_a, r_r_k, r_ln_x, a_sinks, w_out, ln_mix, w_ff_in, w_ff_out, ln_ffn):
    b, seq, d = x.shape
    depth = w_in.shape[0]
    alpha = (2 * depth) ** 0.25
    assert (TOKEN_START + seq) % A_BLOCK == 0

    lp = TOKEN_START + seq
    prefix = jnp.concatenate([jnp.zeros((FRONT_PAD, d), x.dtype), meta_tokens.astype(x.dtype)], axis=0)
    tables = _attn_tables(rel_bias)

    m_end = PA_W + M_RAW
    rv0 = m_end + 2 * R_WIDTH
    w_pad = jnp.concatenate(
        [_pair_heads(w_in[:, :, :A_WIDTH], 2), w_in[:, :, A_WIDTH:m_end],
         jnp.zeros((depth, d, PM_W - M_RAW), w_in.dtype), w_in[:, :, m_end:rv0],
         _swap_pairs(w_in[:, :, rv0:rv0 + R_WIDTH], 2), w_in[:, :, rv0 + R_WIDTH:]], axis=2).astype(BF16)
    w_o = jnp.concatenate([w_out[:, :M_WIDTH], _swap_pairs(w_out[:, M_WIDTH:M_WIDTH + R_WIDTH], 1),
                           _pair_heads(w_out[:, M_WIDTH + R_WIDTH:], 1)], axis=1).astype(BF16)
    w_fi, w_fo = w_ff_in.astype(BF16), w_ff_out.astype(BF16)
    rows = lambda a: a.reshape(depth, 1, -1)
    mu = jnp.concatenate([r_mu_rkv[:, 0], r_mu_rkv[:, 1], _swap_pairs(r_mu_rkv[:, 2], 1),
                          r_mu_w, r_mu_a, r_mu_g], axis=1)
    bias = jnp.pad(rows(m_gate_bias.astype(F32)), ((0, 0), (0, 0), (0, LANES - 2 * M_HEADS)))
    sink = jnp.broadcast_to(
        jnp.repeat(a_sinks.astype(F32).reshape(depth, A_KV_HEADS, 1, A_GROUP), A_BLOCK, axis=3),
        (depth, A_KV_HEADS, 8, A_GROUP * A_BLOCK))
    mixer_params = (sink, bias, rows(m_norm_w), rows(mu), rows(r_w0), r_w2, rows(r_a0), r_a2,
                    _swap_pairs(r_g2, 2), rows(r_k_k), rows(r_k_a), rows(r_r_k), _swap_pairs(r_ln_x, 2))

    h = None
    for l in range(depth):
        pa, pm, pr = _proj(h, x, prefix, w_pad, l, lp)
        y_m, y_r, y_a = _mixers(pa, pm, pr, tables, l, *mixer_params)
        h = _post(y_m, y_r, y_a, h, x, prefix, w_o, ln_mix, w_fi, w_fo, ln_ffn, l, alpha,
                  final=l == depth - 1)
    return h
```

```python
import functools
import math

import jax
import jax.numpy as jnp
from jax import lax
from jax.experimental import pallas as pl
from jax.experimental.pallas import tpu as pltpu

F32 = jnp.float32
BF16 = jnp.bfloat16

HEAD_DIM = 64
N_META = 16
M_HEADS = 4
M_WIDTH = M_HEADS * HEAD_DIM
M_CHUNK = 64
M_NORM_EPS = 1e-6
R_HEADS = 4
R_WIDTH = R_HEADS * HEAD_DIM
R_CHUNK = 128
R_DECAY_RANK = 32
R_A_RANK = 32
R_GATE_RANK = 64
R_GN_EPS = 64e-5
A_HEADS = 8
A_KV_HEADS = 2
A_GROUP = A_HEADS // A_KV_HEADS
A_WIDTH = A_HEADS * HEAD_DIM
A_KV_WIDTH = A_KV_HEADS * HEAD_DIM
WINDOW = 128
A_BLOCK = 128
N_BUCKETS = 32
MAX_DISTANCE = 128
LN_EPS = 1e-5
NEG = -1e30

LANES = 128
FRONT_PAD = A_BLOCK - N_META
TOKEN_START = FRONT_PAD + N_META

PA_W = A_WIDTH + 2 * A_KV_WIDTH
PM_W = 4 * M_WIDTH + LANES
PR_W = 3 * R_WIDTH + R_DECAY_RANK + R_A_RANK + R_GATE_RANK
M_RAW = 4 * M_WIDTH + 2 * M_HEADS

VMEM_LIMIT = 56 * 1024 * 1024
MIX_ROWS = 640


def _row_tile(n):
    for t in (640, 512, 256, 128, 64):
        if n % t == 0:
            return t
    raise ValueError(f"row count {n} has no supported tile")


def _mix_rows(lp):
    return MIX_ROWS if lp % MIX_ROWS == 0 else A_BLOCK


def _cparams(sem):
    return pltpu.CompilerParams(dimension_semantics=sem, vmem_limit_bytes=VMEM_LIMIT)


def _bf(x):
    return x.astype(BF16)


def _mm(a, b):
    return lax.dot_general(_bf(a), _bf(b), (((1,), (0,)), ((), ())), preferred_element_type=F32)


def _mm_nt(a, b):
    return lax.dot_general(_bf(a), _bf(b), (((1,), (1,)), ((), ())), preferred_element_type=F32)


def _mm_tn(a, b):
    return lax.dot_general(_bf(a), _bf(b), (((0,), (0,)), ((), ())), preferred_element_type=F32)


def _iota2(shape, dim):
    return lax.broadcasted_iota(jnp.int32, shape, dim)


def _split3(x):
    hi = _bf(x).astype(F32)
    r1 = x - hi
    mid = _bf(r1).astype(F32)
    return hi, mid, r1 - mid


def _chunk_mask(n, chunk, lower):
    r_i = _iota2((n, n), 0)
    c_i = _iota2((n, n), 1)
    sh = chunk.bit_length() - 1
    same = jnp.right_shift(r_i, sh) == jnp.right_shift(c_i, sh)
    return same & (r_i >= c_i) if lower else same


def _cumsum_rows(tri, x):
    w = x.shape[1]
    res = _mm(tri, jnp.concatenate(_split3(x), axis=1))
    return res[:, :w] + res[:, w:2 * w] + res[:, 2 * w:]


def _cumsum_lanes(tri, x):
    r = x.shape[0]
    res = _mm_nt(jnp.concatenate(_split3(x), axis=0), tri)
    return res[:r] + res[r:2 * r] + res[2 * r:]


def _softplus(x):
    return jnp.maximum(x, 0.0) + jnp.log1p(jnp.exp(-jnp.abs(x)))


def _padded_tile_inputs(h, x, prefix, tm):
    if h is not None:
        return [h], [pl.BlockSpec((None, tm, h.shape[-1]), lambda i, j: (i, j, 0))]
    d = x.shape[-1]
    nb = tm // A_BLOCK
    specs = [pl.BlockSpec((A_BLOCK, d), lambda i, j: (0, 0))]
    specs += [pl.BlockSpec((None, A_BLOCK, d), lambda i, j, r=r: (i, jnp.maximum(j * nb - 1 + r, 0), 0))
              for r in range(nb)]
    return [prefix] + [x] * nb, specs


def _token_tile_inputs(a, tm):
    nb = tm // A_BLOCK
    first = TOKEN_START // A_BLOCK
    specs = [pl.BlockSpec((None, A_BLOCK, a.shape[-1]), lambda i, j, r=r: (i, first + j * nb + r, 0))
             for r in range(nb)]
    return [a] * nb, specs


def _read_tile(refs, from_x):
    if from_x:
        first = jnp.where(pl.program_id(1) == 0, refs[0][...], refs[1][...])
        return jnp.concatenate([first] + [r[...] for r in refs[2:]], axis=0)
    if len(refs) == 1:
        return refs[0][...]
    return jnp.concatenate([r[...] for r in refs], axis=0)


def _proj_kernel(*refs, n_in, from_x):
    w_ref, oa_ref, om_ref, or_ref = refs[n_in:]
    x = _read_tile(refs[:n_in], from_x)
    acc = jnp.dot(x.astype(BF16), w_ref[...], preferred_element_type=F32)
    oa_ref[...] = acc[:, :PA_W]
    om_ref[...] = acc[:, PA_W:PA_W + PM_W]
    or_ref[...] = acc[:, PA_W + PM_W:]


def _layer_spec(a, l, **kw):
    return pl.BlockSpec((None,) + a.shape[1:], lambda i, j: (l,) + (0,) * (a.ndim - 1), **kw)


def _proj(h, x, prefix, w, l, lp):
    b = x.shape[0]
    d, n = w.shape[1:]
    tm = _row_tile(lp)
    arrays, specs = _padded_tile_inputs(h, x, prefix, tm)
    spec = lambda width: pl.BlockSpec((None, tm, width), lambda i, j: (i, j, 0))
    return pl.pallas_call(
        functools.partial(_proj_kernel, n_in=len(arrays), from_x=h is None),
        grid=(b, lp // tm),
        in_specs=specs + [_layer_spec(w, l, pipeline_mode=pl.Buffered(1))],
        out_specs=[spec(PA_W), spec(PM_W), spec(PR_W)],
        out_shape=[jax.ShapeDtypeStruct((b, lp, PA_W), F32),
                   jax.ShapeDtypeStruct((b, lp, PM_W), F32),
                   jax.ShapeDtypeStruct((b, lp, PR_W), F32)],
        compiler_params=_cparams(("arbitrary", "arbitrary")),
        name="in_proj",
    )(*arrays, w)


def _mlstm_stages(pm_ref, bias_ref, nw_ref, out_ref, cst_ref, m_ref):
    blk = pl.program_id(1)
    tb = pm_ref.shape[0]
    ch = M_CHUNK
    hd = HEAD_DIM
    nch = tb // ch
    npair = M_HEADS // 2
    assert 2 * hd == LANES and 2 * ch == LANES
    yield 8

    @pl.when(blk == 0)
    def _():
        cst_ref[...] = jnp.zeros_like(cst_ref)
        m_ref[...] = jnp.full_like(m_ref, NEG)

    row = blk * tb + _iota2((tb, 1), 0)
    is_pad = row < FRONT_PAD
    gb = pm_ref[:, 4 * M_WIDTH:] + bias_ref[...]
    lane = _iota2((tb, LANES), 1)
    li = jnp.where(is_pad, NEG, gb)
    lf = jnp.where(is_pad, 0.0, -_softplus(-gb))
    gcol = jnp.where(lane < M_HEADS, li, jnp.where(lane < 2 * M_HEADS, lf, 0.0))
    sel = (_iota2((8, LANES), 0) == _iota2((8, LANES), 1)).astype(F32)
    grow = _mm_nt(jnp.concatenate([sel] * 3, axis=1), jnp.concatenate(_split3(gcol), axis=1))
    brow = _cumsum_lanes(_chunk_mask(tb, ch, True).astype(F32), grow)
    r8 = _iota2((8, tb), 0)
    t8 = jnp.bitwise_and(_iota2((8, tb), 1), ch - 1)
    cm = grow - pltpu.roll(brow, M_HEADS, axis=0)
    sh = 1
    while sh < ch:
        cm = jnp.where(t8 >= sh, jnp.maximum(cm, pltpu.roll(cm, sh, axis=1)), cm)
        sh *= 2
    rows_lb = jnp.where(r8 < M_HEADS, grow, brow)
    x16 = jnp.concatenate([rows_lb, cm], axis=0)
    nrep = 3 * M_WIDTH
    e_sel = (_iota2((16, nrep), 0) == jnp.right_shift(_iota2((16, nrep), 1), hd.bit_length() - 1)
             ).astype(F32)
    rep = _mm_tn(jnp.concatenate(_split3(x16), axis=0), jnp.concatenate([e_sel] * 3, axis=0))
    rep_of = lambda qn, j, cs: rep[cs, qn * M_WIDTH + j * LANES:qn * M_WIDTH + (j + 1) * LANES]
    rows_up = pltpu.roll(rows_lb, ch, axis=1)
    rows_dn = pltpu.roll(rows_lb, tb - ch, axis=1)
    low1 = _iota2((1, LANES), 1) < hd

    def row_pair(base, c, j):
        ts = slice((c // 2) * LANES, (c // 2 + 1) * LANES)
        h0, h1 = base + 2 * j, base + 2 * j + 1
        if c % 2 == 0:
            return jnp.where(low1, rows_lb[h0:h0 + 1, ts], rows_up[h1:h1 + 1, ts])
        return jnp.where(low1, rows_dn[h0:h0 + 1, ts], rows_lb[h1:h1 + 1, ts])

    low = _iota2((ch, LANES), 1) < hd
    t_i = _iota2((ch, LANES), 0)
    causal2 = t_i >= jnp.bitwise_and(_iota2((ch, LANES), 1), ch - 1)
    r128 = _iota2((LANES, LANES), 0)
    c128 = _iota2((LANES, LANES), 1)
    ones_bd = ((r128 < hd) == (c128 < hd)).astype(F32)
    bd2 = jnp.concatenate([ones_bd, ones_bd], axis=1) > 0.5
    ones_t = jnp.ones((ch, LANES), F32)

    probs = [(c, j) for c in range(nch) for j in range(npair)]
    q_p, k_p, v_p = {}, {}, {}
    for c, j in probs:
        cs = slice(c * ch, (c + 1) * ch)
        ps = slice(j * LANES, (j + 1) * LANES)
        q_p[c, j] = pm_ref[cs, ps]
        k_p[c, j] = pm_ref[cs, M_WIDTH + ps.start:M_WIDTH + ps.stop] * (hd ** -0.5)
        v_p[c, j] = pm_ref[cs, 2 * M_WIDTH + ps.start:2 * M_WIDTH + ps.stop]
    bdiag = lambda a: jnp.concatenate([jnp.where(low, a, 0.0), jnp.where(low, 0.0, a)], axis=0)
    yield
    qk = {p: _mm_nt(q_p[p], bdiag(k_p[p])) for p in probs}
    yield

    dw_p, sint_p, emt_p, wa_p, decay_p = {}, {}, {}, {}, {}
    for j in range(npair):
        m_prev = m_ref[j][0:1, :]
        for c in range(nch):
            cs = slice(c * ch, (c + 1) * ch)
            li_c, b_c, cm_c = rep_of(0, j, cs), rep_of(1, j, cs), rep_of(2, j, cs)
            d_mat = jnp.where(causal2, b_c - row_pair(M_HEADS, c, j) + row_pair(0, c, j), NEG)
            inter = b_c + m_prev
            m_t = jnp.maximum(b_c + cm_c, inter)
            dw_p[c, j] = jnp.where(causal2, jnp.exp(d_mat - m_t), 0.0)
            sint_p[c, j] = jnp.exp(inter - m_t)
            emt_p[c, j] = jnp.exp(-m_t)
            g = b_c[ch - 1:ch, :]
            m_new = jnp.maximum(g + m_prev, g + cm_c[ch - 1:ch, :])
            wa_p[c, j] = jnp.exp(g - b_c + li_c - m_new)
            decay_p[c, j] = jnp.exp(g + m_prev - m_new)
            m_prev = m_new
        m_ref[j] = jnp.broadcast_to(m_prev, m_ref.shape[1:])

    yield
    upd = {p: jnp.where(bd2, _mm_tn(wa_p[p] * k_p[p], jnp.concatenate([v_p[p], ones_t], axis=1)), 0.0)
           for p in probs}
    yield
    st_p = {}
    for j in range(npair):
        st = cst_ref[j]
        for c in range(nch):
            st_p[c, j] = st
            dec = decay_p[c, j]
            st = jnp.concatenate([dec, dec], axis=1) * st + upd[c, j]
        cst_ref[j] = st
    yield
    inter_p = {p: _mm(q_p[p], st_p[p]) for p in probs}
    yield
    intra_p = {p: _mm(qk[p] * dw_p[p], jnp.concatenate([bdiag(v_p[p]), ones_bd], axis=1)) for p in probs}
    yield

    rows = []
    for c in range(nch):
        pairs = []
        for j in range(npair):
            p = (c, j)
            tot = intra_p[p] + jnp.concatenate([sint_p[p], sint_p[p]], axis=1) * inter_p[p]
            pairs.append(tot[:, :LANES] / jnp.maximum(jnp.abs(tot[:, LANES:]), emt_p[p]))
        rows.append(jnp.concatenate(pairs, axis=1))
    hh = jnp.concatenate(rows, axis=0)

    head_ones = _chunk_mask(M_WIDTH, hd, False).astype(BF16)
    mu = _mm(hh, head_ones) * (1.0 / hd)
    hc = hh - mu
    var = _mm(hc * hc, head_ones) * (1.0 / hd)
    o_pre = pm_ref[:, 3 * M_WIDTH:4 * M_WIDTH]
    out_ref[...] = hc * lax.rsqrt(var + M_NORM_EPS) * nw_ref[...] * jax.nn.sigmoid(o_pre)


def _swap_pairs(w, axis=0):
    split = w.shape[:axis] + (R_HEADS // 2, 2, HEAD_DIM) + w.shape[axis + 1:]
    return jnp.flip(w.reshape(split), axis=axis + 1).reshape(w.shape)


def _rwkv_stages(pr_ref, prev_ref, mu_ref, w0_ref, w2_ref, a0_ref, a2_ref, g2_ref,
                 kk_ref, ka_ref, rk_ref, lnx_ref, out_ref, st_ref):
    blk = pl.program_id(1)
    tb = pr_ref.shape[0]
    ch = R_CHUNK
    rw = R_WIDTH
    hd = HEAD_DIM
    nch = tb // ch
    assert ch == LANES and 2 * hd == LANES
    n_levels = ch.bit_length() - 1
    yield 6 + n_levels + nch

    @pl.when(blk == 0)
    def _():
        st_ref[...] = jnp.zeros_like(st_ref)

    x = pr_ref[...]
    last = jnp.where(blk == 0, 0.0, prev_ref[7:8, :])
    prev = jnp.where(_iota2((tb, 1), 0) == 0, last, pltpu.roll(x, 1, axis=0))
    t = x + (prev - x) * mu_ref[...]
    r = t[:, 0:rw]
    k = t[:, rw:2 * rw]
    v = t[:, 2 * rw:3 * rw]
    o = 3 * rw
    w_lat = t[:, o:o + R_DECAY_RANK]
    a_lat = t[:, o + R_DECAY_RANK:o + R_DECAY_RANK + R_A_RANK]
    g_lat = t[:, o + R_DECAY_RANK + R_A_RANK:]

    w_log = -_softplus(-(w0_ref[...] + _mm(jnp.tanh(w_lat), w2_ref[...]))) - 0.5
    lw = -jnp.exp(w_log)
    a = jax.nn.sigmoid(a0_ref[...] + _mm(a_lat, a2_ref[...]))
    g = _mm(jax.nn.sigmoid(g_lat), g2_ref[...])

    head_ones = _chunk_mask(rw, hd, False).astype(BF16)
    sh = hd.bit_length() - 1
    swap_ones = ((jnp.right_shift(_iota2((rw, rw), 0), sh) ^ 1)
                 == jnp.right_shift(_iota2((rw, rw), 1), sh)).astype(BF16)
    kk = k * kk_ref[...]
    kk = kk / jnp.maximum(jnp.sqrt(_mm(kk * kk, head_ones)), 1e-12)
    k2 = k * (1.0 + (a - 1.0) * ka_ref[...])
    avec = -kk
    bvec = kk * a
    cum = _cumsum_rows(_chunk_mask(tb, ch, True).astype(F32), lw)

    r_i = _iota2((ch, ch), 0)
    c_i = _iota2((ch, ch), 1)
    strict = r_i > c_i
    eye = r_i == c_i
    incl2 = jnp.concatenate([r_i >= c_i] * 2, axis=1)
    low_cols = c_i < hd
    low_rows = r_i < hd

    probs = [(c, h) for c in range(nch) for h in range(R_HEADS)]
    at_p, rt_p, ac_p, rc_p, vo_p, kb_t, kb_p, wc_p = ({} for _ in range(8))
    for c in range(nch):
        cs = slice(c * ch, (c + 1) * ch)
        cum_c = cum[cs]
        mid = cum_c[ch // 2 - 1:ch // 2, :]
        cl = cum_c[ch - 1:ch, :]
        e_mid = jnp.exp(-mid)
        a_true = avec[cs] * jnp.exp(cum_c - lw[cs])
        r_true = r[cs] * jnp.exp(cum_c)
        a_cen = a_true * e_mid
        r_cen = r_true * e_mid
        inv = jnp.exp(mid - cum_c)
        rel = jnp.exp(cl - cum_c)
        kt_c, bt_c = k2[cs] * inv, bvec[cs] * inv
        kp_c, bp_c = k2[cs] * rel, bvec[cs] * rel
        wc_c = jnp.exp(cl)
        v_c = v[cs]
        for h in range(R_HEADS):
            ps = slice((h // 2) * LANES, (h // 2 + 1) * LANES)
            mine = low_cols if h % 2 == 0 else jnp.logical_not(low_cols)
            key = (c, h)
            at_p[key] = jnp.where(mine, a_true[:, ps], 0.0)
            rt_p[key] = jnp.where(mine, r_true[:, ps], 0.0)
            ac_p[key] = _bf(jnp.where(mine, a_cen[:, ps], 0.0))
            rc_p[key] = _bf(jnp.where(mine, r_cen[:, ps], 0.0))
            vo_p[key] = _bf(jnp.where(mine, 0.0, v_c[:, ps]))
            kb_t[key] = _bf(jnp.concatenate([kt_c[:, ps], bt_c[:, ps]], axis=0))
            kb_p[key] = _bf(jnp.concatenate([kp_c[:, ps], bp_c[:, ps]], axis=0))
            wc_p[key] = wc_c[:, ps]

    yield
    gram = {p: _mm_nt(jnp.concatenate([ac_p[p], rc_p[p]], axis=0), kb_t[p]) for p in probs}
    m_ak = {p: _bf(jnp.where(strict, gram[p][:ch, :ch], 0.0)) for p in probs}
    pw = {p: _bf(jnp.where(strict, gram[p][:ch, ch:], 0.0)) for p in probs}
    n_cat = {p: _bf(jnp.where(incl2, gram[p][ch:], 0.0)) for p in probs}
    yield
    z = {p: at_p[p] + _mm(m_ak[p], vo_p[p]) for p in probs}
    yield
    for lvl in range(n_levels):
        if lvl + 1 < n_levels:
            res = {p: _mm(pw[p], jnp.concatenate([_bf(z[p]), pw[p]], axis=1)) for p in probs}
            z = {p: z[p] + res[p][:, :LANES] for p in probs}
            pw = {p: _bf(res[p][:, LANES:]) for p in probs}
        else:
            z = {p: z[p] + _mm(pw[p], z[p]) for p in probs}
        yield
    rhs = {p: jnp.concatenate([vo_p[p], _bf(z[p])], axis=0) for p in probs}
    ry = {p: _mm(n_cat[p], rhs[p]) for p in probs}
    yield
    tg = {p: _mm_tn(kb_p[p], rhs[p]) for p in probs}
    yield

    sts = [st_ref[h] for h in range(R_HEADS)]
    y_chunks = []
    for c in range(nch):
        nxt, g_full = [], []
        for h in range(R_HEADS):
            p = (c, h)
            mine = low_cols if h % 2 == 0 else jnp.logical_not(low_cols)
            my_rows = low_rows if h % 2 == 0 else jnp.logical_not(low_rows)
            r_eff = rt_p[p] + jnp.where(mine, ry[p], 0.0)
            t_full = (jnp.where(my_rows & mine, tg[p], 0.0)
                      + jnp.where(eye & mine, wc_p[p], 0.0))
            g_full.append(jnp.where(my_rows & jnp.logical_not(mine), tg[p], 0.0))
            nxt.append(_mm(jnp.concatenate([r_eff, t_full], axis=0), sts[h]))
        ys = [jnp.where(low_cols if h % 2 else jnp.logical_not(low_cols), nxt[h][:ch] + ry[(c, h)], 0.0)
              for h in range(R_HEADS)]
        y_chunks.append(jnp.concatenate([ys[2 * j] + ys[2 * j + 1] for j in range(R_HEADS // 2)], axis=1))
        sts = [nxt[h][ch:] + g_full[h] for h in range(R_HEADS)]
        yield
    for h in range(R_HEADS):
        st_ref[h] = sts[h]
    y = jnp.concatenate(y_chunks, axis=0)

    inv_d = 1.0 / hd
    mu = _mm(y, head_ones) * inv_d
    yc = y - mu
    var = _mm(yc * yc, head_ones) * inv_d
    yn = yc * lax.rsqrt(var + R_GN_EPS) * lnx_ref[0:1, :] + lnx_ref[1:2, :]
    bonus = _mm(r * k2 * rk_ref[...], swap_ones) * v
    out_ref[...] = (yn + bonus) * g


def _t5_bucket(dist):
    max_exact = N_BUCKETS // 2
    d = jnp.maximum(dist, 1).astype(F32)
    large = max_exact + (jnp.log(d / max_exact) / math.log(MAX_DISTANCE / max_exact)
                         * (N_BUCKETS - max_exact)).astype(jnp.int32)
    large = jnp.minimum(large, N_BUCKETS - 1)
    return jnp.where(dist < max_exact, dist, large)


def _attn_tables(rel_bias):
    blk = A_BLOCK
    i = jnp.arange(blk)[:, None]
    j = jnp.arange(blk)[None, :]
    top = 3 * blk - 1
    period = 5 * blk
    by_dist_rev = rel_bias.astype(F32)[:, _t5_bucket(jnp.maximum(top - jnp.arange(period), 0))]
    shifted = jnp.tile(by_dist_rev, (1, blk))[:, :blk * (period - 1)].reshape(A_HEADS, blk, period - 1)

    def bias(offset):
        return shifted[:, :, top - offset:top - offset + blk]

    neg = jnp.full((A_HEADS, blk, blk), NEG, F32)
    is_meta = j >= FRONT_PAD
    cur_d = i - j
    prev_d = blk + i - j
    cur = jnp.where(cur_d >= 0, bias(0), NEG)
    prev = jnp.where(prev_d < WINDOW, bias(blk), NEG)
    meta0 = jnp.where(is_meta & (cur_d >= 0), bias(0), NEG)
    meta1 = jnp.where(is_meta, bias(blk), NEG)
    meta2 = jnp.where(is_meta, bias(2 * blk), NEG)
    case0 = jnp.concatenate([meta0, neg, neg], axis=-1)
    case1 = jnp.concatenate([meta1, neg, cur], axis=-1)
    case2 = jnp.concatenate([meta2, prev, cur], axis=-1)
    tab = jnp.stack([case0, case1, case2], axis=0)
    tab = jnp.concatenate([tab[..., blk:], tab[..., FRONT_PAD:blk]], axis=-1)
    nk = tab.shape[-1]
    tab = tab.reshape(3, A_KV_HEADS, A_GROUP, blk, nk)
    return tab.transpose(0, 1, 4, 2, 3).reshape(3, A_KV_HEADS, nk, A_GROUP * blk)


assert A_KV_HEADS * HEAD_DIM == LANES
_PAIRED_HEADS = tuple(kv * A_GROUP + g for g in range(A_GROUP) for kv in range(A_KV_HEADS))


def _pair_heads(w, axis=0):
    split = w.shape[:axis] + (A_KV_HEADS, A_GROUP, HEAD_DIM) + w.shape[axis + 1:]
    return jnp.swapaxes(w.reshape(split), axis, axis + 1).reshape(w.shape)


def _swa_stages(q_ref, kc_ref, kp_ref, km_ref, vc_ref, vp_ref, vm_ref, tab_ref, sink_ref, out_ref):
    scale = HEAD_DIM ** -0.5
    blk = A_BLOCK
    hd = HEAD_DIM
    nb = q_ref.shape[0] // blk
    step = pl.program_id(1)
    kvs = range(A_KV_HEADS)
    upper = _iota2((blk, LANES), 1) >= hd
    yield 4

    v_t = jnp.concatenate([vm_ref[...], vp_ref[...], vc_ref[...]], axis=0).T
    v_meta = v_t[:, FRONT_PAD:blk]
    k_meta = km_ref[FRONT_PAD:, :]
    ones = jnp.ones((hd, 2 * blk + N_META), F32)

    probs = [(i, kv) for i in range(nb) for kv in kvs]
    s_t = {}
    for i in range(nb):
        rows = slice(i * blk, (i + 1) * blk)
        prev = kp_ref[...] if i == 0 else kc_ref[(i - 1) * blk:i * blk, :]
        keys = jnp.concatenate([prev, kc_ref[rows, :], k_meta], axis=0)
        for kv in kvs:
            mine = upper if kv == 1 else jnp.logical_not(upper)
            q = jnp.concatenate([jnp.where(mine, q_ref[rows, g * LANES:(g + 1) * LANES], 0.0)
                                 for g in range(A_GROUP)], axis=0) * scale
            s_t[i, kv] = _mm_nt(keys, q)
    yield
    p_t, esink = {}, {}
    for i, kv in probs:
        case = 2 if i >= 2 else jnp.minimum(step * nb + i, 2)
        sk = s_t[i, kv] + tab_ref[case, kv]
        sink = sink_ref[kv][0:1, :]
        m = jnp.maximum(jnp.max(sk, axis=0, keepdims=True), sink)
        p_t[i, kv] = _bf(jnp.exp(sk - m))
        esink[i, kv] = jnp.exp(sink - m)
    yield
    o_t = {}
    for i, kv in probs:
        vr = slice(kv * hd, (kv + 1) * hd)
        prev = v_t[vr, blk:2 * blk] if i == 0 else v_t[vr, (i + 1) * blk:(i + 2) * blk]
        vals = jnp.concatenate([prev, v_t[vr, (i + 2) * blk:(i + 3) * blk], v_meta[vr]], axis=1)
        r = _mm(jnp.concatenate([vals, ones], axis=0), p_t[i, kv])
        o_t[i, kv] = r[:hd] / (r[hd:] + esink[i, kv])
    yield
    for i in range(nb):
        tiles = [jnp.concatenate([o_t[i, kv][:, g * blk:(g + 1) * blk] for kv in kvs], axis=0).T
                 for g in range(A_GROUP)]
        out_ref[i * blk:(i + 1) * blk, :] = jnp.concatenate(tiles, axis=1)


def _interleave(gens):
    total = [next(g) for g in gens]
    done = [0] * len(gens)
    live = set(range(len(gens)))
    while live:
        i = min(live, key=lambda n: ((done[n] + 1) / total[n], n))
        try:
            next(gens[i])
            done[i] += 1
        except StopIteration:
            live.discard(i)


def _mixers_kernel(*refs, n_m, n_r, n_s):
    m_in, r_in, s_in = refs[:n_m], refs[n_m:n_m + n_r], refs[n_m + n_r:n_m + n_r + n_s]
    ym_ref, yr_ref, ya_ref, cst_ref, m_ref, st_ref = refs[n_m + n_r + n_s:]
    _interleave([_rwkv_stages(*r_in, yr_ref, st_ref),
                 _mlstm_stages(*m_in, ym_ref, cst_ref, m_ref),
                 _swa_stages(*s_in, ya_ref)])


def _mixers(pa, pm, pr, tables, l, sink, bias, norm_w, mu, w0, w2, a0, a2, g2, k_k, k_a, r_k, ln_x):
    b, lp, _ = pa.shape
    blk = A_BLOCK
    tb = _mix_rows(lp)
    nb = tb // blk
    tile = lambda width: pl.BlockSpec((None, tb, width), lambda i, j: (i, j, 0))

    m_in = [pm, bias, norm_w]
    m_specs = [tile(PM_W), _layer_spec(bias, l), _layer_spec(norm_w, l)]

    r_params = [mu, w0, w2, a0, a2, g2, k_k, k_a, r_k, ln_x]
    r_in = [pr, pr] + r_params
    r_specs = [tile(PR_W),
               pl.BlockSpec((None, 8, PR_W), lambda i, j: (i, jnp.maximum(j * (tb // 8) - 1, 0), 0))]
    r_specs += [_layer_spec(a, l) for a in r_params]

    kcol = A_WIDTH // A_KV_WIDTH
    cur_spec = lambda col: pl.BlockSpec((None, tb, A_KV_WIDTH), lambda i, j: (i, j, col))
    prev_spec = lambda col: pl.BlockSpec((None, blk, A_KV_WIDTH),
                                         lambda i, j: (i, jnp.maximum(j * nb - 1, 0), col))
    meta_spec = lambda col: pl.BlockSpec((None, blk, A_KV_WIDTH), lambda i, j: (i, 0, col))
    s_in = [pa] * 7 + [tables, sink]
    s_specs = [tile(A_WIDTH), cur_spec(kcol), prev_spec(kcol), meta_spec(kcol),
               cur_spec(kcol + 1), prev_spec(kcol + 1), meta_spec(kcol + 1),
               pl.BlockSpec(tables.shape, lambda i, j: (0, 0, 0, 0), pipeline_mode=pl.Buffered(1)),
               _layer_spec(sink, l)]

    return pl.pallas_call(
        functools.partial(_mixers_kernel, n_m=len(m_in), n_r=len(r_in), n_s=len(s_in)),
        grid=(b, lp // tb),
        in_specs=m_specs + r_specs + s_specs,
        out_specs=[tile(M_WIDTH), tile(R_WIDTH), tile(A_WIDTH)],
        out_shape=[jax.ShapeDtypeStruct((b, lp, M_WIDTH), F32),
                   jax.ShapeDtypeStruct((b, lp, R_WIDTH), F32),
                   jax.ShapeDtypeStruct((b, lp, A_WIDTH), F32)],
        scratch_shapes=[pltpu.VMEM((M_HEADS // 2, LANES, 2 * LANES), F32),
                        pltpu.VMEM((M_HEADS // 2, 8, LANES), F32),
                        pltpu.VMEM((R_HEADS, LANES, LANES), F32)],
        compiler_params=_cparams(("arbitrary", "arbitrary")),
        name="mixers",
    )(*m_in, *r_in, *s_in)


def _layer_norm_rows(z, ln_ref):
    mu = jnp.mean(z, axis=-1, keepdims=True)
    zc = z - mu
    var = jnp.mean(zc * zc, axis=-1, keepdims=True)
    return zc * lax.rsqrt(var + LN_EPS) * ln_ref[0:1, :] + ln_ref[1:2, :]


FF_CHUNK = 256


def _post_kernel(*refs, n_each, from_x, final, alpha, d_ff):
    groups, pos = [], 0
    for n in n_each:
        groups.append(refs[pos:pos + n])
        pos += n
    w_ref, lnm_ref, wi_ref, wo_ref, lnf_ref, out_ref = refs[pos:]
    ym, yr, ya = (_read_tile(g, False) for g in groups[:3])
    h = _read_tile(groups[3], from_x)
    tm = h.shape[0]
    y_cat = jnp.concatenate([ym.astype(BF16), yr.astype(BF16), ya.astype(BF16)], axis=1)
    mix = jnp.dot(y_cat, w_ref[...], preferred_element_type=F32)
    x = _layer_norm_rows(alpha * h + mix, lnm_ref)
    if not final:
        is_pad = (pl.program_id(1) * tm + _iota2((tm, 1), 0)) < FRONT_PAD
        x = jnp.where(is_pad, 0.0, x)
    xb = x.astype(BF16)
    acc = alpha * x
    for j in range(d_ff // FF_CHUNK):
        cs = slice(j * FF_CHUNK, (j + 1) * FF_CHUNK)
        us = slice(d_ff + j * FF_CHUNK, d_ff + (j + 1) * FF_CHUNK)
        gate = jnp.dot(xb, wi_ref[:, cs], preferred_element_type=F32)
        up = jnp.dot(xb, wi_ref[:, us], preferred_element_type=F32)
        act = (gate * jax.nn.sigmoid(gate) * up).astype(BF16)
        acc += jnp.dot(act, wo_ref[cs, :], preferred_element_type=F32)
    y = _layer_norm_rows(acc, lnf_ref)
    out_ref[...] = y if final else jnp.where(is_pad, 0.0, y)


def _post(ym, yr, ya, h, x, prefix, w, ln_mix, wi, wo, ln_ffn, l, alpha, final):
    b, lp, _ = ym.shape
    seq, d = x.shape[1:]
    d_ff = wo.shape[1]
    assert d_ff % FF_CHUNK == 0
    from_x = h is None
    if final:
        tm = _row_tile(seq)
        groups = [_token_tile_inputs(a, tm) for a in (ym, yr, ya)]
        groups.append(([x], [pl.BlockSpec((None, tm, d), lambda i, j: (i, j, 0))]) if from_x
                      else _token_tile_inputs(h, tm))
        rows_out = seq
    else:
        tm = _row_tile(lp)
        groups = [_padded_tile_inputs(a, None, None, tm) for a in (ym, yr, ya)]
        groups.append(_padded_tile_inputs(h, x, prefix, tm))
        rows_out = lp
    arrays = [a for g in groups for a in g[0]]
    specs = [s for g in groups for s in g[1]]
    const = lambda a: _layer_spec(a, l, pipeline_mode=pl.Buffered(1))
    return pl.pallas_call(
        functools.partial(_post_kernel, n_each=tuple(len(g[0]) for g in groups),
                          from_x=from_x and not final, final=final, alpha=alpha, d_ff=d_ff),
        grid=(b, rows_out // tm),
        in_specs=specs + [const(w), _layer_spec(ln_mix, l), const(wi), const(wo), _layer_spec(ln_ffn, l)],
        out_specs=pl.BlockSpec((None, tm, d), lambda i, j: (i, j, 0)),
        out_shape=jax.ShapeDtypeStruct((b, rows_out, d), F32),
        compiler_params=_cparams(("arbitrary", "arbitrary")),
        name="out_proj_ffn",
    )(*arrays, w, ln_mix, wi, wo, ln_ffn)


def kernel(x, meta_tokens, rel_bias, w_in, m_gate_bias, m_norm_w, r_mu_rkv, r_mu_w, r_mu_a, r_mu_g, r_w0, r_w2, r_a0, r_a2, r_g2, r_k_k, r_k_a, r_r_k, r_ln_x, a_sinks, w_out, ln_mix, w_ff_in, w_ff_out, ln_ffn):
    b, seq, d = x.shape
    depth = w_in.shape[0]
    alpha = (2 * depth) ** 0.25
    assert (TOKEN_START + seq) % A_BLOCK == 0

    lp = TOKEN_START + seq
    prefix = jnp.concatenate([jnp.zeros((FRONT_PAD, d), x.dtype), meta_tokens.astype(x.dtype)], axis=0)
    tables = _attn_tables(rel_bias)

    m_end = PA_W + M_RAW
    rv0 = m_end + 2 * R_WIDTH
    w_pad = jnp.concatenate(
        [_pair_heads(w_in[:, :, :A_WIDTH], 2), w_in[:, :, A_WIDTH:m_end],
         jnp.zeros((depth, d, PM_W - M_RAW), w_in.dtype), w_in[:, :, m_end:rv0],
         _swap_pairs(w_in[:, :, rv0:rv0 + R_WIDTH], 2), w_in[:, :, rv0 + R_WIDTH:]], axis=2).astype(BF16)
    w_o = jnp.concatenate([w_out[:, :M_WIDTH], _swap_pairs(w_out[:, M_WIDTH:M_WIDTH + R_WIDTH], 1),
                           _pair_heads(w_out[:, M_WIDTH + R_WIDTH:], 1)], axis=1).astype(BF16)
    w_fi, w_fo = w_ff_in.astype(BF16), w_ff_out.astype(BF16)
    rows = lambda a: a.reshape(depth, 1, -1)
    mu = jnp.concatenate([r_mu_rkv[:, 0], r_mu_rkv[:, 1], _swap_pairs(r_mu_rkv[:, 2], 1),
                          r_mu_w, r_mu_a, r_mu_g], axis=1)
    bias = jnp.pad(rows(m_gate_bias.astype(F32)), ((0, 0), (0, 0), (0, LANES - 2 * M_HEADS)))
    sink = jnp.broadcast_to(
        jnp.repeat(a_sinks.astype(F32).reshape(depth, A_KV_HEADS, 1, A_GROUP), A_BLOCK, axis=3),
        (depth, A_KV_HEADS, 8, A_GROUP * A_BLOCK))
    mixer_params = (sink, bias, rows(m_norm_w), rows(mu), rows(r_w0), r_w2, rows(r_a0), r_a2,
                    _swap_pairs(r_g2, 2), rows(r_k_k), rows(r_k_a), rows(r_r_k), _swap_pairs(r_ln_x, 2))

    h = None
    for l in range(depth):
        pa, pm, pr = _proj(h, x, prefix, w_pad, l, lp)
        y_m, y_r, y_a = _mixers(pa, pm, pr, tables, l, *mixer_params)
        h = _post(y_m, y_r, y_a, h, x, prefix, w_o, ln_mix, w_fi, w_fo, ln_ffn, l, alpha,
                  final=l == depth - 1)
    return h
```

```python
import functools
import math

import jax
import jax.numpy as jnp
from jax import lax
from jax.experimental import pallas as pl
from jax.experimental.pallas import tpu as pltpu

F32 = jnp.float32
BF16 = jnp.bfloat16

HEAD_DIM = 64
N_META = 16
M_HEADS = 4
M_WIDTH = M_HEADS * HEAD_DIM
M_CHUNK = 64
M_NORM_EPS = 1e-6
R_HEADS = 4
R_WIDTH = R_HEADS * HEAD_DIM
R_CHUNK = 128
R_DECAY_RANK = 32
R_A_RANK = 32
R_GATE_RANK = 64
R_GN_EPS = 64e-5
A_HEADS = 8
A_KV_HEADS = 2
A_GROUP = A_HEADS // A_KV_HEADS
A_WIDTH = A_HEADS * HEAD_DIM
A_KV_WIDTH = A_KV_HEADS * HEAD_DIM
WINDOW = 128
A_BLOCK = 128
N_BUCKETS = 32
MAX_DISTANCE = 128
LN_EPS = 1e-5
NEG = -1e30

LANES = 128
FRONT_PAD = A_BLOCK - N_META
TOKEN_START = FRONT_PAD + N_META

PA_W = A_WIDTH + 2 * A_KV_WIDTH
PM_W = 4 * M_WIDTH + LANES
PR_W = 3 * R_WIDTH + R_DECAY_RANK + R_A_RANK + R_GATE_RANK
M_RAW = 4 * M_WIDTH + 2 * M_HEADS

VMEM_LIMIT = 56 * 1024 * 1024
MIX_ROWS = 640


def _row_tile(n):
    for t in (640, 512, 256, 128, 64):
        if n % t == 0:
            return t
    raise ValueError(f"row count {n} has no supported tile")


def _mix_rows(lp):
    return MIX_ROWS if lp % MIX_ROWS == 0 else A_BLOCK


def _cparams(sem):
    return pltpu.CompilerParams(dimension_semantics=sem, vmem_limit_bytes=VMEM_LIMIT)


def _bf(x):
    return x.astype(BF16)


def _mm(a, b):
    return lax.dot_general(_bf(a), _bf(b), (((1,), (0,)), ((), ())), preferred_element_type=F32)


def _mm_nt(a, b):
    return lax.dot_general(_bf(a), _bf(b), (((1,), (1,)), ((), ())), preferred_element_type=F32)


def _mm_tn(a, b):
    return lax.dot_general(_bf(a), _bf(b), (((0,), (0,)), ((), ())), preferred_element_type=F32)


def _iota2(shape, dim):
    return lax.broadcasted_iota(jnp.int32, shape, dim)


def _split3(x):
    hi = _bf(x).astype(F32)
    r1 = x - hi
    mid = _bf(r1).astype(F32)
    return hi, mid, r1 - mid


def _chunk_mask(n, chunk, lower):
    r_i = _iota2((n, n), 0)
    c_i = _iota2((n, n), 1)
    sh = chunk.bit_length() - 1
    same = jnp.right_shift(r_i, sh) == jnp.right_shift(c_i, sh)
    return same & (r_i >= c_i) if lower else same


def _cumsum_rows(tri, x):
    w = x.shape[1]
    res = _mm(tri, jnp.concatenate(_split3(x), axis=1))
    return res[:, :w] + res[:, w:2 * w] + res[:, 2 * w:]


def _cumsum_lanes(tri, x):
    r = x.shape[0]
    res = _mm_nt(jnp.concatenate(_split3(x), axis=0), tri)
    return res[:r] + res[r:2 * r] + res[2 * r:]


def _softplus(x):
    return jnp.maximum(x, 0.0) + jnp.log1p(jnp.exp(-jnp.abs(x)))


def _padded_tile_inputs(h, x, prefix, tm):
    if h is not None:
        return [h], [pl.BlockSpec((None, tm, h.shape[-1]), lambda i, j: (i, j, 0))]
    d = x.shape[-1]
    nb = tm // A_BLOCK
    specs = [pl.BlockSpec((A_BLOCK, d), lambda i, j: (0, 0))]
    specs += [pl.BlockSpec((None, A_BLOCK, d), lambda i, j, r=r: (i, jnp.maximum(j * nb - 1 + r, 0), 0))
              for r in range(nb)]
    return [prefix] + [x] * nb, specs


def _token_tile_inputs(a, tm):
    nb = tm // A_BLOCK
    first = TOKEN_START // A_BLOCK
    specs = [pl.BlockSpec((None, A_BLOCK, a.shape[-1]), lambda i, j, r=r: (i, first + j * nb + r, 0))
             for r in range(nb)]
    return [a] * nb, specs


def _read_tile(refs, from_x):
    if from_x:
        first = jnp.where(pl.program_id(1) == 0, refs[0][...], refs[1][...])
        return jnp.concatenate([first] + [r[...] for r in refs[2:]], axis=0)
    if len(refs) == 1:
        return refs[0][...]
    return jnp.concatenate([r[...] for r in refs], axis=0)


def _proj_kernel(*refs, n_in, from_x):
    w_ref, oa_ref, om_ref, or_ref = refs[n_in:]
    x = _read_tile(refs[:n_in], from_x)
    acc = jnp.dot(x.astype(BF16), w_ref[...], preferred_element_type=F32)
    oa_ref[...] = acc[:, :PA_W]
    om_ref[...] = acc[:, PA_W:PA_W + PM_W]
    or_ref[...] = acc[:, PA_W + PM_W:]


def _layer_spec(a, l, **kw):
    return pl.BlockSpec((None,) + a.shape[1:], lambda i, j: (l,) + (0,) * (a.ndim - 1), **kw)


def _proj(h, x, prefix, w, l, lp):
    b = x.shape[0]
    d, n = w.shape[1:]
    tm = _row_tile(lp)
    arrays, specs = _padded_tile_inputs(h, x, prefix, tm)
    spec = lambda width: pl.BlockSpec((None, tm, width), lambda i, j: (i, j, 0))
    return pl.pallas_call(
        functools.partial(_proj_kernel, n_in=len(arrays), from_x=h is None),
        grid=(b, lp // tm),
        in_specs=specs + [_layer_spec(w, l, pipeline_mode=pl.Buffered(1))],
        out_specs=[spec(PA_W), spec(PM_W), spec(PR_W)],
        out_shape=[jax.ShapeDtypeStruct((b, lp, PA_W), F32),
                   jax.ShapeDtypeStruct((b, lp, PM_W), F32),
                   jax.ShapeDtypeStruct((b, lp, PR_W), F32)],
        compiler_params=_cparams(("arbitrary", "arbitrary")),
        name="in_proj",
    )(*arrays, w)


def _mlstm_stages(pm_ref, bias_ref, nw_ref, out_ref, cst_ref, m_ref):
    blk = pl.program_id(1)
    tb = pm_ref.shape[0]
    ch = M_CHUNK
    hd = HEAD_DIM
    nch = tb // ch
    npair = M_HEADS // 2
    assert 2 * hd == LANES and 2 * ch == LANES
    yield 8

    @pl.when(blk == 0)
    def _():
        cst_ref[...] = jnp.zeros_like(cst_ref)
        m_ref[...] = jnp.full_like(m_ref, NEG)

    row = blk * tb + _iota2((tb, 1), 0)
    is_pad = row < FRONT_PAD
    gb = pm_ref[:, 4 * M_WIDTH:] + bias_ref[...]
    lane = _iota2((tb, LANES), 1)
    li = jnp.where(is_pad, NEG, gb)
    lf = jnp.where(is_pad, 0.0, -_softplus(-gb))
    gcol = jnp.where(lane < M_HEADS, li, jnp.where(lane < 2 * M_HEADS, lf, 0.0))
    sel = (_iota2((8, LANES), 0) == _iota2((8, LANES), 1)).astype(F32)
    grow = _mm_nt(jnp.concatenate([sel] * 3, axis=1), jnp.concatenate(_split3(gcol), axis=1))
    brow = _cumsum_lanes(_chunk_mask(tb, ch, True).astype(F32), grow)
    r8 = _iota2((8, tb), 0)
    t8 = jnp.bitwise_and(_iota2((8, tb), 1), ch - 1)
    cm = grow - pltpu.roll(brow, M_HEADS, axis=0)
    sh = 1
    while sh < ch:
        cm = jnp.where(t8 >= sh, jnp.maximum(cm, pltpu.roll(cm, sh, axis=1)), cm)
        sh *= 2
    rows_lb = jnp.where(r8 < M_HEADS, grow, brow)
    x16 = jnp.concatenate([rows_lb, cm], axis=0)
    nrep = 3 * M_WIDTH
    e_sel = (_iota2((16, nrep), 0) == jnp.right_shift(_iota2((16, nrep), 1), hd.bit_length() - 1)
             ).astype(F32)
    rep = _mm_tn(jnp.concatenate(_split3(x16), axis=0), jnp.concatenate([e_sel] * 3, axis=0))
    rep_of = lambda qn, j, cs: rep[cs, qn * M_WIDTH + j * LANES:qn * M_WIDTH + (j + 1) * LANES]
    rows_up = pltpu.roll(rows_lb, ch, axis=1)
    rows_dn = pltpu.roll(rows_lb, tb - ch, axis=1)
    low1 = _iota2((1, LANES), 1) < hd

    def row_pair(base, c, j):
        ts = slice((c // 2) * LANES, (c // 2 + 1) * LANES)
        h0, h1 = base + 2 * j, base + 2 * j + 1
        if c % 2 == 0:
            return jnp.where(low1, rows_lb[h0:h0 + 1, ts], rows_up[h1:h1 + 1, ts])
        return jnp.where(low1, rows_dn[h0:h0 + 1, ts], rows_lb[h1:h1 + 1, ts])

    low = _iota2((ch, LANES), 1) < hd
    t_i = _iota2((ch, LANES), 0)
    causal2 = t_i >= jnp.bitwise_and(_iota2((ch, LANES), 1), ch - 1)
    r128 = _iota2((LANES, LANES), 0)
    c128 = _iota2((LANES, LANES), 1)
    ones_bd = ((r128 < hd) == (c128 < hd)).astype(F32)
    bd2 = jnp.concatenate([ones_bd, ones_bd], axis=1) > 0.5
    ones_t = jnp.ones((ch, LANES), F32)

    probs = [(c, j) for c in range(nch) for j in range(npair)]
    q_p, k_p, v_p = {}, {}, {}
    for c, j in probs:
        cs = slice(c * ch, (c + 1) * ch)
        ps = slice(j * LANES, (j + 1) * LANES)
        q_p[c, j] = pm_ref[cs, ps]
        k_p[c, j] = pm_ref[cs, M_WIDTH + ps.start:M_WIDTH + ps.stop] * (hd ** -0.5)
        v_p[c, j] = pm_ref[cs, 2 * M_WIDTH + ps.start:2 * M_WIDTH + ps.stop]
    bdiag = lambda a: jnp.concatenate([jnp.where(low, a, 0.0), jnp.where(low, 0.0, a)], axis=0)
    yield
    qk = {p: _mm_nt(q_p[p], bdiag(k_p[p])) for p in probs}
    yield

    dw_p, sint_p, emt_p, wa_p, decay_p = {}, {}, {}, {}, {}
    for j in range(npair):
        m_prev = m_ref[j][0:1, :]
        for c in range(nch):
            cs = slice(c * ch, (c + 1) * ch)
            li_c, b_c, cm_c = rep_of(0, j, cs), rep_of(1, j, cs), rep_of(2, j, cs)
            d_mat = jnp.where(causal2, b_c - row_pair(M_HEADS, c, j) + row_pair(0, c, j), NEG)
            inter = b_c + m_prev
            m_t = jnp.maximum(b_c + cm_c, inter)
            dw_p[c, j] = jnp.where(causal2, jnp.exp(d_mat - m_t), 0.0)
            sint_p[c, j] = jnp.exp(inter - m_t)
            emt_p[c, j] = jnp.exp(-m_t)
            g = b_c[ch - 1:ch, :]
            m_new = jnp.maximum(g + m_prev, g + cm_c[ch - 1:ch, :])
            wa_p[c, j] = jnp.exp(g - b_c + li_c - m_new)
            decay_p[c, j] = jnp.exp(g + m_prev - m_new)
            m_prev = m_new
        m_ref[j] = jnp.broadcast_to(m_prev, m_ref.shape[1:])

    yield
    upd = {p: jnp.where(bd2, _mm_tn(wa_p[p] * k_p[p], jnp.concatenate([v_p[p], ones_t], axis=1)), 0.0)
           for p in probs}
    yield
    st_p = {}
    for j in range(npair):
        st = cst_ref[j]
        for c in range(nch):
            st_p[c, j] = st
            dec = decay_p[c, j]
            st = jnp.concatenate([dec, dec], axis=1) * st + upd[c, j]
        cst_ref[j] = st
    yield
    inter_p = {p: _mm(q_p[p], st_p[p]) for p in probs}
    yield
    intra_p = {p: _mm(qk[p] * dw_p[p], jnp.concatenate([bdiag(v_p[p]), ones_bd], axis=1)) for p in probs}
    yield

    rows = []
    for c in range(nch):
        pairs = []
        for j in range(npair):
            p = (c, j)
            tot = intra_p[p] + jnp.concatenate([sint_p[p], sint_p[p]], axis=1) * inter_p[p]
            pairs.append(tot[:, :LANES] / jnp.maximum(jnp.abs(tot[:, LANES:]), emt_p[p]))
        rows.append(jnp.concatenate(pairs, axis=1))
    hh = jnp.concatenate(rows, axis=0)

    head_ones = _chunk_mask(M_WIDTH, hd, False).astype(BF16)
    mu = _mm(hh, head_ones) * (1.0 / hd)
    hc = hh - mu
    var = _mm(hc * hc, head_ones) * (1.0 / hd)
    o_pre = pm_ref[:, 3 * M_WIDTH:4 * M_WIDTH]
    out_ref[...] = hc * lax.rsqrt(var + M_NORM_EPS) * nw_ref[...] * jax.nn.sigmoid(o_pre)


def _swap_pairs(w, axis=0):
    split = w.shape[:axis] + (R_HEADS // 2, 2, HEAD_DIM) + w.shape[axis + 1:]
    return jnp.flip(w.reshape(split), axis=axis + 1).reshape(w.shape)


def _rwkv_stages(pr_ref, prev_ref, mu_ref, w0_ref, w2_ref, a0_ref, a2_ref, g2_ref,
                 kk_ref, ka_ref, rk_ref, lnx_ref, out_ref, st_ref):
    blk = pl.program_id(1)
    tb = pr_ref.shape[0]
    ch = R_CHUNK
    rw = R_WIDTH
    hd = HEAD_DIM
    nch = tb // ch
    assert ch == LANES and 2 * hd == LANES
    n_levels = ch.bit_length() - 1
    yield 6 + n_levels + nch

    @pl.when(blk == 0)
    def _():
        st_ref[...] = jnp.zeros_like(st_ref)

    x = pr_ref[...]
    last = jnp.where(blk == 0, 0.0, prev_ref[7:8, :])
    prev = jnp.where(_iota2((tb, 1), 0) == 0, last, pltpu.roll(x, 1, axis=0))
    t = x + (prev - x) * mu_ref[...]
    r = t[:, 0:rw]
    k = t[:, rw:2 * rw]
    v = t[:, 2 * rw:3 * rw]
    o = 3 * rw
    w_lat = t[:, o:o + R_DECAY_RANK]
    a_lat = t[:, o + R_DECAY_RANK:o + R_DECAY_RANK + R_A_RANK]
    g_lat = t[:, o + R_DECAY_RANK + R_A_RANK:]

    w_log = -_softplus(-(w0_ref[...] + _mm(jnp.tanh(w_lat), w2_ref[...]))) - 0.5
    lw = -jnp.exp(w_log)
    a = jax.nn.sigmoid(a0_ref[...] + _mm(a_lat, a2_ref[...]))
    g = _mm(jax.nn.sigmoid(g_lat), g2_ref[...])

    head_ones = _chunk_mask(rw, hd, False).astype(BF16)
    sh = hd.bit_length() - 1
    swap_ones = ((jnp.right_shift(_iota2((rw, rw), 0), sh) ^ 1)
                 == jnp.right_shift(_iota2((rw, rw), 1), sh)).astype(BF16)
    kk = k * kk_ref[...]
    kk = kk / jnp.maximum(jnp.sqrt(_mm(kk * kk, head_ones)), 1e-12)
    k2 = k * (1.0 + (a - 1.0) * ka_ref[...])
    avec = -kk
    bvec = kk * a
    cum = _cumsum_rows(_chunk_mask(tb, ch, True).astype(F32), lw)

    r_i = _iota2((ch, ch), 0)
    c_i = _iota2((ch, ch), 1)
    strict = r_i > c_i
    eye = r_i == c_i
    incl2 = jnp.concatenate([r_i >= c_i] * 2, axis=1)
    low_cols = c_i < hd
    low_rows = r_i < hd

    probs = [(c, h) for c in range(nch) for h in range(R_HEADS)]
    at_p, rt_p, ac_p, rc_p, vo_p, kb_t, kb_p, wc_p = ({} for _ in range(8))
    for c in range(nch):
        cs = slice(c * ch, (c + 1) * ch)
        cum_c = cum[cs]
        mid = cum_c[ch // 2 - 1:ch // 2, :]
        cl = cum_c[ch - 1:ch, :]
        e_mid = jnp.exp(-mid)
        a_true = avec[cs] * jnp.exp(cum_c - lw[cs])
        r_true = r[cs] * jnp.exp(cum_c)
        a_cen = a_true * e_mid
        r_cen = r_true * e_mid
        inv = jnp.exp(mid - cum_c)
        rel = jnp.exp(cl - cum_c)
        kt_c, bt_c = k2[cs] * inv, bvec[cs] * inv
        kp_c, bp_c = k2[cs] * rel, bvec[cs] * rel
        wc_c = jnp.exp(cl)
        v_c = v[cs]
        for h in range(R_HEADS):
            ps = slice((h // 2) * LANES, (h // 2 + 1) * LANES)
            mine = low_cols if h % 2 == 0 else jnp.logical_not(low_cols)
            key = (c, h)
            at_p[key] = jnp.where(mine, a_true[:, ps], 0.0)
            rt_p[key] = jnp.where(mine, r_true[:, ps], 0.0)
            ac_p[key] = _bf(jnp.where(mine, a_cen[:, ps], 0.0))
            rc_p[key] = _bf(jnp.where(mine, r_cen[:, ps], 0.0))
            vo_p[key] = _bf(jnp.where(mine, 0.0, v_c[:, ps]))
            kb_t[key] = _bf(jnp.concatenate([kt_c[:, ps], bt_c[:, ps]], axis=0))
            kb_p[key] = _bf(jnp.concatenate([kp_c[:, ps], bp_c[:, ps]], axis=0))
            wc_p[key] = wc_c[:, ps]

    yield
    gram = {p: _mm_nt(jnp.concatenate([ac_p[p], rc_p[p]], axis=0), kb_t[p]) for p in probs}
    m_ak = {p: _bf(jnp.where(strict, gram[p][:ch, :ch], 0.0)) for p in probs}
    pw = {p: _bf(jnp.where(strict, gram[p][:ch, ch:], 0.0)) for p in probs}
    n_cat = {p: _bf(jnp.where(incl2, gram[p][ch:], 0.0)) for p in probs}
    yield
    z = {p: at_p[p] + _mm(m_ak[p], vo_p[p]) for p in probs}
    yield
    for lvl in range(n_levels):
        band = (1 << lvl) if (1 << lvl) % 16 == 0 else 0
        keep = ch - band
        last = lvl + 1 == n_levels
        for p in probs:
            rhs_l = _bf(z[p]) if last else jnp.concatenate([_bf(z[p]), pw[p]], axis=1)
            res = _mm(pw[p][band:, :keep], rhs_l[:keep])
            dz = res[:, :LANES]
            z[p] = z[p] + dz if band == 0 else jnp.concatenate([z[p][:band], z[p][band:] + dz], axis=0)
            if not last:
                sq = _bf(res[:, LANES:])
                pw[p] = sq if band == 0 else jnp.concatenate([jnp.zeros((band, ch), BF16), sq], axis=0)
        yield
    rhs = {p: jnp.concatenate([vo_p[p], _bf(z[p])], axis=0) for p in probs}
    ry = {p: _mm(n_cat[p], rhs[p]) for p in probs}
    yield
    tg = {p: _mm_tn(kb_p[p], rhs[p]) for p in probs}
    yield

    sts = [st_ref[h] for h in range(R_HEADS)]
    y_chunks = []
    for c in range(nch):
        nxt, g_full = [], []
        for h in range(R_HEADS):
            p = (c, h)
            mine = low_cols if h % 2 == 0 else jnp.logical_not(low_cols)
            my_rows = low_rows if h % 2 == 0 else jnp.logical_not(low_rows)
            r_eff = rt_p[p] + jnp.where(mine, ry[p], 0.0)
            t_full = (jnp.where(my_rows & mine, tg[p], 0.0)
                      + jnp.where(eye & mine, wc_p[p], 0.0))
            g_full.append(jnp.where(my_rows & jnp.logical_not(mine), tg[p], 0.0))
            nxt.append(_mm(jnp.concatenate([r_eff, t_full], axis=0), sts[h]))
        ys = [jnp.where(low_cols if h % 2 else jnp.logical_not(low_cols), nxt[h][:ch] + ry[(c, h)], 0.0)
              for h in range(R_HEADS)]
        y_chunks.append(jnp.concatenate([ys[2 * j] + ys[2 * j + 1] for j in range(R_HEADS // 2)], axis=1))
        sts = [nxt[h][ch:] + g_full[h] for h in range(R_HEADS)]
        yield
    for h in range(R_HEADS):
        st_ref[h] = sts[h]
    y = jnp.concatenate(y_chunks, axis=0)

    inv_d = 1.0 / hd
    mu = _mm(y, head_ones) * inv_d
    yc = y - mu
    var = _mm(yc * yc, head_ones) * inv_d
    yn = yc * lax.rsqrt(var + R_GN_EPS) * lnx_ref[0:1, :] + lnx_ref[1:2, :]
    bonus = _mm(r * k2 * rk_ref[...], swap_ones) * v
    out_ref[...] = (yn + bonus) * g


def _t5_bucket(dist):
    max_exact = N_BUCKETS // 2
    d = jnp.maximum(dist, 1).astype(F32)
    large = max_exact + (jnp.log(d / max_exact) / math.log(MAX_DISTANCE / max_exact)
                         * (N_BUCKETS - max_exact)).astype(jnp.int32)
    large = jnp.minimum(large, N_BUCKETS - 1)
    return jnp.where(dist < max_exact, dist, large)


def _attn_tables(rel_bias):
    blk = A_BLOCK
    i = jnp.arange(blk)[:, None]
    j = jnp.arange(blk)[None, :]
    top = 3 * blk - 1
    period = 5 * blk
    by_dist_rev = rel_bias.astype(F32)[:, _t5_bucket(jnp.maximum(top - jnp.arange(period), 0))]
    shifted = jnp.tile(by_dist_rev, (1, blk))[:, :blk * (period - 1)].reshape(A_HEADS, blk, period - 1)

    def bias(offset):
        return shifted[:, :, top - offset:top - offset + blk]

    neg = jnp.full((A_HEADS, blk, blk), NEG, F32)
    is_meta = j >= FRONT_PAD
    cur_d = i - j
    prev_d = blk + i - j
    cur = jnp.where(cur_d >= 0, bias(0), NEG)
    prev = jnp.where(prev_d < WINDOW, bias(blk), NEG)
    meta0 = jnp.where(is_meta & (cur_d >= 0), bias(0), NEG)
    meta1 = jnp.where(is_meta, bias(blk), NEG)
    meta2 = jnp.where(is_meta, bias(2 * blk), NEG)
    case0 = jnp.concatenate([meta0, neg, neg], axis=-1)
    case1 = jnp.concatenate([meta1, neg, cur], axis=-1)
    case2 = jnp.concatenate([meta2, prev, cur], axis=-1)
    tab = jnp.stack([case0, case1, case2], axis=0)
    tab = jnp.concatenate([tab[..., blk:], tab[..., FRONT_PAD:blk]], axis=-1)
    nk = tab.shape[-1]
    tab = tab.reshape(3, A_KV_HEADS, A_GROUP, blk, nk)
    return tab.transpose(0, 1, 4, 2, 3).reshape(3, A_KV_HEADS, nk, A_GROUP * blk)


assert A_KV_HEADS * HEAD_DIM == LANES
_PAIRED_HEADS = tuple(kv * A_GROUP + g for g in range(A_GROUP) for kv in range(A_KV_HEADS))


def _pair_heads(w, axis=0):
    split = w.shape[:axis] + (A_KV_HEADS, A_GROUP, HEAD_DIM) + w.shape[axis + 1:]
    return jnp.swapaxes(w.reshape(split), axis, axis + 1).reshape(w.shape)


def _swa_stages(q_ref, kc_ref, kp_ref, km_ref, vc_ref, vp_ref, vm_ref, tab_ref, sink_ref, out_ref):
    scale = HEAD_DIM ** -0.5
    blk = A_BLOCK
    hd = HEAD_DIM
    nb = q_ref.shape[0] // blk
    step = pl.program_id(1)
    kvs = range(A_KV_HEADS)
    upper = _iota2((blk, LANES), 1) >= hd
    yield 4

    v_t = jnp.concatenate([vm_ref[...], vp_ref[...], vc_ref[...]], axis=0).T
    v_meta = v_t[:, FRONT_PAD:blk]
    k_meta = km_ref[FRONT_PAD:, :]
    ones = jnp.ones((hd, 2 * blk + N_META), F32)

    probs = [(i, kv) for i in range(nb) for kv in kvs]
    s_t = {}
    for i in range(nb):
        rows = slice(i * blk, (i + 1) * blk)
        prev = kp_ref[...] if i == 0 else kc_ref[(i - 1) * blk:i * blk, :]
        keys = jnp.concatenate([prev, kc_ref[rows, :], k_meta], axis=0)
        for kv in kvs:
            mine = upper if kv == 1 else jnp.logical_not(upper)
            q = jnp.concatenate([jnp.where(mine, q_ref[rows, g * LANES:(g + 1) * LANES], 0.0)
                                 for g in range(A_GROUP)], axis=0) * scale
            s_t[i, kv] = _mm_nt(keys, q)
    yield
    p_t, esink = {}, {}
    for i, kv in probs:
        case = 2 if i >= 2 else jnp.minimum(step * nb + i, 2)
        sk = s_t[i, kv] + tab_ref[case, kv]
        sink = sink_ref[kv][0:1, :]
        m = jnp.maximum(jnp.max(sk, axis=0, keepdims=True), sink)
        p_t[i, kv] = _bf(jnp.exp(sk - m))
        esink[i, kv] = jnp.exp(sink - m)
    yield
    o_t = {}
    for i, kv in probs:
        vr = slice(kv * hd, (kv + 1) * hd)
        prev = v_t[vr, blk:2 * blk] if i == 0 else v_t[vr, (i + 1) * blk:(i + 2) * blk]
        vals = jnp.concatenate([prev, v_t[vr, (i + 2) * blk:(i + 3) * blk], v_meta[vr]], axis=1)
        r = _mm(jnp.concatenate([vals, ones], axis=0), p_t[i, kv])
        o_t[i, kv] = r[:hd] / (r[hd:] + esink[i, kv])
    yield
    for i in range(nb):
        tiles = [jnp.concatenate([o_t[i, kv][:, g * blk:(g + 1) * blk] for kv in kvs], axis=0).T
                 for g in range(A_GROUP)]
        out_ref[i * blk:(i + 1) * blk, :] = jnp.concatenate(tiles, axis=1)


def _interleave(gens):
    total = [next(g) for g in gens]
    done = [0] * len(gens)
    live = set(range(len(gens)))
    while live:
        i = min(live, key=lambda n: ((done[n] + 1) / total[n], n))
        try:
            next(gens[i])
            done[i] += 1
        except StopIteration:
            live.discard(i)


def _mixers_kernel(*refs, n_m, n_r, n_s):
    m_in, r_in, s_in = refs[:n_m], refs[n_m:n_m + n_r], refs[n_m + n_r:n_m + n_r + n_s]
    ym_ref, yr_ref, ya_ref, cst_ref, m_ref, st_ref = refs[n_m + n_r + n_s:]
    _interleave([_rwkv_stages(*r_in, yr_ref, st_ref),
                 _mlstm_stages(*m_in, ym_ref, cst_ref, m_ref),
                 _swa_stages(*s_in, ya_ref)])


def _mixers(pa, pm, pr, tables, l, sink, bias, norm_w, mu, w0, w2, a0, a2, g2, k_k, k_a, r_k, ln_x):
    b, lp, _ = pa.shape
    blk = A_BLOCK
    tb = _mix_rows(lp)
    nb = tb // blk
    tile = lambda width: pl.BlockSpec((None, tb, width), lambda i, j: (i, j, 0))

    m_in = [pm, bias, norm_w]
    m_specs = [tile(PM_W), _layer_spec(bias, l), _layer_spec(norm_w, l)]

    r_params = [mu, w0, w2, a0, a2, g2, k_k, k_a, r_k, ln_x]
    r_in = [pr, pr] + r_params
    r_specs = [tile(PR_W),
               pl.BlockSpec((None, 8, PR_W), lambda i, j: (i, jnp.maximum(j * (tb // 8) - 1, 0), 0))]
    r_specs += [_layer_spec(a, l) for a in r_params]

    kcol = A_WIDTH // A_KV_WIDTH
    cur_spec = lambda col: pl.BlockSpec((None, tb, A_KV_WIDTH), lambda i, j: (i, j, col))
    prev_spec = lambda col: pl.BlockSpec((None, blk, A_KV_WIDTH),
                                         lambda i, j: (i, jnp.maximum(j * nb - 1, 0), col))
    meta_spec = lambda col: pl.BlockSpec((None, blk, A_KV_WIDTH), lambda i, j: (i, 0, col))
    s_in = [pa] * 7 + [tables, sink]
    s_specs = [tile(A_WIDTH), cur_spec(kcol), prev_spec(kcol), meta_spec(kcol),
               cur_spec(kcol + 1), prev_spec(kcol + 1), meta_spec(kcol + 1),
               pl.BlockSpec(tables.shape, lambda i, j: (0, 0, 0, 0), pipeline_mode=pl.Buffered(1)),
               _layer_spec(sink, l)]

    return pl.pallas_call(
        functools.partial(_mixers_kernel, n_m=len(m_in), n_r=len(r_in), n_s=len(s_in)),
        grid=(b, lp // tb),
        in_specs=m_specs + r_specs + s_specs,
        out_specs=[tile(M_WIDTH), tile(R_WIDTH), tile(A_WIDTH)],
        out_shape=[jax.ShapeDtypeStruct((b, lp, M_WIDTH), F32),
                   jax.ShapeDtypeStruct((b, lp, R_WIDTH), F32),
                   jax.ShapeDtypeStruct((b, lp, A_WIDTH), F32)],
        scratch_shapes=[pltpu.VMEM((M_HEADS // 2, LANES, 2 * LANES), F32),
                        pltpu.VMEM((M_HEADS // 2, 8, LANES), F32),
                        pltpu.VMEM((R_HEADS, LANES, LANES), F32)],
        compiler_params=_cparams(("arbitrary", "arbitrary")),
        name="mixers",
    )(*m_in, *r_in, *s_in)


def _layer_norm_rows(z, ln_ref):
    mu = jnp.mean(z, axis=-1, keepdims=True)
    zc = z - mu
    var = jnp.mean(zc * zc, axis=-1, keepdims=True)
    return zc * lax.rsqrt(var + LN_EPS) * ln_ref[0:1, :] + ln_ref[1:2, :]


FF_CHUNK = 256


def _post_kernel(*refs, n_each, from_x, final, alpha, d_ff):
    groups, pos = [], 0
    for n in n_each:
        groups.append(refs[pos:pos + n])
        pos += n
    w_ref, lnm_ref, wi_ref, wo_ref, lnf_ref, out_ref = refs[pos:]
    ym, yr, ya = (_read_tile(g, False) for g in groups[:3])
    h = _read_tile(groups[3], from_x)
    tm = h.shape[0]
    y_cat = jnp.concatenate([ym.astype(BF16), yr.astype(BF16), ya.astype(BF16)], axis=1)
    mix = jnp.dot(y_cat, w_ref[...], preferred_element_type=F32)
    x = _layer_norm_rows(alpha * h + mix, lnm_ref)
    if not final:
        is_pad = (pl.program_id(1) * tm + _iota2((tm, 1), 0)) < FRONT_PAD
        x = jnp.where(is_pad, 0.0, x)
    xb = x.astype(BF16)
    acc = alpha * x
    for j in range(d_ff // FF_CHUNK):
        cs = slice(j * FF_CHUNK, (j + 1) * FF_CHUNK)
        us = slice(d_ff + j * FF_CHUNK, d_ff + (j + 1) * FF_CHUNK)
        gate = jnp.dot(xb, wi_ref[:, cs], preferred_element_type=F32)
        up = jnp.dot(xb, wi_ref[:, us], preferred_element_type=F32)
        act = (gate * jax.nn.sigmoid(gate) * up).astype(BF16)
        acc += jnp.dot(act, wo_ref[cs, :], preferred_element_type=F32)
    y = _layer_norm_rows(acc, lnf_ref)
    out_ref[...] = y if final else jnp.where(is_pad, 0.0, y)


def _post(ym, yr, ya, h, x, prefix, w, ln_mix, wi, wo, ln_ffn, l, alpha, final):
    b, lp, _ = ym.shape
    seq, d = x.shape[1:]
    d_ff = wo.shape[1]
    assert d_ff % FF_CHUNK == 0
    from_x = h is None
    if final:
        tm = _row_tile(seq)
        groups = [_token_tile_inputs(a, tm) for a in (ym, yr, ya)]
        groups.append(([x], [pl.BlockSpec((None, tm, d), lambda i, j: (i, j, 0))]) if from_x
                      else _token_tile_inputs(h, tm))
        rows_out = seq
    else:
        tm = _row_tile(lp)
        groups = [_padded_tile_inputs(a, None, None, tm) for a in (ym, yr, ya)]
        groups.append(_padded_tile_inputs(h, x, prefix, tm))
        rows_out = lp
    arrays = [a for g in groups for a in g[0]]
    specs = [s for g in groups for s in g[1]]
    const = lambda a: _layer_spec(a, l, pipeline_mode=pl.Buffered(1))
    return pl.pallas_call(
        functools.partial(_post_kernel, n_each=tuple(len(g[0]) for g in groups),
                          from_x=from_x and not final, final=final, alpha=alpha, d_ff=d_ff),
        grid=(b, rows_out // tm),
        in_specs=specs + [const(w), _layer_spec(ln_mix, l), const(wi), const(wo), _layer_spec(ln_ffn, l)],
        out_specs=pl.BlockSpec((None, tm, d), lambda i, j: (i, j, 0)),
        out_shape=jax.ShapeDtypeStruct((b, rows_out, d), F32),
        compiler_params=_cparams(("arbitrary", "arbitrary")),
        name="out_proj_ffn",
    )(*arrays, w, ln_mix, wi, wo, ln_ffn)


def kernel(x, meta_tokens, rel_bias, w_in, m_gate_bias, m_norm_w, r_mu_rkv, r_mu_w, r_mu_a, r_mu_g, r_w0, r_w2, r_a0, r_a2, r_g2, r_k_k, r_k_a, r_r_k, r_ln_x, a_sinks, w_out, ln_mix, w_ff_in, w_ff_out, ln_ffn):
    b, seq, d = x.shape
    depth = w_in.shape[0]
    alpha = (2 * depth) ** 0.25
    assert (TOKEN_START + seq) % A_BLOCK == 0

    lp = TOKEN_START + seq
    prefix = jnp.concatenate([jnp.zeros((FRONT_PAD, d), x.dtype), meta_tokens.astype(x.dtype)], axis=0)
    tables = _attn_tables(rel_bias)

    m_end = PA_W + M_RAW
    rv0 = m_end + 2 * R_WIDTH
    w_pad = jnp.concatenate(
        [_pair_heads(w_in[:, :, :A_WIDTH], 2), w_in[:, :, A_WIDTH:m_end],
         jnp.zeros((depth, d, PM_W - M_RAW), w_in.dtype), w_in[:, :, m_end:rv0],
         _swap_pairs(w_in[:, :, rv0:rv0 + R_WIDTH], 2), w_in[:, :, rv0 + R_WIDTH:]], axis=2).astype(BF16)
    w_o = jnp.concatenate([w_out[:, :M_WIDTH], _swap_pairs(w_out[:, M_WIDTH:M_WIDTH + R_WIDTH], 1),
                           _pair_heads(w_out[:, M_WIDTH + R_WIDTH:], 1)], axis=1).astype(BF16)
    w_fi, w_fo = w_ff_in.astype(BF16), w_ff_out.astype(BF16)
    rows = lambda a: a.reshape(depth, 1, -1)
    mu = jnp.concatenate([r_mu_rkv[:, 0], r_mu_rkv[:, 1], _swap_pairs(r_mu_rkv[:, 2], 1),
                          r_mu_w, r_mu_a, r_mu_g], axis=1)
    bias = jnp.pad(rows(m_gate_bias.astype(F32)), ((0, 0), (0, 0), (0, LANES - 2 * M_HEADS)))
    sink = jnp.broadcast_to(
        jnp.repeat(a_sinks.astype(F32).reshape(depth, A_KV_HEADS, 1, A_GROUP), A_BLOCK, axis=3),
        (depth, A_KV_HEADS, 8, A_GROUP * A_BLOCK))
    mixer_params = (sink, bias, rows(m_norm_w), rows(mu), rows(r_w0), r_w2, rows(r_a0), r_a2,
                    _swap_pairs(r_g2, 2), rows(r_k_k), rows(r_k_a), rows(r_r_k), _swap_pairs(r_ln_x, 2))

    h = None
    for l in range(depth):
        pa, pm, pr = _proj(h, x, prefix, w_pad, l, lp)
        y_m, y_r, y_a = _mixers(pa, pm, pr, tables, l, *mixer_params)
        h = _post(y_m, y_r, y_a, h, x, prefix, w_o, ln_mix, w_fi, w_fo, ln_ffn, l, alpha,
                  final=l == depth - 1)
    return h
```
